```python
import jax, jax.numpy as jnp
from jax import lax
import numpy as np

D_MODEL = 4096
BATCH = 4
SEQ = 4096
DEPTH = 1

N_META = 16
ROPE_THETA = 500000.0
Q_BLOCK = 128
NORM_EPS = 1e-6
MLA_HEADS = 16
MLA_Q_LORA = 1024
MLA_KV_LORA = 512
MLA_NOPE = 128
MLA_ROPE = 64
MLA_V = 128
DSA_HEADS = 16
DSA_KV_HEADS = 4
DSA_HEAD_DIM = 128
DSA_ROT = DSA_HEAD_DIM // 4
IDX_HEADS = 16
IDX_DIM = 64
IDX_ROT = IDX_DIM // 4
DSA_TOPK_MAX = 256
PEER_HEADS = 8
PEER_N_KEYS = 128
PEER_N_EXPERTS = PEER_N_KEYS * PEER_N_KEYS
PEER_KEY_DIM = 256
PEER_HALF = PEER_KEY_DIM // 2
PEER_TOPK = 16
PEER_BLOCK = 64

IN_SPLITS = (
    MLA_Q_LORA,
    MLA_KV_LORA,
    MLA_ROPE,
    DSA_HEADS * DSA_HEAD_DIM,
    DSA_KV_HEADS * DSA_HEAD_DIM,
    DSA_KV_HEADS * DSA_HEAD_DIM,
    IDX_HEADS * IDX_DIM,
    IDX_DIM,
    IDX_HEADS,
    D_MODEL,
    D_MODEL,
)
IN_WIDTH = sum(IN_SPLITS)
BRANCH_A_WIDTH = MLA_HEADS * MLA_V
BRANCH_B_WIDTH = DSA_HEADS * DSA_HEAD_DIM

kernel_name = "hybrid_mla_dsa_peer_block"


def rms_norm(x, g):
    xf = x.astype(jnp.float32)
    y = xf * lax.rsqrt(jnp.mean(xf * xf, axis=-1, keepdims=True) + NORM_EPS)
    return (y * g.astype(jnp.float32)).astype(x.dtype)


def rope_tables(pos, rot_dim):
    inv = ROPE_THETA ** (-jnp.arange(0, rot_dim, 2, dtype=jnp.float32) / rot_dim)
    ang = pos.astype(jnp.float32)[:, None] * inv[None, :]
    return jnp.cos(ang), jnp.sin(ang)


def apply_rope(x, cos, sin, rot_dim):
    xr, xp = x[..., :rot_dim], x[..., rot_dim:]
    x1, x2 = jnp.split(xr.astype(jnp.float32), 2, axis=-1)
    c, s = cos[:, None, :], sin[:, None, :]
    rot = jnp.concatenate([x1 * c - x2 * s, x2 * c + x1 * s], axis=-1).astype(x.dtype)
    return jnp.concatenate([rot, xp], axis=-1)


def to_blocks(a, nb):
    return jnp.swapaxes(a.reshape((a.shape[0], nb, Q_BLOCK) + a.shape[2:]), 0, 1)


def causal_dense_attention(q, k, v, scale):
    B, T, H, _ = q.shape
    nb = T // Q_BLOCK
    kpos = jnp.arange(T)

    def one(args):
        qi, i = args
        qpos = i * Q_BLOCK + jnp.arange(Q_BLOCK)
        s = jnp.einsum('bqhd,bkhd->bhqk', qi, k, preferred_element_type=jnp.float32) * scale
        s = jnp.where((kpos[None, :] <= qpos[:, None])[None, None], s, -jnp.inf)
        p = jax.nn.softmax(s, axis=-1)
        o = jnp.einsum('bhqk,bkhd->bqhd', p, v.astype(jnp.float32))
        return o.astype(q.dtype)

    out = lax.map(one, (to_blocks(q, nb), jnp.arange(nb)))
    return jnp.swapaxes(out, 0, 1).reshape(B, T, -1)


def dsa_attention(q, k, v, q_ix, k_ix, w_ix, topk):
    B, T, H, Dh = q.shape
    G = k.shape[2]
    nb = T // Q_BLOCK
    kpos = jnp.arange(T)
    gather = jax.vmap(lambda src, idx: src[idx])

    def one(args):
        qi, qix, wix, i = args
        qpos = i * Q_BLOCK + jnp.arange(Q_BLOCK)
        dots = jnp.einsum('bqhd,bkd->bqhk', qix, k_ix,
                          preferred_element_type=jnp.float32) * (IDX_DIM ** -0.5)
        score = jnp.einsum('bqh,bqhk->bqk', wix.astype(jnp.float32) * (IDX_HEADS ** -0.5),
                           jax.nn.relu(dots))
        score = jnp.where((kpos[None, :] <= qpos[:, None])[None], score, -jnp.inf)
        _, idx = lax.top_k(score, topk)
        sel_ok = idx <= qpos[None, :, None]
        k_sel = gather(k, idx)
        v_sel = gather(v, idx)
        qg = qi.reshape(B, Q_BLOCK, G, H // G, Dh)
        s = jnp.einsum('bqghd,bqkgd->bqghk', qg, k_sel,
                       preferred_element_type=jnp.float32) * (DSA_HEAD_DIM ** -0.5)
        s = jnp.where(sel_ok[:, :, None, None, :], s, -jnp.inf)
        p = jax.nn.softmax(s, axis=-1)
        o = jnp.einsum('bqghk,bqkgd->bqghd', p, v_sel.astype(jnp.float32))
        return o.reshape(B, Q_BLOCK, H * Dh).astype(q.dtype)

    out = lax.map(one, (to_blocks(q, nb), to_blocks(q_ix, nb), to_blocks(w_ix, nb), jnp.arange(nb)))
    return jnp.swapaxes(out, 0, 1).reshape(B, T, H * Dh)


def hybrid_mixer(hn, w_in, q_norm_g, w_uq, kv_norm_g, w_ukv, w_branch, w_out,
                 rope_mla, rope_dsa, rope_idx, topk):
    B, T, _ = hn.shape
    proj = hn @ w_in
    offs = np.cumsum(IN_SPLITS)[:-1].tolist()
    (c_q, c_kv, k_pe, q_b, k_b, v_b, q_ix, k_ix, w_ix, gate_a, gate_b) = jnp.split(proj, offs, axis=-1)

    q_a = (rms_norm(c_q, q_norm_g) @ w_uq).reshape(B, T, MLA_HEADS, MLA_NOPE + MLA_ROPE)
    q_a = jnp.concatenate([q_a[..., :MLA_NOPE],
                           apply_rope(q_a[..., MLA_NOPE:], *rope_mla, MLA_ROPE)], axis=-1)
    kv = (rms_norm(c_kv, kv_norm_g) @ w_ukv).reshape(B, T, MLA_HEADS, MLA_NOPE + MLA_V)
    k_pe = apply_rope(k_pe[:, :, None, :], *rope_mla, MLA_ROPE)
    k_a = jnp.concatenate([kv[..., :MLA_NOPE],
                           jnp.broadcast_to(k_pe, (B, T, MLA_HEADS, MLA_ROPE))], axis=-1)
    v_a = kv[..., MLA_NOPE:]
    o_a = causal_dense_attention(q_a, k_a, v_a, (MLA_NOPE + MLA_ROPE) ** -0.5)

    q_b = apply_rope(q_b.reshape(B, T, DSA_HEADS, DSA_HEAD_DIM), *rope_dsa, DSA_ROT)
    k_b = apply_rope(k_b.reshape(B, T, DSA_KV_HEADS, DSA_HEAD_DIM), *rope_dsa, DSA_ROT)
    v_b = v_b.reshape(B, T, DSA_KV_HEADS, DSA_HEAD_DIM)
    q_ix = apply_rope(q_ix.reshape(B, T, IDX_HEADS, IDX_DIM), *rope_idx, IDX_ROT)
    k_ix = apply_rope(k_ix[:, :, None, :], *rope_idx, IDX_ROT)[:, :, 0, :]
    o_b = dsa_attention(q_b, k_b, v_b, q_ix, k_ix, w_ix, topk)

    y_a = o_a @ w_branch[:BRANCH_A_WIDTH]
    y_b = o_b @ w_branch[BRANCH_A_WIDTH:]
    merged = jax.nn.sigmoid(gate_a) * y_a + jax.nn.sigmoid(gate_b) * y_b
    return merged @ w_out


def peer_ffn(hn, w_q, sub_keys, u_tab, v_tab):
    B, T, D = hn.shape
    q = (hn @ w_q).reshape(B, T, PEER_HEADS, 2, PEER_HALF)
    s = jnp.einsum('bthcd,hcnd->bthcn', q, sub_keys, preferred_element_type=jnp.float32)
    s1, i1 = lax.top_k(s[..., 0, :], PEER_TOPK)
    s2, i2 = lax.top_k(s[..., 1, :], PEER_TOPK)
    cand_s = (s1[..., :, None] + s2[..., None, :]).reshape(B, T, PEER_HEADS, PEER_TOPK * PEER_TOPK)
    cand_i = (i1[..., :, None] * PEER_N_KEYS + i2[..., None, :]).reshape(B, T, PEER_HEADS, PEER_TOPK * PEER_TOPK)
    top_s, top_pos = lax.top_k(cand_s, PEER_TOPK)
    experts = jnp.take_along_axis(cand_i, top_pos, axis=-1)
    gates = jax.nn.softmax(top_s, axis=-1)

    n_sel = PEER_HEADS * PEER_TOPK
    nblk = (B * T) // PEER_BLOCK
    hb = hn.reshape(nblk, PEER_BLOCK, D)
    eb = experts.reshape(nblk, PEER_BLOCK, n_sel)
    gb = gates.reshape(nblk, PEER_BLOCK, n_sel)

    def one(args):
        h_blk, e_blk, g_blk = args
        u = u_tab[e_blk]
        a = jnp.einsum('pd,ped->pe', h_blk, u, preferred_element_type=jnp.float32)
        a = jax.nn.gelu(a) * g_blk
        v = v_tab[e_blk]
        o = jnp.einsum('pe,ped->pd', a.astype(v.dtype), v, preferred_element_type=jnp.float32)
        return o.astype(h_blk.dtype)

    return lax.map(one, (hb, eb, gb)).reshape(B, T, D)


def setup_inputs(seed: int = 0) -> dict:
    key = jax.random.key(seed)
    ks = jax.random.split(key, 16)
    f32 = jnp.float32

    def nrm(k, shape, fan_in):
        return jax.random.normal(k, shape, f32) * (fan_in ** -0.5)

    def gain(k, shape):
        return 1.0 + 0.05 * jax.random.normal(k, shape, f32)

    return {
        "x": jax.random.normal(ks[0], (BATCH, SEQ, D_MODEL), f32),
        "meta_tokens": jax.random.normal(ks[1], (N_META, D_MODEL), f32),
        "attn_norm_g": gain(ks[2], (DEPTH, D_MODEL)),
        "w_in": nrm(ks[3], (DEPTH, D_MODEL, IN_WIDTH), D_MODEL),
        "q_norm_g": gain(ks[4], (DEPTH, MLA_Q_LORA)),
        "w_uq": nrm(ks[5], (DEPTH, MLA_Q_LORA, MLA_HEADS * (MLA_NOPE + MLA_ROPE)), MLA_Q_LORA),
        "kv_norm_g": gain(ks[6], (DEPTH, MLA_KV_LORA)),
        "w_ukv": nrm(ks[7], (DEPTH, MLA_KV_LORA, MLA_HEADS * (MLA_NOPE + MLA_V)), MLA_KV_LORA),
        "w_branch": nrm(ks[8], (DEPTH, BRANCH_A_WIDTH + BRANCH_B_WIDTH, D_MODEL), BRANCH_A_WIDTH),
        "w_out": nrm(ks[9], (DEPTH, D_MODEL, D_MODEL), D_MODEL),
        "ffn_norm_g": gain(ks[10], (DEPTH, D_MODEL)),
        "peer_w_q": nrm(ks[11], (DEPTH, D_MODEL, PEER_HEADS * PEER_KEY_DIM), D_MODEL),
        "peer_sub_keys": nrm(ks[12], (DEPTH, PEER_HEADS, 2, PEER_N_KEYS, PEER_HALF), PEER_HALF),
        "peer_u": nrm(ks[13], (DEPTH, PEER_N_EXPERTS, D_MODEL), D_MODEL),
        "peer_v": nrm(ks[14], (DEPTH, PEER_N_EXPERTS, D_MODEL), PEER_HEADS * PEER_TOPK),
        "final_norm_g": gain(ks[15], (D_MODEL,)),
    }


def reference(x, meta_tokens, attn_norm_g, w_in, q_norm_g, w_uq, kv_norm_g, w_ukv,
              w_branch, w_out, ffn_norm_g, peer_w_q, peer_sub_keys, peer_u, peer_v,
              final_norm_g):
    B, S, D = x.shape
    T = S + N_META
    T_pad = -(-T // Q_BLOCK) * Q_BLOCK
    topk = min(DSA_TOPK_MAX, S // 4)

    meta = jnp.broadcast_to(meta_tokens.astype(x.dtype)[None], (B, N_META, D))
    h = jnp.concatenate([meta, x, jnp.zeros((B, T_pad - T, D), x.dtype)], axis=1)

    pos = jnp.arange(T_pad)
    rope_mla = rope_tables(pos, MLA_ROPE)
    rope_dsa = rope_tables(pos, DSA_ROT)
    rope_idx = rope_tables(pos, IDX_ROT)

    for l in range(DEPTH):
        h = h + hybrid_mixer(rms_norm(h, attn_norm_g[l]), w_in[l], q_norm_g[l], w_uq[l],
                             kv_norm_g[l], w_ukv[l], w_branch[l], w_out[l],
                             rope_mla, rope_dsa, rope_idx, topk)
        h = h + peer_ffn(rms_norm(h, ffn_norm_g[l]), peer_w_q[l], peer_sub_keys[l],
                         peer_u[l], peer_v[l])

    h = rms_norm(h, final_norm_g)
    return h[:, N_META:N_META + S]
```

```python
import functools

import jax
import jax.numpy as jnp
import numpy as np
from jax import lax
from jax.experimental import pallas as pl
from jax.experimental.pallas import tpu as pltpu

N_META = 16
ROPE_THETA = 500000.0
Q_BLOCK = 128
NORM_EPS = 1e-6
MLA_HEADS = 16
MLA_Q_LORA = 1024
MLA_KV_LORA = 512
MLA_NOPE = 128
MLA_ROPE = 64
MLA_V = 128
DSA_HEADS = 16
DSA_KV_HEADS = 4
DSA_HEAD_DIM = 128
DSA_ROT = DSA_HEAD_DIM // 4
IDX_HEADS = 16
IDX_DIM = 64
IDX_ROT = IDX_DIM // 4
DSA_TOPK_MAX = 256
PEER_HEADS = 8
PEER_N_KEYS = 128
PEER_HALF = 128
PEER_TOPK = 16

LANE = 128
MLA_QK_PAD = 2 * LANE
MASK_NEG = -1e30
INT32_MIN = -(2 ** 31)
MIB = 1024 * 1024

BF16 = jnp.bfloat16
F32 = jnp.float32


def _pick(n, prefs):
    for p in prefs:
        if n % p == 0:
            return p
    raise ValueError(f"no tile in {prefs} divides {n}")


def _params(sem, vmem_mib):
    return pltpu.CompilerParams(dimension_semantics=sem, vmem_limit_bytes=vmem_mib * MIB)


def _nt_dot(a, b):
    return lax.dot_general(a, b, (((1,), (1,)), ((), ())), preferred_element_type=F32)


def _rms(x, g):
    ms = jnp.mean(x * x, axis=-1, keepdims=True)
    return x * lax.rsqrt(ms + NORM_EPS) * g


def _rope(x, c, s_up, s_dn, half):
    return x * c + pltpu.roll(x, half, 1) * s_up + pltpu.roll(x, LANE - half, 1) * s_dn


def _rope_tables(t_pad, block, rot):
    half = rot // 2
    inv = ROPE_THETA ** (-jnp.arange(0, rot, 2, dtype=F32) / rot)
    ang = jnp.arange(t_pad, dtype=F32)[:, None] * inv[None, :]
    cos, sin = jnp.cos(ang), jnp.sin(ang)
    lane = np.arange(LANE) % block
    lo = lane < half
    hi = (lane >= half) & (lane < rot)
    idx = np.where(lo, lane, np.where(hi, lane - half, 0))
    cos_l, sin_l = cos[:, idx], sin[:, idx]
    c = jnp.where(lo | hi, cos_l, 1.0)
    s_up = jnp.where(hi, sin_l, 0.0)
    s_dn = jnp.where(lo, -sin_l, 0.0)
    return c.astype(F32), s_up.astype(F32), s_dn.astype(F32)


def _rmsnorm_kernel(x_ref, g_ref, o_ref):
    o_ref[...] = _rms(x_ref[...].astype(F32), g_ref[...]).astype(o_ref.dtype)


def _rmsnorm(x, g, out_dtype):
    m, d = x.shape
    tm = _pick(m, (256, 128))
    return pl.pallas_call(
        _rmsnorm_kernel,
        grid=(m // tm,),
        in_specs=[pl.BlockSpec((tm, d), lambda i: (i, 0)),
                  pl.BlockSpec((1, d), lambda i: (0, 0))],
        out_specs=pl.BlockSpec((tm, d), lambda i: (i, 0)),
        out_shape=jax.ShapeDtypeStruct((m, d), out_dtype),
        compiler_params=_params(("parallel",), 40),
        name="rmsnorm",
    )(x, g.reshape(1, d).astype(F32))


def _mm_kernel(a_ref, b_ref, o_ref):
    o_ref[...] = jnp.dot(a_ref[...], b_ref[...], preferred_element_type=F32).astype(o_ref.dtype)


def _matmul(a, b, out_dtype, name):
    m, k = a.shape
    _, n = b.shape
    tm = _pick(m, (512, 256, 128))
    tn = _pick(n, (1024, 512, 256, 128))
    return pl.pallas_call(
        _mm_kernel,
        grid=(n // tn, m // tm),
        in_specs=[pl.BlockSpec((tm, k), lambda j, i: (i, 0)),
                  pl.BlockSpec((k, tn), lambda j, i: (0, j))],
        out_specs=pl.BlockSpec((tm, tn), lambda j, i: (i, j)),
        out_shape=jax.ShapeDtypeStruct((m, n), out_dtype),
        compiler_params=_params(("parallel", "parallel"), 48),
        name=name,
    )(a, b)


def _mla_q_kernel(cq_ref, g_ref, w_ref, c_ref, su_ref, sd_ref, o_ref):
    xn = _rms(cq_ref[...], g_ref[...]).astype(BF16)
    q = jnp.dot(xn, w_ref[...], preferred_element_type=F32)
    c, su, sd = c_ref[...], su_ref[...], sd_ref[...]
    for h in range(MLA_HEADS):
        base = h * MLA_QK_PAD
        o_ref[:, base:base + LANE] = q[:, base:base + LANE].astype(BF16)
        o_ref[:, base + LANE:base + 2 * LANE] = _rope(
            q[:, base + LANE:base + 2 * LANE], c, su, sd, MLA_ROPE // 2).astype(BF16)


def _mla_kv_kernel(ckv_ref, g_ref, w_ref, kpe_ref, c_ref, su_ref, sd_ref, k_ref, v_ref):
    xn = _rms(ckv_ref[...], g_ref[...]).astype(BF16)
    kv = jnp.dot(xn, w_ref[...], preferred_element_type=F32)
    kpe = _rope(kpe_ref[...], c_ref[...], su_ref[...], sd_ref[...], MLA_ROPE // 2).astype(BF16)
    nk = MLA_HEADS * MLA_NOPE
    for h in range(MLA_HEADS):
        base = h * MLA_QK_PAD
        k_ref[:, base:base + LANE] = kv[:, h * MLA_NOPE:(h + 1) * MLA_NOPE].astype(BF16)
        k_ref[:, base + LANE:base + 2 * LANE] = kpe
    v_ref[...] = kv[:, nk:].astype(BF16)


def _dsa_prep_kernel(qb_ref, kb_ref, vb_ref, qix_ref, kix_ref, wix_ref,
                     dc_ref, dsu_ref, dsd_ref, ic_ref, isu_ref, isd_ref,
                     qb_o, kb_o, vb_o, qix_o, klo_o, khi_o, w_o):
    dc, dsu, dsd = dc_ref[...], dsu_ref[...], dsd_ref[...]
    ic, isu, isd = ic_ref[...], isu_ref[...], isd_ref[...]
    for h in range(DSA_HEADS):
        sl = slice(h * LANE, (h + 1) * LANE)
        qb_o[:, sl] = _rope(qb_ref[:, sl], dc, dsu, dsd, DSA_ROT // 2).astype(BF16)
    for h in range(DSA_KV_HEADS):
        sl = slice(h * LANE, (h + 1) * LANE)
        kb_o[:, sl] = _rope(kb_ref[:, sl], dc, dsu, dsd, DSA_ROT // 2).astype(BF16)
    vb_o[...] = vb_ref[...].astype(BF16)
    for p in range(IDX_HEADS * IDX_DIM // LANE):
        sl = slice(p * LANE, (p + 1) * LANE)
        qix_o[:, sl] = _rope(qix_ref[:, sl], ic, isu, isd, IDX_ROT // 2).astype(BF16)
    kix = _rope(kix_ref[...], ic, isu, isd, IDX_ROT // 2)
    klo_o[...] = kix.astype(BF16)
    khi_o[...] = pltpu.roll(kix, IDX_DIM, 1).astype(BF16)
    w_o[...] = wix_ref[...] * (IDX_HEADS ** -0.5 * IDX_DIM ** -0.5)


def _softmax_step(s, v, m, l, acc):
    m_new = jnp.maximum(m, jnp.max(s, axis=1, keepdims=True))
    alpha = jnp.exp(m - m_new)
    p = jnp.exp(s - m_new)
    l = alpha * l + jnp.sum(p, axis=1, keepdims=True)
    acc = alpha * acc + jnp.dot(p.astype(BF16), v, preferred_element_type=F32)
    return m_new, l, acc


def _mla_attn_kernel(q_ref, k_ref, v_ref, o_ref, *, tq, scale):
    i = pl.program_id(2)
    q = q_ref[...]

    def chunk(j, carry, diagonal):
        off = pl.multiple_of(j * tq, tq)
        s = _nt_dot(q, k_ref[pl.ds(off, tq), :]) * scale
        if diagonal:
            row = lax.broadcasted_iota(jnp.int32, (tq, tq), 0)
            col = lax.broadcasted_iota(jnp.int32, (tq, tq), 1)
            s = jnp.where(col <= row, s, MASK_NEG)
        return _softmax_step(s, v_ref[pl.ds(off, tq), :], *carry)

    init = (jnp.full((tq, 1), MASK_NEG, F32), jnp.zeros((tq, 1), F32),
            jnp.zeros((tq, MLA_V), F32))
    carry = lax.fori_loop(0, i, lambda j, c: chunk(j, c, False), init)
    _, l, acc = chunk(i, carry, True)
    o_ref[...] = (acc / l).astype(o_ref.dtype)


def _mla_attention(q, k, v, b, t_pad):
    tq = _pick(t_pad, (384, 128))
    scale = (MLA_NOPE + MLA_ROPE) ** -0.5
    q3 = q.reshape(b, t_pad, MLA_HEADS * MLA_QK_PAD)
    k3 = k.reshape(b, t_pad, MLA_HEADS * MLA_QK_PAD)
    v3 = v.reshape(b, t_pad, MLA_HEADS * MLA_V)
    out = pl.pallas_call(
        functools.partial(_mla_attn_kernel, tq=tq, scale=scale),
        grid=(b, MLA_HEADS, t_pad // tq),
        in_specs=[pl.BlockSpec((None, tq, MLA_QK_PAD), lambda bb, h, i: (bb, i, h)),
                  pl.BlockSpec((None, t_pad, MLA_QK_PAD), lambda bb, h, i: (bb, 0, h)),
                  pl.BlockSpec((None, t_pad, MLA_V), lambda bb, h, i: (bb, 0, h))],
        out_specs=pl.BlockSpec((None, tq, MLA_V), lambda bb, h, i: (bb, i, h)),
        out_shape=jax.ShapeDtypeStruct((b, t_pad, MLA_HEADS * MLA_V), BF16),
        compiler_params=_params(("parallel", "parallel", "arbitrary"), 32),
        name="mla_attention",
    )(q3, k3, v3)
    return out.reshape(b * t_pad, MLA_HEADS * MLA_V)


def _dsa_kernel(qb_ref, kb_ref, vb_ref, qix_ref, klo_ref, khi_ref, w_ref, o_ref,
                key_scr, bias_scr, *, tq, tk, topk, scale):
    i = pl.program_id(1)
    n_chunks = ((i + 1) * tq + tk - 1) // tk
    row = i * tq + lax.broadcasted_iota(jnp.int32, (tq, tk), 0)
    col0 = lax.broadcasted_iota(jnp.int32, (tq, tk), 1)
    w = w_ref[...]
    qix = qix_ref[...]

    def score_chunk(c, carry):
        off = pl.multiple_of(c * tk, tk)
        klo = klo_ref[pl.ds(off, tk), :]
        khi = khi_ref[pl.ds(off, tk), :]
        sc = jnp.zeros((tq, tk), F32)
        for h in range(IDX_HEADS):
            qp = qix[:, (h // 2) * LANE:(h // 2 + 1) * LANE]
            d = _nt_dot(qp, klo if h % 2 == 0 else khi)
            sc = sc + w[:, h:h + 1] * jnp.maximum(d, 0.0)
        sc = jnp.where(off + col0 <= row, sc, -jnp.inf)
        bits = pltpu.bitcast(sc, jnp.int32)
        key_scr[:, pl.ds(off, tk)] = bits ^ ((bits >> 31) & jnp.int32(0x7FFFFFFF))
        return carry

    lax.fori_loop(0, n_chunks, score_chunk, 0)

    def bit_step(bi, t):
        cand = t + (jnp.int32(1) << (31 - bi))

        def count_chunk(c, acc):
            off = pl.multiple_of(c * tk, tk)
            ge = jnp.where(key_scr[:, pl.ds(off, tk)] >= cand, 1.0, 0.0)
            for p in range(tk // LANE):
                acc = acc + ge[:, p * LANE:(p + 1) * LANE]
            return acc

        acc = lax.fori_loop(0, n_chunks, count_chunk, jnp.zeros((tq, LANE), F32))
        cnt = jnp.sum(acc, axis=1, keepdims=True)
        return jnp.where(cnt >= topk, cand, t)

    thr = lax.fori_loop(0, 32, bit_step, jnp.full((tq, 1), INT32_MIN, jnp.int32))

    def bias_chunk(c, carry):
        off = pl.multiple_of(c * tk, tk)
        ok = (key_scr[:, pl.ds(off, tk)] >= thr) & (off + col0 <= row)
        bias_scr[:, pl.ds(off, tk)] = jnp.where(ok, 0.0, MASK_NEG)
        return carry

    lax.fori_loop(0, n_chunks, bias_chunk, 0)

    rep = DSA_HEADS // DSA_KV_HEADS
    for g in range(DSA_KV_HEADS):
        qg = jnp.concatenate(
            [qb_ref[:, (g * rep + j) * LANE:(g * rep + j + 1) * LANE] for j in range(rep)], axis=0)
        gsl = slice(g * LANE, (g + 1) * LANE)

        def att_chunk(c, carry, qg=qg, gsl=gsl):
            off = pl.multiple_of(c * tk, tk)
            s = _nt_dot(qg, kb_ref[pl.ds(off, tk), gsl]) * scale
            bias = bias_scr[:, pl.ds(off, tk)]
            s = s + jnp.concatenate([bias] * rep, axis=0)
            return _softmax_step(s, vb_ref[pl.ds(off, tk), gsl], *carry)

        init = (jnp.full((rep * tq, 1), MASK_NEG, F32), jnp.zeros((rep * tq, 1), F32),
                jnp.zeros((rep * tq, DSA_HEAD_DIM), F32))
        _, l, acc = lax.fori_loop(0, n_chunks, att_chunk, init)
        o = acc / l
        for j in range(rep):
            hq = g * rep + j
            o_ref[:, hq * LANE:(hq + 1) * LANE] = o[j * tq:(j + 1) * tq].astype(o_ref.dtype)


def _dsa_attention(qb, kb, vb, qix, klo, khi, w, b, t_pad, topk):
    tq = Q_BLOCK
    tk = _pick(t_pad, (384, 128))
    kvw = DSA_KV_HEADS * DSA_HEAD_DIM
    r3 = lambda a: a.reshape(b, t_pad, a.shape[-1])
    qspec = lambda wd: pl.BlockSpec((None, tq, wd), lambda bb, i: (bb, i, 0))
    kspec = lambda wd: pl.BlockSpec((None, t_pad, wd), lambda bb, i: (bb, 0, 0))
    out = pl.pallas_call(
        functools.partial(_dsa_kernel, tq=tq, tk=tk, topk=topk, scale=DSA_HEAD_DIM ** -0.5),
        grid=(b, t_pad // tq),
        in_specs=[qspec(DSA_HEADS * DSA_HEAD_DIM), kspec(kvw), kspec(kvw),
                  qspec(IDX_HEADS * IDX_DIM), kspec(LANE), kspec(LANE), qspec(LANE)],
        out_specs=qspec(DSA_HEADS * DSA_HEAD_DIM),
        out_shape=jax.ShapeDtypeStruct((b, t_pad, DSA_HEADS * DSA_HEAD_DIM), BF16),
        scratch_shapes=[pltpu.VMEM((tq, t_pad), jnp.int32), pltpu.VMEM((tq, t_pad), F32)],
        compiler_params=_params(("parallel", "arbitrary"), 48),
        name="dsa_attention",
    )(r3(qb), r3(kb), r3(vb), r3(qix), r3(klo), r3(khi), r3(w))
    return out.reshape(b * t_pad, DSA_HEADS * DSA_HEAD_DIM)


def _sigmoid(x):
    return 1.0 / (1.0 + jnp.exp(-x))


def _branch_kernel(oa_ref, ob_ref, wa_ref, wb_ref, ga_ref, gb_ref, o_ref):
    ya = jnp.dot(oa_ref[...], wa_ref[...], preferred_element_type=F32)
    yb = jnp.dot(ob_ref[...], wb_ref[...], preferred_element_type=F32)
    o_ref[...] = (_sigmoid(ga_ref[...]) * ya + _sigmoid(gb_ref[...]) * yb).astype(o_ref.dtype)


def _residual_mm_kernel(a_ref, b_ref, h_ref, o_ref):
    o_ref[...] = h_ref[...] + jnp.dot(a_ref[...], b_ref[...], preferred_element_type=F32)


def _peer_select_kernel(qt_ref, keys_ref, s1_ref, s2_ref, e1_ref, e2_ref, tau_ref):
    neg_inf = -jnp.inf
    taus = []
    for h in range(PEER_HEADS):
        scores, tops = [], []
        for c in range(2):
            sl = slice((2 * h + c) * PEER_HALF, (2 * h + c + 1) * PEER_HALF)
            s = jnp.dot(keys_ref[sl, :], qt_ref[sl, :].astype(BF16), preferred_element_type=F32)
            scores.append(s)
            vals, x = [], s
            for _ in range(PEER_TOPK):
                m = jnp.max(x, axis=0, keepdims=True)
                vals.append(m)
                x = jnp.where(x == m, neg_inf, x)
            tops.append(vals)
        a, bv = tops
        cands = [a[i] + bv[j] for i in range(PEER_TOPK) for j in range(PEER_TOPK)
                 if (i + 1) * (j + 1) <= PEER_TOPK]
        v0 = a[0] + bv[0]
        z = jnp.zeros_like(v0)
        m = v0
        for _ in range(PEER_TOPK):
            m = functools.reduce(jnp.maximum, cands)
            z = z + jnp.exp(m - v0)
            cands = [jnp.where(cd == m, neg_inf, cd) for cd in cands]
        taus.append(m)
        hs = slice(h * PEER_N_KEYS, (h + 1) * PEER_N_KEYS)
        s1_ref[hs, :] = scores[0]
        s2_ref[hs, :] = scores[1]
        e1_ref[hs, :] = jnp.exp(scores[0] - a[0])
        e2_ref[hs, :] = jnp.exp(scores[1] - bv[0]) / z
    tau_ref[...] = jnp.concatenate(taus, axis=0)


def _peer_select(qt):
    rows, n = qt[0].shape
    tm = _pick(n, (256, 128))
    qt_arr, keys = qt
    hk = PEER_HEADS * PEER_N_KEYS
    big = lambda: pl.BlockSpec((hk, tm), lambda i: (0, i))
    return pl.pallas_call(
        _peer_select_kernel,
        grid=(n // tm,),
        in_specs=[pl.BlockSpec((rows, tm), lambda i: (0, i)),
                  pl.BlockSpec(keys.shape, lambda i: (0, 0))],
        out_specs=[big(), big(), big(), big(), pl.BlockSpec((PEER_HEADS, tm), lambda i: (0, i))],
        out_shape=[jax.ShapeDtypeStruct((hk, n), F32)] * 4
        + [jax.ShapeDtypeStruct((PEER_HEADS, n), F32)],
        compiler_params=_params(("parallel",), 40),
        name="peer_select",
    )(qt_arr, keys)


def _gelu_tanh(x):
    c = np.float32(np.sqrt(2.0 / np.pi))
    return x * (0.5 * (1.0 + jnp.tanh(c * (x + 0.044715 * (x * x * x)))))


def _peer_main_kernel(hn_ref, u_ref, vt_ref, s1_ref, s2_ref, e1_ref, e2_ref, tau_ref, o_ref,
                      a_scr, *, rows_per_tile):
    j = pl.program_id(1)
    a = jnp.dot(u_ref[...], hn_ref[...], preferred_element_type=F32)
    for rr in range(rows_per_tile):
        r = j * rows_per_tile + rr
        gate = None
        for h in range(PEER_HEADS):
            hs = slice(h * PEER_N_KEYS, (h + 1) * PEER_N_KEYS)
            s1_row = s1_ref[pl.ds(h * PEER_N_KEYS + r, 1), :]
            e1_row = e1_ref[pl.ds(h * PEER_N_KEYS + r, 1), :]
            hit = (s2_ref[hs, :] + s1_row) >= tau_ref[h:h + 1, :]
            term = jnp.where(hit, e2_ref[hs, :] * e1_row, 0.0)
            gate = term if gate is None else gate + term
        rs = slice(rr * PEER_N_KEYS, (rr + 1) * PEER_N_KEYS)
        a_scr[rs, :] = (_gelu_tanh(a[rs, :]) * gate).astype(BF16)
    contrib = jnp.dot(vt_ref[...], a_scr[...], preferred_element_type=F32)

    @pl.when(j == 0)
    def _():
        o_ref[...] = contrib

    @pl.when(j > 0)
    def _():
        o_ref[...] += contrib


def _peer_main(hn_t, u, vt, s1, s2, e1, e2, tau):
    d, n = hn_t.shape
    n_exp = u.shape[0]
    tm = _pick(n, (512, 256, 128))
    te = 512
    hk = PEER_HEADS * PEER_N_KEYS
    tok = lambda rows: pl.BlockSpec((rows, tm), lambda i, j: (0, i), pipeline_mode=pl.Buffered(1))
    return pl.pallas_call(
        functools.partial(_peer_main_kernel, rows_per_tile=te // PEER_N_KEYS),
        grid=(n // tm, n_exp // te),
        in_specs=[tok(d),
                  pl.BlockSpec((te, d), lambda i, j: (j, 0)),
                  pl.BlockSpec((d, te), lambda i, j: (0, j)),
                  tok(hk), tok(hk), tok(hk), tok(hk), tok(PEER_HEADS)],
        out_specs=pl.BlockSpec((d, tm), lambda i, j: (0, i)),
        out_shape=jax.ShapeDtypeStruct((d, n), F32),
        scratch_shapes=[pltpu.VMEM((te, tm), BF16)],
        compiler_params=_params(("parallel", "arbitrary"), 58),
        name="peer_main",
    )(hn_t, u, vt, s1, s2, e1, e2, tau)


def _final_kernel(h_ref, p_ref, g_ref, o_ref):
    o_ref[...] = _rms(h_ref[...] + p_ref[...], g_ref[...])


def _in_proj_layout(d):
    segs = [("gate_a", d), ("gate_b", d), ("q_b", 2048), ("c_q", 1024), ("q_ix", 1024),
            ("c_kv", 512), ("k_b", 512), ("v_b", 512), ("k_pe", LANE), ("k_ix", LANE),
            ("w_ix", LANE)]
    segs.sort(key=lambda s: -s[1])
    offs, o = {}, 0
    for name, wd in segs:
        assert o % wd == 0
        offs[name] = (o, wd)
        o += wd
    total = -(-o // 1024) * 1024
    return offs, total


def _build_w_in(w_in, d):
    src = {}
    o = 0
    for name, wd in (("c_q", MLA_Q_LORA), ("c_kv", MLA_KV_LORA), ("k_pe", MLA_ROPE),
                     ("q_b", 2048), ("k_b", 512), ("v_b", 512), ("q_ix", 1024),
                     ("k_ix", IDX_DIM), ("w_ix", IDX_HEADS), ("gate_a", d), ("gate_b", d)):
        src[name] = w_in[:, o:o + wd]
        o += wd
    offs, total = _in_proj_layout(d)
    cols = []
    o = 0
    for name, (off, wd) in sorted(offs.items(), key=lambda kv: kv[1][0]):
        blk = src[name]
        cols.append(jnp.pad(blk, ((0, 0), (0, wd - blk.shape[1]))))
        o = off + wd
    cols.append(jnp.zeros((d, total - o), w_in.dtype))
    return jnp.concatenate(cols, axis=1).astype(BF16), offs


def kernel(x, meta_tokens, attn_norm_g, w_in, q_norm_g, w_uq, kv_norm_g, w_ukv, w_branch, w_out,
           ffn_norm_g, peer_w_q, peer_sub_keys, peer_u, peer_v, final_norm_g):
    b, s, d = x.shape
    assert attn_norm_g.shape[0] == 1, "single-layer block"
    assert s % Q_BLOCK == 0 and d % 512 == 0 and (d & (d - 1)) == 0
    t = s + N_META
    t_pad = -(-t // Q_BLOCK) * Q_BLOCK
    topk = min(DSA_TOPK_MAX, s // 4)
    n = b * t_pad

    meta = jnp.broadcast_to(meta_tokens.astype(x.dtype)[None], (b, N_META, d))
    h0 = jnp.concatenate([meta, x, jnp.zeros((b, t_pad - t, d), x.dtype)], axis=1).reshape(n, d)

    w_in_p, offs = _build_w_in(w_in[0], d)
    wq = w_uq[0].reshape(MLA_Q_LORA, MLA_HEADS, MLA_NOPE + MLA_ROPE)
    wq = jnp.pad(wq, ((0, 0), (0, 0), (0, MLA_QK_PAD - MLA_NOPE - MLA_ROPE)))
    wq = wq.reshape(MLA_Q_LORA, MLA_HEADS * MLA_QK_PAD).astype(BF16)
    wkv = w_ukv[0].reshape(MLA_KV_LORA, MLA_HEADS, MLA_NOPE + MLA_V)
    wkv = jnp.concatenate([wkv[:, :, :MLA_NOPE].reshape(MLA_KV_LORA, -1),
                           wkv[:, :, MLA_NOPE:].reshape(MLA_KV_LORA, -1)], axis=1).astype(BF16)
    w_br = w_branch[0].astype(BF16)
    w_o = w_out[0].astype(BF16)
    w_pq_t = peer_w_q[0].T.astype(BF16)
    keys = peer_sub_keys[0].reshape(PEER_HEADS * 2 * PEER_N_KEYS, PEER_HALF).astype(BF16)
    u_tab = peer_u[0].astype(BF16)
    v_tab_t = peer_v[0].T.astype(BF16)

    mla_tabs = _rope_tables(t_pad, LANE, MLA_ROPE)
    dsa_tabs = _rope_tables(t_pad, LANE, DSA_ROT)
    idx_tabs = _rope_tables(t_pad, IDX_DIM, IDX_ROT)

    hn = _rmsnorm(h0, attn_norm_g[0], BF16)
    proj = _matmul(hn, w_in_p, F32, "in_proj")

    tm = _pick(t_pad, (384, 128))
    n_pos = t_pad // tm
    row = lambda wd, off: pl.BlockSpec((tm, wd), lambda i, o=off // wd: (i, o))
    seg = lambda name: row(offs[name][1], offs[name][0])
    tab = pl.BlockSpec((tm, LANE), lambda i: (i % n_pos, 0))
    full = lambda a: pl.BlockSpec(a.shape, lambda i: (0,) * a.ndim)
    out_row = lambda wd: pl.BlockSpec((tm, wd), lambda i: (i, 0))
    sds = lambda wd, dt: jax.ShapeDtypeStruct((n, wd), dt)

    qg = q_norm_g[0].reshape(1, -1).astype(F32)
    q_a = pl.pallas_call(
        _mla_q_kernel, grid=(n // tm,),
        in_specs=[seg("c_q"), full(qg), full(wq), tab, tab, tab],
        out_specs=out_row(MLA_HEADS * MLA_QK_PAD),
        out_shape=sds(MLA_HEADS * MLA_QK_PAD, BF16),
        compiler_params=_params(("parallel",), 48), name="mla_q_prep",
    )(proj, qg, wq, *mla_tabs)

    kg = kv_norm_g[0].reshape(1, -1).astype(F32)
    k_a, v_a = pl.pallas_call(
        _mla_kv_kernel, grid=(n // tm,),
        in_specs=[seg("c_kv"), full(kg), full(wkv), seg("k_pe"), tab, tab, tab],
        out_specs=[out_row(MLA_HEADS * MLA_QK_PAD), out_row(MLA_HEADS * MLA_V)],
        out_shape=[sds(MLA_HEADS * MLA_QK_PAD, BF16), sds(MLA_HEADS * MLA_V, BF16)],
        compiler_params=_params(("parallel",), 48), name="mla_kv_prep",
    )(proj, kg, wkv, proj, *mla_tabs)

    kvw = DSA_KV_HEADS * DSA_HEAD_DIM
    q_b, k_b, v_b, q_ix, k_lo, k_hi, w_ix = pl.pallas_call(
        _dsa_prep_kernel, grid=(n // tm,),
        in_specs=[seg("q_b"), seg("k_b"), seg("v_b"), seg("q_ix"), seg("k_ix"), seg("w_ix")]
        + [tab] * 6,
        out_specs=[out_row(2048), out_row(kvw), out_row(kvw), out_row(1024),
                   out_row(LANE), out_row(LANE), out_row(LANE)],
        out_shape=[sds(2048, BF16), sds(kvw, BF16), sds(kvw, BF16), sds(1024, BF16),
                   sds(LANE, BF16), sds(LANE, BF16), sds(LANE, F32)],
        compiler_params=_params(("parallel",), 48), name="dsa_prep",
    )(proj, proj, proj, proj, proj, proj, *dsa_tabs, *idx_tabs)

    o_a = _mla_attention(q_a, k_a, v_a, b, t_pad)
    o_b = _dsa_attention(q_b, k_b, v_b, q_ix, k_lo, k_hi, w_ix, b, t_pad, topk)

    tmm = _pick(n, (512, 256, 128))
    tn = _pick(d, (1024, 512))
    ga_off, gb_off = offs["gate_a"][0] // tn, offs["gate_b"][0] // tn
    bw = MLA_HEADS * MLA_V
    merged = pl.pallas_call(
        _branch_kernel, grid=(d // tn, n // tmm),
        in_specs=[pl.BlockSpec((tmm, bw), lambda j, i: (i, 0)),
                  pl.BlockSpec((tmm, bw), lambda j, i: (i, 0)),
                  pl.BlockSpec((bw, tn), lambda j, i: (0, j)),
                  pl.BlockSpec((bw, tn), lambda j, i: (1, j)),
                  pl.BlockSpec((tmm, tn), lambda j, i: (i, ga_off + j)),
                  pl.BlockSpec((tmm, tn), lambda j, i: (i, gb_off + j))],
        out_specs=pl.BlockSpec((tmm, tn), lambda j, i: (i, j)),
        out_shape=jax.ShapeDtypeStruct((n, d), BF16),
        compiler_params=_params(("parallel", "parallel"), 48), name="branch_merge",
    )(o_a, o_b, w_br, w_br, proj, proj)

    h1 = pl.pallas_call(
        _residual_mm_kernel, grid=(d // tn, n // tmm),
        in_specs=[pl.BlockSpec((tmm, d), lambda j, i: (i, 0)),
                  pl.BlockSpec((d, tn), lambda j, i: (0, j)),
                  pl.BlockSpec((tmm, tn), lambda j, i: (i, j))],
        out_specs=pl.BlockSpec((tmm, tn), lambda j, i: (i, j)),
        out_shape=jax.ShapeDtypeStruct((n, d), F32),
        compiler_params=_params(("parallel", "parallel"), 48), name="out_proj",
    )(merged, w_o, h0)

    h1r = h1.reshape(b, t_pad, d)[:, N_META:N_META + s].reshape(b * s, d)
    hn2_t = _rmsnorm(h1r, ffn_norm_g[0], BF16).T
    q_t = _matmul(w_pq_t, hn2_t, F32, "peer_query")
    s1, s2, e1, e2, tau = _peer_select((q_t, keys))
    peer_t = _peer_main(hn2_t, u_tab, v_tab_t, s1, s2, e1, e2, tau)

    np_tok = b * s
    tf = _pick(np_tok, (256, 128))
    out = pl.pallas_call(
        _final_kernel, grid=(np_tok // tf,),
        in_specs=[pl.BlockSpec((tf, d), lambda i: (i, 0)), pl.BlockSpec((tf, d), lambda i: (i, 0)),
                  pl.BlockSpec((1, d), lambda i: (0, 0))],
        out_specs=pl.BlockSpec((tf, d), lambda i: (i, 0)),
        out_shape=jax.ShapeDtypeStruct((np_tok, d), F32),
        compiler_params=_params(("parallel",), 40), name="final_norm",
    )(h1r, peer_t.T, final_norm_g.reshape(1, d).astype(F32))
    return out.reshape(b, s, d)
```

```python
import functools

import jax
import jax.numpy as jnp
import numpy as np
from jax import lax
from jax.experimental import pallas as pl
from jax.experimental.pallas import tpu as pltpu

N_META = 16
ROPE_THETA = 500000.0
Q_BLOCK = 128
NORM_EPS = 1e-6
MLA_HEADS = 16
MLA_Q_LORA = 1024
MLA_KV_LORA = 512
MLA_NOPE = 128
MLA_ROPE = 64
MLA_V = 128
DSA_HEADS = 16
DSA_KV_HEADS = 4
DSA_HEAD_DIM = 128
DSA_ROT = DSA_HEAD_DIM // 4
IDX_HEADS = 16
IDX_DIM = 64
IDX_ROT = IDX_DIM // 4
DSA_TOPK_MAX = 256
PEER_HEADS = 8
PEER_N_KEYS = 128
PEER_HALF = 128
PEER_TOPK = 16

LANE = 128
MXU_WIDTH = 256
GATE_ROWS = 16
MLA_QK_PAD = 2 * LANE
MASK_NEG = -1e30
INT32_MIN = -(2 ** 31)
MIB = 1024 * 1024

BF16 = jnp.bfloat16
F32 = jnp.float32


def _pick(n, prefs):
    for p in prefs:
        if n % p == 0:
            return p
    raise ValueError(f"no tile in {prefs} divides {n}")


def _params(sem, vmem_mib):
    return pltpu.CompilerParams(dimension_semantics=sem, vmem_limit_bytes=vmem_mib * MIB)


def _nt_dot(a, b):
    return lax.dot_general(a, b, (((1,), (1,)), ((), ())), preferred_element_type=F32)


def _rms(x, g):
    ms = jnp.mean(x * x, axis=-1, keepdims=True)
    return x * lax.rsqrt(ms + NORM_EPS) * g


def _rope(x, c, s_up, s_dn, half):
    return x * c + pltpu.roll(x, half, 1) * s_up + pltpu.roll(x, LANE - half, 1) * s_dn


def _rope_tables(t_pad, block, rot):
    half = rot // 2
    inv = ROPE_THETA ** (-jnp.arange(0, rot, 2, dtype=F32) / rot)
    ang = jnp.arange(t_pad, dtype=F32)[:, None] * inv[None, :]
    cos, sin = jnp.cos(ang), jnp.sin(ang)
    lane = np.arange(LANE) % block
    lo = lane < half
    hi = (lane >= half) & (lane < rot)
    idx = np.where(lo, lane, np.where(hi, lane - half, 0))
    cos_l, sin_l = cos[:, idx], sin[:, idx]
    c = jnp.where(lo | hi, cos_l, 1.0)
    s_up = jnp.where(hi, sin_l, 0.0)
    s_dn = jnp.where(lo, -sin_l, 0.0)
    return c.astype(F32), s_up.astype(F32), s_dn.astype(F32)


def _rmsnorm_kernel(x_ref, g_ref, o_ref):
    o_ref[...] = _rms(x_ref[...].astype(F32), g_ref[...]).astype(o_ref.dtype)


def _rmsnorm(x, g, out_dtype):
    m, d = x.shape
    tm = _pick(m, (256, 128))
    return pl.pallas_call(
        _rmsnorm_kernel,
        grid=(m // tm,),
        in_specs=[pl.BlockSpec((tm, d), lambda i: (i, 0)),
                  pl.BlockSpec((1, d), lambda i: (0, 0))],
        out_specs=pl.BlockSpec((tm, d), lambda i: (i, 0)),
        out_shape=jax.ShapeDtypeStruct((m, d), out_dtype),
        compiler_params=_params(("parallel",), 40),
        name="rmsnorm",
    )(x, g.reshape(1, d).astype(F32))


def _mm_kernel(a_ref, b_ref, o_ref):
    o_ref[...] = jnp.dot(a_ref[...], b_ref[...], preferred_element_type=F32).astype(o_ref.dtype)


def _matmul(a, b, out_dtype, name):
    m, k = a.shape
    _, n = b.shape
    tm = _pick(m, (512, 256, 128))
    tn = _pick(n, (1024, 512, 256, 128))
    return pl.pallas_call(
        _mm_kernel,
        grid=(n // tn, m // tm),
        in_specs=[pl.BlockSpec((tm, k), lambda j, i: (i, 0)),
                  pl.BlockSpec((k, tn), lambda j, i: (0, j))],
        out_specs=pl.BlockSpec((tm, tn), lambda j, i: (i, j)),
        out_shape=jax.ShapeDtypeStruct((m, n), out_dtype),
        compiler_params=_params(("parallel", "parallel"), 48),
        name=name,
    )(a, b)


def _mla_q_kernel(cq_ref, g_ref, w_ref, c_ref, su_ref, sd_ref, o_ref):
    xn = _rms(cq_ref[...], g_ref[...]).astype(BF16)
    q = jnp.dot(xn, w_ref[...], preferred_element_type=F32)
    c, su, sd = c_ref[...], su_ref[...], sd_ref[...]
    for h in range(MLA_HEADS):
        base = h * MLA_QK_PAD
        o_ref[:, base:base + LANE] = q[:, base:base + LANE].astype(BF16)
        o_ref[:, base + LANE:base + 2 * LANE] = _rope(
            q[:, base + LANE:base + 2 * LANE], c, su, sd, MLA_ROPE // 2).astype(BF16)


def _mla_kv_kernel(ckv_ref, g_ref, w_ref, kpe_ref, c_ref, su_ref, sd_ref, k_ref, v_ref):
    xn = _rms(ckv_ref[...], g_ref[...]).astype(BF16)
    kv = jnp.dot(xn, w_ref[...], preferred_element_type=F32)
    kpe = _rope(kpe_ref[...], c_ref[...], su_ref[...], sd_ref[...], MLA_ROPE // 2).astype(BF16)
    nk = MLA_HEADS * MLA_NOPE
    for h in range(MLA_HEADS):
        base = h * MLA_QK_PAD
        k_ref[:, base:base + LANE] = kv[:, h * MLA_NOPE:(h + 1) * MLA_NOPE].astype(BF16)
        k_ref[:, base + LANE:base + 2 * LANE] = kpe
    v_ref[...] = kv[:, nk:].astype(BF16)


def _dsa_prep_kernel(qb_ref, kb_ref, vb_ref, qix_ref, kix_ref, wix_ref,
                     dc_ref, dsu_ref, dsd_ref, ic_ref, isu_ref, isd_ref,
                     qb_o, kb_o, vb_o, qix_o, klo_o, khi_o, w_o):
    dc, dsu, dsd = dc_ref[...], dsu_ref[...], dsd_ref[...]
    ic, isu, isd = ic_ref[...], isu_ref[...], isd_ref[...]
    for h in range(DSA_HEADS):
        sl = slice(h * LANE, (h + 1) * LANE)
        qb_o[:, sl] = _rope(qb_ref[:, sl], dc, dsu, dsd, DSA_ROT // 2).astype(BF16)
    for h in range(DSA_KV_HEADS):
        sl = slice(h * LANE, (h + 1) * LANE)
        kb_o[:, sl] = _rope(kb_ref[:, sl], dc, dsu, dsd, DSA_ROT // 2).astype(BF16)
    vb_o[...] = vb_ref[...].astype(BF16)
    for p in range(IDX_HEADS * IDX_DIM // LANE):
        sl = slice(p * LANE, (p + 1) * LANE)
        qix_o[:, sl] = _rope(qix_ref[:, sl], ic, isu, isd, IDX_ROT // 2).astype(BF16)
    kix = _rope(kix_ref[...], ic, isu, isd, IDX_ROT // 2)
    klo_o[...] = kix.astype(BF16)
    khi_o[...] = pltpu.roll(kix, IDX_DIM, 1).astype(BF16)
    w_o[...] = wix_ref[...] * (IDX_HEADS ** -0.5 * IDX_DIM ** -0.5)


def _softmax_step(s, v, m, l, acc):
    m_new = jnp.maximum(m, jnp.max(s, axis=1, keepdims=True))
    alpha = jnp.exp(m - m_new)
    p = jnp.exp(s - m_new)
    l = alpha * l + jnp.sum(p, axis=1, keepdims=True)
    acc = alpha * acc + jnp.dot(p.astype(BF16), v, preferred_element_type=F32)
    return m_new, l, acc


def _mla_attn_kernel(q_ref, k_ref, v_ref, o_ref, *, tq, scale):
    i = pl.program_id(2)
    q = q_ref[...]

    def chunk(j, carry, diagonal):
        off = pl.multiple_of(j * tq, tq)
        s = _nt_dot(q, k_ref[pl.ds(off, tq), :]) * scale
        if diagonal:
            row = lax.broadcasted_iota(jnp.int32, (tq, tq), 0)
            col = lax.broadcasted_iota(jnp.int32, (tq, tq), 1)
            s = jnp.where(col <= row, s, MASK_NEG)
        return _softmax_step(s, v_ref[pl.ds(off, tq), :], *carry)

    init = (jnp.full((tq, 1), MASK_NEG, F32), jnp.zeros((tq, 1), F32),
            jnp.zeros((tq, MLA_V), F32))
    carry = lax.fori_loop(0, i, lambda j, c: chunk(j, c, False), init)
    _, l, acc = chunk(i, carry, True)
    o_ref[...] = (acc / l).astype(o_ref.dtype)


def _mla_attention(q, k, v, b, t_pad):
    tq = _pick(t_pad, (384, 128))
    scale = (MLA_NOPE + MLA_ROPE) ** -0.5
    q3 = q.reshape(b, t_pad, MLA_HEADS * MLA_QK_PAD)
    k3 = k.reshape(b, t_pad, MLA_HEADS * MLA_QK_PAD)
    v3 = v.reshape(b, t_pad, MLA_HEADS * MLA_V)
    out = pl.pallas_call(
        functools.partial(_mla_attn_kernel, tq=tq, scale=scale),
        grid=(b, MLA_HEADS, t_pad // tq),
        in_specs=[pl.BlockSpec((None, tq, MLA_QK_PAD), lambda bb, h, i: (bb, i, h)),
                  pl.BlockSpec((None, t_pad, MLA_QK_PAD), lambda bb, h, i: (bb, 0, h)),
                  pl.BlockSpec((None, t_pad, MLA_V), lambda bb, h, i: (bb, 0, h))],
        out_specs=pl.BlockSpec((None, tq, MLA_V), lambda bb, h, i: (bb, i, h)),
        out_shape=jax.ShapeDtypeStruct((b, t_pad, MLA_HEADS * MLA_V), BF16),
        compiler_params=_params(("parallel", "parallel", "arbitrary"), 32),
        name="mla_attention",
    )(q3, k3, v3)
    return out.reshape(b * t_pad, MLA_HEADS * MLA_V)


def _dsa_kernel(qb_ref, kb_ref, vb_ref, qix_ref, klo_ref, khi_ref, w_ref, o_ref,
                key_scr, bias_scr, *, tq, tk, topk, scale):
    i = pl.program_id(1)
    n_chunks = ((i + 1) * tq + tk - 1) // tk
    row = i * tq + lax.broadcasted_iota(jnp.int32, (tq, tk), 0)
    col0 = lax.broadcasted_iota(jnp.int32, (tq, tk), 1)
    w = w_ref[...]
    qix = qix_ref[...]

    def score_chunk(c, carry):
        off = pl.multiple_of(c * tk, tk)
        klo = klo_ref[pl.ds(off, tk), :]
        khi = khi_ref[pl.ds(off, tk), :]
        sc = jnp.zeros((tq, tk), F32)
        for h in range(IDX_HEADS):
            qp = qix[:, (h // 2) * LANE:(h // 2 + 1) * LANE]
            d = _nt_dot(qp, klo if h % 2 == 0 else khi)
            sc = sc + w[:, h:h + 1] * jnp.maximum(d, 0.0)
        sc = jnp.where(off + col0 <= row, sc, -jnp.inf)
        bits = pltpu.bitcast(sc, jnp.int32)
        key_scr[:, pl.ds(off, tk)] = bits ^ ((bits >> 31) & jnp.int32(0x7FFFFFFF))
        return carry

    lax.fori_loop(0, n_chunks, score_chunk, 0)

    def bit_step(bi, t):
        cand = t + (jnp.int32(1) << (31 - bi))

        def count_chunk(c, acc):
            off = pl.multiple_of(c * tk, tk)
            ge = jnp.where(key_scr[:, pl.ds(off, tk)] >= cand, 1.0, 0.0)
            for p in range(tk // LANE):
                acc = acc + ge[:, p * LANE:(p + 1) * LANE]
            return acc

        acc = lax.fori_loop(0, n_chunks, count_chunk, jnp.zeros((tq, LANE), F32))
        cnt = jnp.sum(acc, axis=1, keepdims=True)
        return jnp.where(cnt >= topk, cand, t)

    thr = lax.fori_loop(0, 32, bit_step, jnp.full((tq, 1), INT32_MIN, jnp.int32))

    def bias_chunk(c, carry):
        off = pl.multiple_of(c * tk, tk)
        ok = (key_scr[:, pl.ds(off, tk)] >= thr) & (off + col0 <= row)
        bias_scr[:, pl.ds(off, tk)] = jnp.where(ok, 0.0, MASK_NEG)
        return carry

    lax.fori_loop(0, n_chunks, bias_chunk, 0)

    rep = DSA_HEADS // DSA_KV_HEADS
    for g in range(DSA_KV_HEADS):
        qg = jnp.concatenate(
            [qb_ref[:, (g * rep + j) * LANE:(g * rep + j + 1) * LANE] for j in range(rep)], axis=0)
        gsl = slice(g * LANE, (g + 1) * LANE)

        def att_chunk(c, carry, qg=qg, gsl=gsl):
            off = pl.multiple_of(c * tk, tk)
            s = _nt_dot(qg, kb_ref[pl.ds(off, tk), gsl]) * scale
            bias = bias_scr[:, pl.ds(off, tk)]
            s = s + jnp.concatenate([bias] * rep, axis=0)
            return _softmax_step(s, vb_ref[pl.ds(off, tk), gsl], *carry)

        init = (jnp.full((rep * tq, 1), MASK_NEG, F32), jnp.zeros((rep * tq, 1), F32),
                jnp.zeros((rep * tq, DSA_HEAD_DIM), F32))
        _, l, acc = lax.fori_loop(0, n_chunks, att_chunk, init)
        o = acc / l
        for j in range(rep):
            hq = g * rep + j
            o_ref[:, hq * LANE:(hq + 1) * LANE] = o[j * tq:(j + 1) * tq].astype(o_ref.dtype)


def _dsa_attention(qb, kb, vb, qix, klo, khi, w, b, t_pad, topk):
    tq = Q_BLOCK
    tk = _pick(t_pad, (384, 128))
    kvw = DSA_KV_HEADS * DSA_HEAD_DIM
    r3 = lambda a: a.reshape(b, t_pad, a.shape[-1])
    qspec = lambda wd: pl.BlockSpec((None, tq, wd), lambda bb, i: (bb, i, 0))
    kspec = lambda wd: pl.BlockSpec((None, t_pad, wd), lambda bb, i: (bb, 0, 0))
    out = pl.pallas_call(
        functools.partial(_dsa_kernel, tq=tq, tk=tk, topk=topk, scale=DSA_HEAD_DIM ** -0.5),
        grid=(b, t_pad // tq),
        in_specs=[qspec(DSA_HEADS * DSA_HEAD_DIM), kspec(kvw), kspec(kvw),
                  qspec(IDX_HEADS * IDX_DIM), kspec(LANE), kspec(LANE), qspec(LANE)],
        out_specs=qspec(DSA_HEADS * DSA_HEAD_DIM),
        out_shape=jax.ShapeDtypeStruct((b, t_pad, DSA_HEADS * DSA_HEAD_DIM), BF16),
        scratch_shapes=[pltpu.VMEM((tq, t_pad), jnp.int32), pltpu.VMEM((tq, t_pad), F32)],
        compiler_params=_params(("parallel", "arbitrary"), 48),
        name="dsa_attention",
    )(r3(qb), r3(kb), r3(vb), r3(qix), r3(klo), r3(khi), r3(w))
    return out.reshape(b * t_pad, DSA_HEADS * DSA_HEAD_DIM)


def _sigmoid(x):
    return 1.0 / (1.0 + jnp.exp(-x))


def _branch_kernel(oa_ref, ob_ref, wa_ref, wb_ref, ga_ref, gb_ref, o_ref):
    ya = jnp.dot(oa_ref[...], wa_ref[...], preferred_element_type=F32)
    yb = jnp.dot(ob_ref[...], wb_ref[...], preferred_element_type=F32)
    o_ref[...] = (_sigmoid(ga_ref[...]) * ya + _sigmoid(gb_ref[...]) * yb).astype(o_ref.dtype)


def _residual_mm_kernel(a_ref, b_ref, h_ref, o_ref):
    o_ref[...] = h_ref[...] + jnp.dot(a_ref[...], b_ref[...], preferred_element_type=F32)


def _peer_select_kernel(qt_ref, keys_ref, s1_ref, s2_ref, e1_ref, e2_ref, tau_ref):
    neg_inf = -jnp.inf
    taus = []
    for h in range(PEER_HEADS):
        scores, tops = [], []
        for c in range(2):
            sl = slice((2 * h + c) * PEER_HALF, (2 * h + c + 1) * PEER_HALF)
            s = jnp.dot(keys_ref[sl, :], qt_ref[sl, :].astype(BF16), preferred_element_type=F32)
            scores.append(s)
            vals, x = [], s
            for _ in range(PEER_TOPK):
                m = jnp.max(x, axis=0, keepdims=True)
                vals.append(m)
                x = jnp.where(x == m, neg_inf, x)
            tops.append(vals)
        a, bv = tops
        cands = [a[i] + bv[j] for i in range(PEER_TOPK) for j in range(PEER_TOPK)
                 if (i + 1) * (j + 1) <= PEER_TOPK]
        v0 = a[0] + bv[0]
        z = jnp.zeros_like(v0)
        m = v0
        for _ in range(PEER_TOPK):
            m = functools.reduce(jnp.maximum, cands)
            z = z + jnp.exp(m - v0)
            cands = [jnp.where(cd == m, neg_inf, cd) for cd in cands]
        taus.append(m)
        hs = slice(h * PEER_N_KEYS, (h + 1) * PEER_N_KEYS)
        s1_ref[hs, :] = scores[0]
        s2_ref[hs, :] = scores[1]
        e1_ref[hs, :] = jnp.exp(scores[0] - a[0])
        e2_ref[hs, :] = jnp.exp(scores[1] - bv[0]) / z
    tau_ref[...] = jnp.concatenate(taus, axis=0)


def _peer_select(qt):
    rows, n = qt[0].shape
    tm = _pick(n, (256, 128))
    qt_arr, keys = qt
    hk = PEER_HEADS * PEER_N_KEYS
    big = lambda: pl.BlockSpec((hk, tm), lambda i: (0, i))
    return pl.pallas_call(
        _peer_select_kernel,
        grid=(n // tm,),
        in_specs=[pl.BlockSpec((rows, tm), lambda i: (0, i)),
                  pl.BlockSpec(keys.shape, lambda i: (0, 0))],
        out_specs=[big(), big(), big(), big(), pl.BlockSpec((PEER_HEADS, tm), lambda i: (0, i))],
        out_shape=[jax.ShapeDtypeStruct((hk, n), F32)] * 4
        + [jax.ShapeDtypeStruct((PEER_HEADS, n), F32)],
        compiler_params=_params(("parallel",), 40),
        name="peer_select",
    )(qt_arr, keys)


def _gelu_tanh(x):
    c = np.float32(np.sqrt(2.0 / np.pi))
    return x * (0.5 * (1.0 + jnp.tanh(c * (x + 0.044715 * (x * x * x)))))


def _peer_main_kernel(hn_ref, u_ref, vt_ref, s1_ref, s2_ref, e1_ref, e2_ref, tau_ref, o_ref,
                      raw_scr, a_scr, *, rows_per_tile, n_tiles):
    j = pl.program_id(1)
    last_slot = (n_tiles - 1) % 2

    def first_matmul(slot):
        raw_scr[slot] = jnp.dot(u_ref[...], hn_ref[...], preferred_element_type=F32)

    def gate_and_project(slot, init):
        tm = o_ref.shape[1]
        for rr in range(rows_per_tile):
            r = (j - 1) * rows_per_tile + rr
            s1_rows = [jnp.broadcast_to(s1_ref[pl.ds(h * PEER_N_KEYS + r, 1), :], (GATE_ROWS, tm))
                       for h in range(PEER_HEADS)]
            e1_rows = [jnp.broadcast_to(e1_ref[pl.ds(h * PEER_N_KEYS + r, 1), :], (GATE_ROWS, tm))
                       for h in range(PEER_HEADS)]
            taus = [jnp.broadcast_to(tau_ref[h:h + 1, :], (GATE_ROWS, tm))
                    for h in range(PEER_HEADS)]
            for c0 in range(0, PEER_N_KEYS, GATE_ROWS):
                gate = None
                for h in range(PEER_HEADS):
                    cs = slice(h * PEER_N_KEYS + c0, h * PEER_N_KEYS + c0 + GATE_ROWS)
                    hit = (s2_ref[cs, :] + s1_rows[h]) >= taus[h]
                    term = jnp.where(hit, e2_ref[cs, :] * e1_rows[h], 0.0)
                    gate = term if gate is None else gate + term
                es = slice(rr * PEER_N_KEYS + c0, rr * PEER_N_KEYS + c0 + GATE_ROWS)
                a_scr[es, :] = (_gelu_tanh(raw_scr[slot, es, :]) * gate).astype(BF16)
        contrib = jnp.dot(vt_ref[...], a_scr[...], preferred_element_type=F32)
        o_ref[...] = contrib if init else o_ref[...] + contrib

    @pl.when(j == 0)
    def _():
        first_matmul(0)

    @pl.when(j == 1)
    def _():
        first_matmul(1)
        gate_and_project(0, init=True)

    @pl.when((j > 1) & (j < n_tiles) & (j % 2 == 0))
    def _():
        first_matmul(0)
        gate_and_project(1, init=False)

    @pl.when((j > 1) & (j < n_tiles) & (j % 2 == 1))
    def _():
        first_matmul(1)
        gate_and_project(0, init=False)

    @pl.when(j == n_tiles)
    def _():
        gate_and_project(last_slot, init=False)


def _peer_main(hn_t, u, vt, s1, s2, e1, e2, tau):
    d, n = hn_t.shape
    n_exp = u.shape[0]
    tm = _pick(n, (512, 256, 128))
    te = 512
    hk = PEER_HEADS * PEER_N_KEYS
    tok = lambda rows: pl.BlockSpec((rows, tm), lambda i, j: (0, i), pipeline_mode=pl.Buffered(1))
    n_tiles = n_exp // te
    assert n_tiles >= 2
    return pl.pallas_call(
        functools.partial(_peer_main_kernel, rows_per_tile=te // PEER_N_KEYS, n_tiles=n_tiles),
        grid=(n // tm, n_tiles + 1),
        in_specs=[tok(d),
                  pl.BlockSpec((te, d), lambda i, j: (jnp.minimum(j, n_tiles - 1), 0)),
                  pl.BlockSpec((d, te), lambda i, j: (0, jnp.maximum(j - 1, 0))),
                  tok(hk), tok(hk), tok(hk), tok(hk), tok(PEER_HEADS)],
        out_specs=pl.BlockSpec((d, tm), lambda i, j: (0, i)),
        out_shape=jax.ShapeDtypeStruct((d, n), F32),
        scratch_shapes=[pltpu.VMEM((2, te, tm), F32), pltpu.VMEM((te, tm), BF16)],
        compiler_params=_params(("parallel", "arbitrary"), 58),
        name="peer_main",
    )(hn_t, u, vt, s1, s2, e1, e2, tau)


def _final_kernel(h_ref, p_ref, g_ref, o_ref):
    o_ref[...] = _rms(h_ref[...] + p_ref[...], g_ref[...])


def _in_proj_layout(d):
    segs = [("gate_a", d), ("gate_b", d), ("q_b", 2048), ("c_q", 1024), ("q_ix", 1024),
            ("c_kv", 512), ("k_b", 512), ("v_b", 512), ("k_pe", LANE), ("k_ix", LANE),
            ("w_ix", LANE)]
    segs.sort(key=lambda s: -s[1])
    offs, o = {}, 0
    for name, wd in segs:
        assert o % wd == 0
        offs[name] = (o, wd)
        o += wd
    total = -(-o // 1024) * 1024
    return offs, total


def _build_w_in(w_in, d):
    src = {}
    o = 0
    for name, wd in (("c_q", MLA_Q_LORA), ("c_kv", MLA_KV_LORA), ("k_pe", MLA_ROPE),
                     ("q_b", 2048), ("k_b", 512), ("v_b", 512), ("q_ix", 1024),
                     ("k_ix", IDX_DIM), ("w_ix", IDX_HEADS), ("gate_a", d), ("gate_b", d)):
        src[name] = w_in[:, o:o + wd]
        o += wd
    offs, total = _in_proj_layout(d)
    cols = []
    o = 0
    for name, (off, wd) in sorted(offs.items(), key=lambda kv: kv[1][0]):
        blk = src[name]
        cols.append(jnp.pad(blk, ((0, 0), (0, wd - blk.shape[1]))))
        o = off + wd
    cols.append(jnp.zeros((d, total - o), w_in.dtype))
    return jnp.concatenate(cols, axis=1).astype(BF16), offs


def kernel(x, meta_tokens, attn_norm_g, w_in, q_norm_g, w_uq, kv_norm_g, w_ukv, w_branch, w_out,
           ffn_norm_g, peer_w_q, peer_sub_keys, peer_u, peer_v, final_norm_g):
    b, s, d = x.shape
    assert attn_norm_g.shape[0] == 1, "single-layer block"
    assert s % Q_BLOCK == 0 and d % 512 == 0 and (d & (d - 1)) == 0
    t = s + N_META
    t_pad = -(-t // Q_BLOCK) * Q_BLOCK
    topk = min(DSA_TOPK_MAX, s // 4)
    n = b * t_pad

    meta = jnp.broadcast_to(meta_tokens.astype(x.dtype)[None], (b, N_META, d))
    h0 = jnp.concatenate([meta, x, jnp.zeros((b, t_pad - t, d), x.dtype)], axis=1).reshape(n, d)

    w_in_p, offs = _build_w_in(w_in[0], d)
    wq = w_uq[0].reshape(MLA_Q_LORA, MLA_HEADS, MLA_NOPE + MLA_ROPE)
    wq = jnp.pad(wq, ((0, 0), (0, 0), (0, MLA_QK_PAD - MLA_NOPE - MLA_ROPE)))
    wq = wq.reshape(MLA_Q_LORA, MLA_HEADS * MLA_QK_PAD).astype(BF16)
    wkv = w_ukv[0].reshape(MLA_KV_LORA, MLA_HEADS, MLA_NOPE + MLA_V)
    wkv = jnp.concatenate([wkv[:, :, :MLA_NOPE].reshape(MLA_KV_LORA, -1),
                           wkv[:, :, MLA_NOPE:].reshape(MLA_KV_LORA, -1)], axis=1).astype(BF16)
    w_br = w_branch[0].astype(BF16)
    w_o = w_out[0].astype(BF16)
    w_pq_t = peer_w_q[0].T.astype(BF16)
    keys = peer_sub_keys[0].reshape(PEER_HEADS * 2 * PEER_N_KEYS, PEER_HALF).astype(BF16)
    u_tab = peer_u[0].astype(BF16)
    v_tab_t = peer_v[0].T.astype(BF16)

    mla_tabs = _rope_tables(t_pad, LANE, MLA_ROPE)
    dsa_tabs = _rope_tables(t_pad, LANE, DSA_ROT)
    idx_tabs = _rope_tables(t_pad, IDX_DIM, IDX_ROT)

    hn = _rmsnorm(h0, attn_norm_g[0], BF16)
    proj = _matmul(hn, w_in_p, F32, "in_proj")

    tm = _pick(t_pad, (384, 128))
    n_pos = t_pad // tm
    row = lambda wd, off: pl.BlockSpec((tm, wd), lambda i, o=off // wd: (i, o))
    seg = lambda name: row(offs[name][1], offs[name][0])
    tab = pl.BlockSpec((tm, LANE), lambda i: (i % n_pos, 0))
    full = lambda a: pl.BlockSpec(a.shape, lambda i: (0,) * a.ndim)
    out_row = lambda wd: pl.BlockSpec((tm, wd), lambda i: (i, 0))
    sds = lambda wd, dt: jax.ShapeDtypeStruct((n, wd), dt)

    qg = q_norm_g[0].reshape(1, -1).astype(F32)
    q_a = pl.pallas_call(
        _mla_q_kernel, grid=(n // tm,),
        in_specs=[seg("c_q"), full(qg), full(wq), tab, tab, tab],
        out_specs=out_row(MLA_HEADS * MLA_QK_PAD),
        out_shape=sds(MLA_HEADS * MLA_QK_PAD, BF16),
        compiler_params=_params(("parallel",), 48), name="mla_q_prep",
    )(proj, qg, wq, *mla_tabs)

    kg = kv_norm_g[0].reshape(1, -1).astype(F32)
    k_a, v_a = pl.pallas_call(
        _mla_kv_kernel, grid=(n // tm,),
        in_specs=[seg("c_kv"), full(kg), full(wkv), seg("k_pe"), tab, tab, tab],
        out_specs=[out_row(MLA_HEADS * MLA_QK_PAD), out_row(MLA_HEADS * MLA_V)],
        out_shape=[sds(MLA_HEADS * MLA_QK_PAD, BF16), sds(MLA_HEADS * MLA_V, BF16)],
        compiler_params=_params(("parallel",), 48), name="mla_kv_prep",
    )(proj, kg, wkv, proj, *mla_tabs)

    kvw = DSA_KV_HEADS * DSA_HEAD_DIM
    q_b, k_b, v_b, q_ix, k_lo, k_hi, w_ix = pl.pallas_call(
        _dsa_prep_kernel, grid=(n // tm,),
        in_specs=[seg("q_b"), seg("k_b"), seg("v_b"), seg("q_ix"), seg("k_ix"), seg("w_ix")]
        + [tab] * 6,
        out_specs=[out_row(2048), out_row(kvw), out_row(kvw), out_row(1024),
                   out_row(LANE), out_row(LANE), out_row(LANE)],
        out_shape=[sds(2048, BF16), sds(kvw, BF16), sds(kvw, BF16), sds(1024, BF16),
                   sds(LANE, BF16), sds(LANE, BF16), sds(LANE, F32)],
        compiler_params=_params(("parallel",), 48), name="dsa_prep",
    )(proj, proj, proj, proj, proj, proj, *dsa_tabs, *idx_tabs)

    o_a = _mla_attention(q_a, k_a, v_a, b, t_pad)
    o_b = _dsa_attention(q_b, k_b, v_b, q_ix, k_lo, k_hi, w_ix, b, t_pad, topk)

    tmm = _pick(n, (512, 256, 128))
    tn = _pick(d, (1024, 512))
    ga_off, gb_off = offs["gate_a"][0] // tn, offs["gate_b"][0] // tn
    bw = MLA_HEADS * MLA_V
    merged = pl.pallas_call(
        _branch_kernel, grid=(d // tn, n // tmm),
        in_specs=[pl.BlockSpec((tmm, bw), lambda j, i: (i, 0)),
                  pl.BlockSpec((tmm, bw), lambda j, i: (i, 0)),
                  pl.BlockSpec((bw, tn), lambda j, i: (0, j)),
                  pl.BlockSpec((bw, tn), lambda j, i: (1, j)),
                  pl.BlockSpec((tmm, tn), lambda j, i: (i, ga_off + j)),
                  pl.BlockSpec((tmm, tn), lambda j, i: (i, gb_off + j))],
        out_specs=pl.BlockSpec((tmm, tn), lambda j, i: (i, j)),
        out_shape=jax.ShapeDtypeStruct((n, d), BF16),
        compiler_params=_params(("parallel", "parallel"), 48), name="branch_merge",
    )(o_a, o_b, w_br, w_br, proj, proj)

    h1 = pl.pallas_call(
        _residual_mm_kernel, grid=(d // tn, n // tmm),
        in_specs=[pl.BlockSpec((tmm, d), lambda j, i: (i, 0)),
                  pl.BlockSpec((d, tn), lambda j, i: (0, j)),
                  pl.BlockSpec((tmm, tn), lambda j, i: (i, j))],
        out_specs=pl.BlockSpec((tmm, tn), lambda j, i: (i, j)),
        out_shape=jax.ShapeDtypeStruct((n, d), F32),
        compiler_params=_params(("parallel", "parallel"), 48), name="out_proj",
    )(merged, w_o, h0)

    h1r = h1.reshape(b, t_pad, d)[:, N_META:N_META + s].reshape(b * s, d)
    hn2_t = _rmsnorm(h1r, ffn_norm_g[0], BF16).T
    q_t = _matmul(w_pq_t, hn2_t, F32, "peer_query")
    s1, s2, e1, e2, tau = _peer_select((q_t, keys))
    peer_t = _peer_main(hn2_t, u_tab, v_tab_t, s1, s2, e1, e2, tau)

    np_tok = b * s
    tf = _pick(np_tok, (256, 128))
    out = pl.pallas_call(
        _final_kernel, grid=(np_tok // tf,),
        in_specs=[pl.BlockSpec((tf, d), lambda i: (i, 0)), pl.BlockSpec((tf, d), lambda i: (i, 0)),
                  pl.BlockSpec((1, d), lambda i: (0, 0))],
        out_specs=pl.BlockSpec((tf, d), lambda i: (i, 0)),
        out_shape=jax.ShapeDtypeStruct((np_tok, d), F32),
        compiler_params=_params(("parallel",), 40), name="final_norm",
    )(h1r, peer_t.T, final_norm_g.reshape(1, d).astype(F32))
    return out.reshape(b, s, d)
```

```python
import functools

import jax
import jax.numpy as jnp
import numpy as np
from jax import lax
from jax.experimental import pallas as pl
from jax.experimental.pallas import tpu as pltpu

N_META = 16
ROPE_THETA = 500000.0
Q_BLOCK = 128
NORM_EPS = 1e-6
MLA_HEADS = 16
MLA_Q_LORA = 1024
MLA_KV_LORA = 512
MLA_NOPE = 128
MLA_ROPE = 64
MLA_V = 128
DSA_HEADS = 16
DSA_KV_HEADS = 4
DSA_HEAD_DIM = 128
DSA_ROT = DSA_HEAD_DIM // 4
IDX_HEADS = 16
IDX_DIM = 64
IDX_ROT = IDX_DIM // 4
DSA_TOPK_MAX = 256
PEER_HEADS = 8
PEER_N_KEYS = 128
PEER_HALF = 128
PEER_TOPK = 16

LANE = 128
MXU_WIDTH = 256
GATE_ROWS = 16
MLA_HEAD_GROUP = 4
LOG2_E = 1.4426950408889634
MLA_QK_PAD = 2 * LANE
MASK_NEG = -1e30
INT32_MIN = -(2 ** 31)
MIB = 1024 * 1024

BF16 = jnp.bfloat16
F32 = jnp.float32


def _pick(n, prefs):
    for p in prefs:
        if n % p == 0:
            return p
    raise ValueError(f"no tile in {prefs} divides {n}")


def _params(sem, vmem_mib):
    return pltpu.CompilerParams(dimension_semantics=sem, vmem_limit_bytes=vmem_mib * MIB)


def _nt_dot(a, b):
    return lax.dot_general(a, b, (((1,), (1,)), ((), ())), preferred_element_type=F32)


def _rms(x, g):
    ms = jnp.mean(x * x, axis=-1, keepdims=True)
    return x * lax.rsqrt(ms + NORM_EPS) * g


def _rope(x, c, s_up, s_dn, half):
    return x * c + pltpu.roll(x, half, 1) * s_up + pltpu.roll(x, LANE - half, 1) * s_dn


def _rope_tables(t_pad, block, rot):
    half = rot // 2
    inv = ROPE_THETA ** (-jnp.arange(0, rot, 2, dtype=F32) / rot)
    ang = jnp.arange(t_pad, dtype=F32)[:, None] * inv[None, :]
    cos, sin = jnp.cos(ang), jnp.sin(ang)
    lane = np.arange(LANE) % block
    lo = lane < half
    hi = (lane >= half) & (lane < rot)
    idx = np.where(lo, lane, np.where(hi, lane - half, 0))
    cos_l, sin_l = cos[:, idx], sin[:, idx]
    c = jnp.where(lo | hi, cos_l, 1.0)
    s_up = jnp.where(hi, sin_l, 0.0)
    s_dn = jnp.where(lo, -sin_l, 0.0)
    return c.astype(F32), s_up.astype(F32), s_dn.astype(F32)


def _rmsnorm_kernel(x_ref, g_ref, o_ref):
    o_ref[...] = _rms(x_ref[...].astype(F32), g_ref[...]).astype(o_ref.dtype)


def _rmsnorm(x, g, out_dtype):
    m, d = x.shape
    tm = _pick(m, (256, 128))
    return pl.pallas_call(
        _rmsnorm_kernel,
        grid=(m // tm,),
        in_specs=[pl.BlockSpec((tm, d), lambda i: (i, 0)),
                  pl.BlockSpec((1, d), lambda i: (0, 0))],
        out_specs=pl.BlockSpec((tm, d), lambda i: (i, 0)),
        out_shape=jax.ShapeDtypeStruct((m, d), out_dtype),
        compiler_params=_params(("parallel",), 40),
        name="rmsnorm",
    )(x, g.reshape(1, d).astype(F32))


def _mm_kernel(a_ref, b_ref, o_ref):
    o_ref[...] = jnp.dot(a_ref[...], b_ref[...], preferred_element_type=F32).astype(o_ref.dtype)


def _matmul(a, b, out_dtype, name):
    m, k = a.shape
    _, n = b.shape
    tm = _pick(m, (512, 256, 128))
    tn = _pick(n, (1024, 512, 256, 128))
    return pl.pallas_call(
        _mm_kernel,
        grid=(n // tn, m // tm),
        in_specs=[pl.BlockSpec((tm, k), lambda j, i: (i, 0)),
                  pl.BlockSpec((k, tn), lambda j, i: (0, j))],
        out_specs=pl.BlockSpec((tm, tn), lambda j, i: (i, j)),
        out_shape=jax.ShapeDtypeStruct((m, n), out_dtype),
        compiler_params=_params(("parallel", "parallel"), 48),
        name=name,
    )(a, b)


def _mla_q_kernel(cq_ref, g_ref, w_ref, c_ref, su_ref, sd_ref, o_ref):
    xn = _rms(cq_ref[...], g_ref[...]).astype(BF16)
    q = jnp.dot(xn, w_ref[...], preferred_element_type=F32)
    q = q * ((MLA_NOPE + MLA_ROPE) ** -0.5 * LOG2_E)
    c, su, sd = c_ref[...], su_ref[...], sd_ref[...]
    for h in range(MLA_HEADS):
        base = h * MLA_QK_PAD
        o_ref[:, base:base + LANE] = q[:, base:base + LANE].astype(BF16)
        o_ref[:, base + LANE:base + 2 * LANE] = _rope(
            q[:, base + LANE:base + 2 * LANE], c, su, sd, MLA_ROPE // 2).astype(BF16)


def _mla_kv_kernel(ckv_ref, g_ref, w_ref, kpe_ref, c_ref, su_ref, sd_ref, k_ref, v_ref):
    xn = _rms(ckv_ref[...], g_ref[...]).astype(BF16)
    kv = jnp.dot(xn, w_ref[...], preferred_element_type=F32)
    kpe = _rope(kpe_ref[...], c_ref[...], su_ref[...], sd_ref[...], MLA_ROPE // 2).astype(BF16)
    nk = MLA_HEADS * MLA_NOPE
    for h in range(MLA_HEADS):
        base = h * MLA_QK_PAD
        k_ref[:, base:base + LANE] = kv[:, h * MLA_NOPE:(h + 1) * MLA_NOPE].astype(BF16)
        k_ref[:, base + LANE:base + 2 * LANE] = kpe
    v_ref[...] = kv[:, nk:].astype(BF16)


def _dsa_prep_kernel(qb_ref, kb_ref, vb_ref, qix_ref, kix_ref, wix_ref,
                     dc_ref, dsu_ref, dsd_ref, ic_ref, isu_ref, isd_ref,
                     qb_o, kb_o, vb_o, qix_o, klo_o, khi_o, w_o):
    dc, dsu, dsd = dc_ref[...], dsu_ref[...], dsd_ref[...]
    ic, isu, isd = ic_ref[...], isu_ref[...], isd_ref[...]
    for h in range(DSA_HEADS):
        sl = slice(h * LANE, (h + 1) * LANE)
        qb = _rope(qb_ref[:, sl], dc, dsu, dsd, DSA_ROT // 2)
        qb_o[:, sl] = (qb * (DSA_HEAD_DIM ** -0.5 * LOG2_E)).astype(BF16)
    for h in range(DSA_KV_HEADS):
        sl = slice(h * LANE, (h + 1) * LANE)
        kb_o[:, sl] = _rope(kb_ref[:, sl], dc, dsu, dsd, DSA_ROT // 2).astype(BF16)
    vb_o[...] = vb_ref[...].astype(BF16)
    for p in range(IDX_HEADS * IDX_DIM // LANE):
        sl = slice(p * LANE, (p + 1) * LANE)
        qix_o[:, sl] = _rope(qix_ref[:, sl], ic, isu, isd, IDX_ROT // 2).astype(BF16)
    kix = _rope(kix_ref[...], ic, isu, isd, IDX_ROT // 2)
    klo_o[...] = kix.astype(BF16)
    khi_o[...] = pltpu.roll(kix, IDX_DIM, 1).astype(BF16)
    w_o[...] = wix_ref[...] * (IDX_HEADS ** -0.5 * IDX_DIM ** -0.5)


def _softmax_steps(scores, values, carries):
    stats = []
    for s, (m, l, _) in zip(scores, carries):
        m_new = jnp.maximum(m, jnp.max(s, axis=1, keepdims=True))
        alpha = jnp.exp2(m - m_new)
        p = jnp.exp2(s - m_new)
        stats.append((m_new, alpha, alpha * l + jnp.sum(p, axis=1, keepdims=True),
                      p.astype(BF16)))
    out = []
    for (m_new, alpha, l, p), v, (_, _, acc) in zip(stats, values, carries):
        out.append((m_new, l, alpha * acc + jnp.dot(p, v, preferred_element_type=F32)))
    return tuple(out)


def _softmax_init(rows, width, chains):
    return tuple((jnp.full((rows, 1), MASK_NEG, F32), jnp.zeros((rows, 1), F32),
                  jnp.zeros((rows, width), F32)) for _ in range(chains))


def _mla_attn_kernel(q_ref, k_ref, v_ref, o_ref, *, tq, heads):
    i = pl.program_id(2)
    qs = [q_ref[:, h * MLA_QK_PAD:(h + 1) * MLA_QK_PAD] for h in range(heads)]

    def chunk(j, carries, diagonal):
        off = pl.multiple_of(j * tq, tq)
        scores = [_nt_dot(qs[h], k_ref[pl.ds(off, tq), h * MLA_QK_PAD:(h + 1) * MLA_QK_PAD])
                  for h in range(heads)]
        if diagonal:
            row = lax.broadcasted_iota(jnp.int32, (tq, tq), 0)
            col = lax.broadcasted_iota(jnp.int32, (tq, tq), 1)
            scores = [jnp.where(col <= row, s, MASK_NEG) for s in scores]
        values = [v_ref[pl.ds(off, tq), h * MLA_V:(h + 1) * MLA_V] for h in range(heads)]
        return _softmax_steps(scores, values, carries)

    carries = lax.fori_loop(0, i, lambda j, c: chunk(j, c, False),
                            _softmax_init(tq, MLA_V, heads))
    carries = chunk(i, carries, True)
    for h, (_, l, acc) in enumerate(carries):
        o_ref[:, h * MLA_V:(h + 1) * MLA_V] = (acc / l).astype(o_ref.dtype)


def _mla_attention(q, k, v, b, t_pad):
    tq = _pick(t_pad, (384, 128))
    hg = MLA_HEAD_GROUP
    q3 = q.reshape(b, t_pad, MLA_HEADS * MLA_QK_PAD)
    k3 = k.reshape(b, t_pad, MLA_HEADS * MLA_QK_PAD)
    v3 = v.reshape(b, t_pad, MLA_HEADS * MLA_V)
    out = pl.pallas_call(
        functools.partial(_mla_attn_kernel, tq=tq, heads=hg),
        grid=(b, MLA_HEADS // hg, t_pad // tq),
        in_specs=[pl.BlockSpec((None, tq, hg * MLA_QK_PAD), lambda bb, h, i: (bb, i, h)),
                  pl.BlockSpec((None, t_pad, hg * MLA_QK_PAD), lambda bb, h, i: (bb, 0, h)),
                  pl.BlockSpec((None, t_pad, hg * MLA_V), lambda bb, h, i: (bb, 0, h))],
        out_specs=pl.BlockSpec((None, tq, hg * MLA_V), lambda bb, h, i: (bb, i, h)),
        out_shape=jax.ShapeDtypeStruct((b, t_pad, MLA_HEADS * MLA_V), BF16),
        compiler_params=_params(("parallel", "parallel", "arbitrary"), 48),
        name="mla_attention",
    )(q3, k3, v3)
    return out.reshape(b * t_pad, MLA_HEADS * MLA_V)


def _dsa_kernel(qb_ref, kb_ref, vb_ref, qix_ref, klo_ref, khi_ref, w_ref, o_ref,
                key_scr, bias_scr, *, tq, tk, topk):
    i = pl.program_id(1)
    n_chunks = ((i + 1) * tq + tk - 1) // tk
    row = i * tq + lax.broadcasted_iota(jnp.int32, (tq, tk), 0)
    col0 = lax.broadcasted_iota(jnp.int32, (tq, tk), 1)
    w = w_ref[...]
    qix = qix_ref[...]

    def score_chunk(c, carry):
        off = pl.multiple_of(c * tk, tk)
        klo = klo_ref[pl.ds(off, tk), :]
        khi = khi_ref[pl.ds(off, tk), :]
        sc = jnp.zeros((tq, tk), F32)
        for h in range(IDX_HEADS):
            qp = qix[:, (h // 2) * LANE:(h // 2 + 1) * LANE]
            d = _nt_dot(qp, klo if h % 2 == 0 else khi)
            sc = sc + w[:, h:h + 1] * jnp.maximum(d, 0.0)
        sc = jnp.where(off + col0 <= row, sc, -jnp.inf)
        bits = pltpu.bitcast(sc, jnp.int32)
        key_scr[:, pl.ds(off, tk)] = bits ^ ((bits >> 31) & jnp.int32(0x7FFFFFFF))
        return carry

    lax.fori_loop(0, n_chunks, score_chunk, 0)

    def bit_step(bi, t):
        cand = t + (jnp.int32(1) << (31 - bi))

        def count_chunk(c, acc):
            off = pl.multiple_of(c * tk, tk)
            ge = jnp.where(key_scr[:, pl.ds(off, tk)] >= cand, 1.0, 0.0)
            for p in range(tk // LANE):
                acc = acc + ge[:, p * LANE:(p + 1) * LANE]
            return acc

        acc = lax.fori_loop(0, n_chunks, count_chunk, jnp.zeros((tq, LANE), F32))
        cnt = jnp.sum(acc, axis=1, keepdims=True)
        return jnp.where(cnt >= topk, cand, t)

    thr = lax.fori_loop(0, 32, bit_step, jnp.full((tq, 1), INT32_MIN, jnp.int32))

    def bias_chunk(c, carry):
        off = pl.multiple_of(c * tk, tk)
        ok = (key_scr[:, pl.ds(off, tk)] >= thr) & (off + col0 <= row)
        bias_scr[:, pl.ds(off, tk)] = jnp.where(ok, 0.0, MASK_NEG)
        return carry

    lax.fori_loop(0, n_chunks, bias_chunk, 0)

    rep = DSA_HEADS // DSA_KV_HEADS
    groups = [slice(g * LANE, (g + 1) * LANE) for g in range(DSA_KV_HEADS)]
    qgs = [jnp.concatenate(
        [qb_ref[:, (g * rep + j) * LANE:(g * rep + j + 1) * LANE] for j in range(rep)], axis=0)
        for g in range(DSA_KV_HEADS)]

    def att_chunk(c, carries):
        off = pl.multiple_of(c * tk, tk)
        bias = bias_scr[:, pl.ds(off, tk)]
        bias = jnp.concatenate([bias] * rep, axis=0)
        scores = [_nt_dot(qg, kb_ref[pl.ds(off, tk), gsl]) + bias for qg, gsl in zip(qgs, groups)]
        values = [vb_ref[pl.ds(off, tk), gsl] for gsl in groups]
        return _softmax_steps(scores, values, carries)

    carries = lax.fori_loop(0, n_chunks, att_chunk,
                            _softmax_init(rep * tq, DSA_HEAD_DIM, DSA_KV_HEADS))
    for g, (_, l, acc) in enumerate(carries):
        o = acc / l
        for j in range(rep):
            hq = g * rep + j
            o_ref[:, hq * LANE:(hq + 1) * LANE] = o[j * tq:(j + 1) * tq].astype(o_ref.dtype)


def _dsa_attention(qb, kb, vb, qix, klo, khi, w, b, t_pad, topk):
    tq = Q_BLOCK
    tk = _pick(t_pad, (384, 128))
    kvw = DSA_KV_HEADS * DSA_HEAD_DIM
    r3 = lambda a: a.reshape(b, t_pad, a.shape[-1])
    qspec = lambda wd: pl.BlockSpec((None, tq, wd), lambda bb, i: (bb, i, 0))
    kspec = lambda wd: pl.BlockSpec((None, t_pad, wd), lambda bb, i: (bb, 0, 0))
    out = pl.pallas_call(
        functools.partial(_dsa_kernel, tq=tq, tk=tk, topk=topk),
        grid=(b, t_pad // tq),
        in_specs=[qspec(DSA_HEADS * DSA_HEAD_DIM), kspec(kvw), kspec(kvw),
                  qspec(IDX_HEADS * IDX_DIM), kspec(LANE), kspec(LANE), qspec(LANE)],
        out_specs=qspec(DSA_HEADS * DSA_HEAD_DIM),
        out_shape=jax.ShapeDtypeStruct((b, t_pad, DSA_HEADS * DSA_HEAD_DIM), BF16),
        scratch_shapes=[pltpu.VMEM((tq, t_pad), jnp.int32), pltpu.VMEM((tq, t_pad), F32)],
        compiler_params=_params(("parallel", "arbitrary"), 48),
        name="dsa_attention",
    )(r3(qb), r3(kb), r3(vb), r3(qix), r3(klo), r3(khi), r3(w))
    return out.reshape(b * t_pad, DSA_HEADS * DSA_HEAD_DIM)


def _sigmoid(x):
    return 1.0 / (1.0 + jnp.exp(-x))


def _branch_kernel(oa_ref, ob_ref, wa_ref, wb_ref, ga_ref, gb_ref, o_ref):
    ya = jnp.dot(oa_ref[...], wa_ref[...], preferred_element_type=F32)
    yb = jnp.dot(ob_ref[...], wb_ref[...], preferred_element_type=F32)
    o_ref[...] = (_sigmoid(ga_ref[...]) * ya + _sigmoid(gb_ref[...]) * yb).astype(o_ref.dtype)


def _residual_mm_kernel(a_ref, b_ref, h_ref, o_ref):
    o_ref[...] = h_ref[...] + jnp.dot(a_ref[...], b_ref[...], preferred_element_type=F32)


def _peer_select_kernel(qt_ref, keys_ref, s1_ref, s2_ref, e1_ref, e2_ref, tau_ref):
    neg_inf = -jnp.inf
    taus = []
    for h in range(PEER_HEADS):
        scores, tops = [], []
        for c in range(2):
            sl = slice((2 * h + c) * PEER_HALF, (2 * h + c + 1) * PEER_HALF)
            s = jnp.dot(keys_ref[sl, :], qt_ref[sl, :].astype(BF16), preferred_element_type=F32)
            scores.append(s)
            vals, x = [], s
            for _ in range(PEER_TOPK):
                m = jnp.max(x, axis=0, keepdims=True)
                vals.append(m)
                x = jnp.where(x == m, neg_inf, x)
            tops.append(vals)
        a, bv = tops
        cands = [a[i] + bv[j] for i in range(PEER_TOPK) for j in range(PEER_TOPK)
                 if (i + 1) * (j + 1) <= PEER_TOPK]
        v0 = a[0] + bv[0]
        z = jnp.zeros_like(v0)
        m = v0
        for _ in range(PEER_TOPK):
            m = functools.reduce(jnp.maximum, cands)
            z = z + jnp.exp(m - v0)
            cands = [jnp.where(cd == m, neg_inf, cd) for cd in cands]
        taus.append(m)
        hs = slice(h * PEER_N_KEYS, (h + 1) * PEER_N_KEYS)
        s1_ref[hs, :] = scores[0]
        s2_ref[hs, :] = scores[1]
        e1_ref[hs, :] = jnp.exp(scores[0] - a[0])
        e2_ref[hs, :] = jnp.exp(scores[1] - bv[0]) / z
    tau_ref[...] = jnp.concatenate(taus, axis=0)


def _peer_select(qt):
    rows, n = qt[0].shape
    tm = _pick(n, (256, 128))
    qt_arr, keys = qt
    hk = PEER_HEADS * PEER_N_KEYS
    big = lambda: pl.BlockSpec((hk, tm), lambda i: (0, i))
    return pl.pallas_call(
        _peer_select_kernel,
        grid=(n // tm,),
        in_specs=[pl.BlockSpec((rows, tm), lambda i: (0, i)),
                  pl.BlockSpec(keys.shape, lambda i: (0, 0))],
        out_specs=[big(), big(), big(), big(), pl.BlockSpec((PEER_HEADS, tm), lambda i: (0, i))],
        out_shape=[jax.ShapeDtypeStruct((hk, n), F32)] * 4
        + [jax.ShapeDtypeStruct((PEER_HEADS, n), F32)],
        compiler_params=_params(("parallel",), 40),
        name="peer_select",
    )(qt_arr, keys)


def _gelu_tanh(x):
    c = np.float32(np.sqrt(2.0 / np.pi))
    return x * (0.5 * (1.0 + jnp.tanh(c * (x + 0.044715 * (x * x * x)))))


def _peer_main_kernel(hn_ref, u_ref, vt_ref, s1_ref, s2_ref, e1_ref, e2_ref, tau_ref, o_ref,
                      raw_scr, a_scr, *, rows_per_tile, n_tiles):
    j = pl.program_id(1)
    last_slot = (n_tiles - 1) % 2

    def first_matmul(slot):
        raw_scr[slot] = jnp.dot(u_ref[...], hn_ref[...], preferred_element_type=F32)

    def gate_and_project(slot, init):
        tm = o_ref.shape[1]
        for rr in range(rows_per_tile):
            r = (j - 1) * rows_per_tile + rr
            s1_rows = [jnp.broadcast_to(s1_ref[pl.ds(h * PEER_N_KEYS + r, 1), :], (GATE_ROWS, tm))
                       for h in range(PEER_HEADS)]
            e1_rows = [jnp.broadcast_to(e1_ref[pl.ds(h * PEER_N_KEYS + r, 1), :], (GATE_ROWS, tm))
                       for h in range(PEER_HEADS)]
            taus = [jnp.broadcast_to(tau_ref[h:h + 1, :], (GATE_ROWS, tm))
                    for h in range(PEER_HEADS)]
            for c0 in range(0, PEER_N_KEYS, GATE_ROWS):
                gate = None
                for h in range(PEER_HEADS):
                    cs = slice(h * PEER_N_KEYS + c0, h * PEER_N_KEYS + c0 + GATE_ROWS)
                    hit = (s2_ref[cs, :] + s1_rows[h]) >= taus[h]
                    term = jnp.where(hit, e2_ref[cs, :] * e1_rows[h], 0.0)
                    gate = term if gate is None else gate + term
                es = slice(rr * PEER_N_KEYS + c0, rr * PEER_N_KEYS + c0 + GATE_ROWS)
                a_scr[es, :] = (_gelu_tanh(raw_scr[slot, es, :]) * gate).astype(BF16)
        contrib = jnp.dot(vt_ref[...], a_scr[...], preferred_element_type=F32)
        o_ref[...] = contrib if init else o_ref[...] + contrib

    @pl.when(j == 0)
    def _():
        first_matmul(0)

    @pl.when(j == 1)
    def _():
        first_matmul(1)
        gate_and_project(0, init=True)

    @pl.when((j > 1) & (j < n_tiles) & (j % 2 == 0))
    def _():
        first_matmul(0)
        gate_and_project(1, init=False)

    @pl.when((j > 1) & (j < n_tiles) & (j % 2 == 1))
    def _():
        first_matmul(1)
        gate_and_project(0, init=False)

    @pl.when(j == n_tiles)
    def _():
        gate_and_project(last_slot, init=False)


def _peer_main(hn_t, u, vt, s1, s2, e1, e2, tau):
    d, n = hn_t.shape
    n_exp = u.shape[0]
    tm = _pick(n, (512, 256, 128))
    te = 512
    hk = PEER_HEADS * PEER_N_KEYS
    tok = lambda rows: pl.BlockSpec((rows, tm), lambda i, j: (0, i), pipeline_mode=pl.Buffered(1))
    n_tiles = n_exp // te
    assert n_tiles >= 2
    return pl.pallas_call(
        functools.partial(_peer_main_kernel, rows_per_tile=te // PEER_N_KEYS, n_tiles=n_tiles),
        grid=(n // tm, n_tiles + 1),
        in_specs=[tok(d),
                  pl.BlockSpec((te, d), lambda i, j: (jnp.minimum(j, n_tiles - 1), 0)),
                  pl.BlockSpec((d, te), lambda i, j: (0, jnp.maximum(j - 1, 0))),
                  tok(hk), tok(hk), tok(hk), tok(hk), tok(PEER_HEADS)],
        out_specs=pl.BlockSpec((d, tm), lambda i, j: (0, i)),
        out_shape=jax.ShapeDtypeStruct((d, n), F32),
        scratch_shapes=[pltpu.VMEM((2, te, tm), F32), pltpu.VMEM((te, tm), BF16)],
        compiler_params=_params(("parallel", "arbitrary"), 58),
        name="peer_main",
    )(hn_t, u, vt, s1, s2, e1, e2, tau)


def _final_kernel(h_ref, p_ref, g_ref, o_ref):
    o_ref[...] = _rms(h_ref[...] + p_ref[...], g_ref[...])


def _in_proj_layout(d):
    segs = [("gate_a", d), ("gate_b", d), ("q_b", 2048), ("c_q", 1024), ("q_ix", 1024),
            ("c_kv", 512), ("k_b", 512), ("v_b", 512), ("k_pe", LANE), ("k_ix", LANE),
            ("w_ix", LANE)]
    segs.sort(key=lambda s: -s[1])
    offs, o = {}, 0
    for name, wd in segs:
        assert o % wd == 0
        offs[name] = (o, wd)
        o += wd
    total = -(-o // 1024) * 1024
    return offs, total


def _build_w_in(w_in, d):
    src = {}
    o = 0
    for name, wd in (("c_q", MLA_Q_LORA), ("c_kv", MLA_KV_LORA), ("k_pe", MLA_ROPE),
                     ("q_b", 2048), ("k_b", 512), ("v_b", 512), ("q_ix", 1024),
                     ("k_ix", IDX_DIM), ("w_ix", IDX_HEADS), ("gate_a", d), ("gate_b", d)):
        src[name] = w_in[:, o:o + wd]
        o += wd
    offs, total = _in_proj_layout(d)
    cols = []
    o = 0
    for name, (off, wd) in sorted(offs.items(), key=lambda kv: kv[1][0]):
        blk = src[name]
        cols.append(jnp.pad(blk, ((0, 0), (0, wd - blk.shape[1]))))
        o = off + wd
    cols.append(jnp.zeros((d, total - o), w_in.dtype))
    return jnp.concatenate(cols, axis=1).astype(BF16), offs


def kernel(x, meta_tokens, attn_norm_g, w_in, q_norm_g, w_uq, kv_norm_g, w_ukv, w_branch, w_out,
           ffn_norm_g, peer_w_q, peer_sub_keys, peer_u, peer_v, final_norm_g):
    b, s, d = x.shape
    assert attn_norm_g.shape[0] == 1, "single-layer block"
    assert s % Q_BLOCK == 0 and d % 512 == 0 and (d & (d - 1)) == 0
    t = s + N_META
    t_pad = -(-t // Q_BLOCK) * Q_BLOCK
    topk = min(DSA_TOPK_MAX, s // 4)
    n = b * t_pad

    meta = jnp.broadcast_to(meta_tokens.astype(x.dtype)[None], (b, N_META, d))
    h0 = jnp.concatenate([meta, x, jnp.zeros((b, t_pad - t, d), x.dtype)], axis=1).reshape(n, d)

    w_in_p, offs = _build_w_in(w_in[0], d)
    wq = w_uq[0].reshape(MLA_Q_LORA, MLA_HEADS, MLA_NOPE + MLA_ROPE)
    wq = jnp.pad(wq, ((0, 0), (0, 0), (0, MLA_QK_PAD - MLA_NOPE - MLA_ROPE)))
    wq = wq.reshape(MLA_Q_LORA, MLA_HEADS * MLA_QK_PAD).astype(BF16)
    wkv = w_ukv[0].reshape(MLA_KV_LORA, MLA_HEADS, MLA_NOPE + MLA_V)
    wkv = jnp.concatenate([wkv[:, :, :MLA_NOPE].reshape(MLA_KV_LORA, -1),
                           wkv[:, :, MLA_NOPE:].reshape(MLA_KV_LORA, -1)], axis=1).astype(BF16)
    w_br = w_branch[0].astype(BF16)
    w_o = w_out[0].astype(BF16)
    w_pq_t = peer_w_q[0].T.astype(BF16)
    keys = peer_sub_keys[0].reshape(PEER_HEADS * 2 * PEER_N_KEYS, PEER_HALF).astype(BF16)
    u_tab = peer_u[0].astype(BF16)
    v_tab_t = peer_v[0].T.astype(BF16)

    mla_tabs = _rope_tables(t_pad, LANE, MLA_ROPE)
    dsa_tabs = _rope_tables(t_pad, LANE, DSA_ROT)
    idx_tabs = _rope_tables(t_pad, IDX_DIM, IDX_ROT)

    hn = _rmsnorm(h0, attn_norm_g[0], BF16)
    proj = _matmul(hn, w_in_p, F32, "in_proj")

    tm = _pick(t_pad, (384, 128))
    n_pos = t_pad // tm
    row = lambda wd, off: pl.BlockSpec((tm, wd), lambda i, o=off // wd: (i, o))
    seg = lambda name: row(offs[name][1], offs[name][0])
    tab = pl.BlockSpec((tm, LANE), lambda i: (i % n_pos, 0))
    full = lambda a: pl.BlockSpec(a.shape, lambda i: (0,) * a.ndim)
    out_row = lambda wd: pl.BlockSpec((tm, wd), lambda i: (i, 0))
    sds = lambda wd, dt: jax.ShapeDtypeStruct((n, wd), dt)

    qg = q_norm_g[0].reshape(1, -1).astype(F32)
    q_a = pl.pallas_call(
        _mla_q_kernel, grid=(n // tm,),
        in_specs=[seg("c_q"), full(qg), full(wq), tab, tab, tab],
        out_specs=out_row(MLA_HEADS * MLA_QK_PAD),
        out_shape=sds(MLA_HEADS * MLA_QK_PAD, BF16),
        compiler_params=_params(("parallel",), 48), name="mla_q_prep",
    )(proj, qg, wq, *mla_tabs)

    kg = kv_norm_g[0].reshape(1, -1).astype(F32)
    k_a, v_a = pl.pallas_call(
        _mla_kv_kernel, grid=(n // tm,),
        in_specs=[seg("c_kv"), full(kg), full(wkv), seg("k_pe"), tab, tab, tab],
        out_specs=[out_row(MLA_HEADS * MLA_QK_PAD), out_row(MLA_HEADS * MLA_V)],
        out_shape=[sds(MLA_HEADS * MLA_QK_PAD, BF16), sds(MLA_HEADS * MLA_V, BF16)],
        compiler_params=_params(("parallel",), 48), name="mla_kv_prep",
    )(proj, kg, wkv, proj, *mla_tabs)

    kvw = DSA_KV_HEADS * DSA_HEAD_DIM
    q_b, k_b, v_b, q_ix, k_lo, k_hi, w_ix = pl.pallas_call(
        _dsa_prep_kernel, grid=(n // tm,),
        in_specs=[seg("q_b"), seg("k_b"), seg("v_b"), seg("q_ix"), seg("k_ix"), seg("w_ix")]
        + [tab] * 6,
        out_specs=[out_row(2048), out_row(kvw), out_row(kvw), out_row(1024),
                   out_row(LANE), out_row(LANE), out_row(LANE)],
        out_shape=[sds(2048, BF16), sds(kvw, BF16), sds(kvw, BF16), sds(1024, BF16),
                   sds(LANE, BF16), sds(LANE, BF16), sds(LANE, F32)],
        compiler_params=_params(("parallel",), 48), name="dsa_prep",
    )(proj, proj, proj, proj, proj, proj, *dsa_tabs, *idx_tabs)

    o_a = _mla_attention(q_a, k_a, v_a, b, t_pad)
    o_b = _dsa_attention(q_b, k_b, v_b, q_ix, k_lo, k_hi, w_ix, b, t_pad, topk)

    tmm = _pick(n, (512, 256, 128))
    tn = _pick(d, (1024, 512))
    ga_off, gb_off = offs["gate_a"][0] // tn, offs["gate_b"][0] // tn
    bw = MLA_HEADS * MLA_V
    merged = pl.pallas_call(
        _branch_kernel, grid=(d // tn, n // tmm),
        in_specs=[pl.BlockSpec((tmm, bw), lambda j, i: (i, 0)),
                  pl.BlockSpec((tmm, bw), lambda j, i: (i, 0)),
                  pl.BlockSpec((bw, tn), lambda j, i: (0, j)),
                  pl.BlockSpec((bw, tn), lambda j, i: (1, j)),
                  pl.BlockSpec((tmm, tn), lambda j, i: (i, ga_off + j)),
                  pl.BlockSpec((tmm, tn), lambda j, i: (i, gb_off + j))],
        out_specs=pl.BlockSpec((tmm, tn), lambda j, i: (i, j)),
        out_shape=jax.ShapeDtypeStruct((n, d), BF16),
        compiler_params=_params(("parallel", "parallel"), 48), name="branch_merge",
    )(o_a, o_b, w_br, w_br, proj, proj)

    h1 = pl.pallas_call(
        _residual_mm_kernel, grid=(d // tn, n // tmm),
        in_specs=[pl.BlockSpec((tmm, d), lambda j, i: (i, 0)),
                  pl.BlockSpec((d, tn), lambda j, i: (0, j)),
                  pl.BlockSpec((tmm, tn), lambda j, i: (i, j))],
        out_specs=pl.BlockSpec((tmm, tn), lambda j, i: (i, j)),
        out_shape=jax.ShapeDtypeStruct((n, d), F32),
        compiler_params=_params(("parallel", "parallel"), 48), name="out_proj",
    )(merged, w_o, h0)

    h1r = h1.reshape(b, t_pad, d)[:, N_META:N_META + s].reshape(b * s, d)
    hn2_t = _rmsnorm(h1r, ffn_norm_g[0], BF16).T
    q_t = _matmul(w_pq_t, hn2_t, F32, "peer_query")
    s1, s2, e1, e2, tau = _peer_select((q_t, keys))
    peer_t = _peer_main(hn2_t, u_tab, v_tab_t, s1, s2, e1, e2, tau)

    np_tok = b * s
    tf = _pick(np_tok, (256, 128))
    out = pl.pallas_call(
        _final_kernel, grid=(np_tok // tf,),
        in_specs=[pl.BlockSpec((tf, d), lambda i: (i, 0)), pl.BlockSpec((tf, d), lambda i: (i, 0)),
                  pl.BlockSpec((1, d), lambda i: (0, 0))],
        out_specs=pl.BlockSpec((tf, d), lambda i: (i, 0)),
        out_shape=jax.ShapeDtypeStruct((np_tok, d), F32),
        compiler_params=_params(("parallel",), 40), name="final_norm",
    )(h1r, peer_t.T, final_norm_g.reshape(1, d).astype(F32))
    return out.reshape(b, s, d)
```

```python
import functools

import jax
import jax.numpy as jnp
import numpy as np
from jax import lax
from jax.experimental import pallas as pl
from jax.experimental.pallas import tpu as pltpu

N_META = 16
ROPE_THETA = 500000.0
Q_BLOCK = 128
NORM_EPS = 1e-6
MLA_HEADS = 16
MLA_Q_LORA = 1024
MLA_KV_LORA = 512
MLA_NOPE = 128
MLA_ROPE = 64
MLA_V = 128
DSA_HEADS = 16
DSA_KV_HEADS = 4
DSA_HEAD_DIM = 128
DSA_ROT = DSA_HEAD_DIM // 4
IDX_HEADS = 16
IDX_DIM = 64
IDX_ROT = IDX_DIM // 4
DSA_TOPK_MAX = 256
PEER_HEADS = 8
PEER_N_KEYS = 128
PEER_HALF = 128
PEER_TOPK = 16

LANE = 128
SUBLANE = 8
MXU_WIDTH = 256
GATE_ROWS = 16
MLA_HEAD_GROUP = 4
LOG2_E = 1.4426950408889634
MLA_QK_PAD = 2 * LANE
MASK_NEG = -1e30
INT32_MIN = -(2 ** 31)
MIB = 1024 * 1024

BF16 = jnp.bfloat16
F32 = jnp.float32


def _pick(n, prefs):
    for p in prefs:
        if n % p == 0:
            return p
    raise ValueError(f"no tile in {prefs} divides {n}")


def _params(sem, vmem_mib, flags=None):
    return pltpu.CompilerParams(dimension_semantics=sem, vmem_limit_bytes=vmem_mib * MIB,
                                flags=flags)


def _nt_dot(a, b):
    return lax.dot_general(a, b, (((1,), (1,)), ((), ())), preferred_element_type=F32)


def _rms(x, g):
    ms = jnp.mean(x * x, axis=-1, keepdims=True)
    return x * lax.rsqrt(ms + NORM_EPS) * g


def _rope(x, c, s_up, s_dn, half):
    return x * c + pltpu.roll(x, half, 1) * s_up + pltpu.roll(x, LANE - half, 1) * s_dn


def _rope_tables(t_pad, block, rot):
    half = rot // 2
    inv = ROPE_THETA ** (-jnp.arange(0, rot, 2, dtype=F32) / rot)
    ang = jnp.arange(t_pad, dtype=F32)[:, None] * inv[None, :]
    cos, sin = jnp.cos(ang), jnp.sin(ang)
    lane = np.arange(LANE) % block
    lo = lane < half
    hi = (lane >= half) & (lane < rot)
    idx = np.where(lo, lane, np.where(hi, lane - half, 0))
    cos_l, sin_l = cos[:, idx], sin[:, idx]
    c = jnp.where(lo | hi, cos_l, 1.0)
    s_up = jnp.where(hi, sin_l, 0.0)
    s_dn = jnp.where(lo, -sin_l, 0.0)
    return c.astype(F32), s_up.astype(F32), s_dn.astype(F32)


def _rmsnorm_kernel(x_ref, g_ref, o_ref):
    o_ref[...] = _rms(x_ref[...].astype(F32), g_ref[...]).astype(o_ref.dtype)


def _rmsnorm(x, g, out_dtype):
    m, d = x.shape
    tm = _pick(m, (256, 128))
    return pl.pallas_call(
        _rmsnorm_kernel,
        grid=(m // tm,),
        in_specs=[pl.BlockSpec((tm, d), lambda i: (i, 0)),
                  pl.BlockSpec((1, d), lambda i: (0, 0))],
        out_specs=pl.BlockSpec((tm, d), lambda i: (i, 0)),
        out_shape=jax.ShapeDtypeStruct((m, d), out_dtype),
        compiler_params=_params(("parallel",), 40),
        name="rmsnorm",
    )(x, g.reshape(1, d).astype(F32))


def _mm_kernel(a_ref, b_ref, o_ref):
    o_ref[...] = jnp.dot(a_ref[...], b_ref[...], preferred_element_type=F32).astype(o_ref.dtype)


def _matmul(a, b, out_dtype, name):
    m, k = a.shape
    _, n = b.shape
    tm = _pick(m, (512, 256, 128))
    tn = _pick(n, (1024, 512, 256, 128))
    return pl.pallas_call(
        _mm_kernel,
        grid=(n // tn, m // tm),
        in_specs=[pl.BlockSpec((tm, k), lambda j, i: (i, 0)),
                  pl.BlockSpec((k, tn), lambda j, i: (0, j))],
        out_specs=pl.BlockSpec((tm, tn), lambda j, i: (i, j)),
        out_shape=jax.ShapeDtypeStruct((m, n), out_dtype),
        compiler_params=_params(("parallel", "parallel"), 48),
        name=name,
    )(a, b)


def _mla_q_kernel(cq_ref, g_ref, w_ref, c_ref, su_ref, sd_ref, o_ref):
    xn = _rms(cq_ref[...], g_ref[...]).astype(BF16)
    q = jnp.dot(xn, w_ref[...], preferred_element_type=F32)
    q = q * ((MLA_NOPE + MLA_ROPE) ** -0.5 * LOG2_E)
    c, su, sd = c_ref[...], su_ref[...], sd_ref[...]
    for h in range(MLA_HEADS):
        base = h * MLA_QK_PAD
        o_ref[:, base:base + LANE] = q[:, base:base + LANE].astype(BF16)
        o_ref[:, base + LANE:base + 2 * LANE] = _rope(
            q[:, base + LANE:base + 2 * LANE], c, su, sd, MLA_ROPE // 2).astype(BF16)


def _mla_kv_kernel(ckv_ref, g_ref, w_ref, kpe_ref, c_ref, su_ref, sd_ref, k_ref, v_ref):
    xn = _rms(ckv_ref[...], g_ref[...]).astype(BF16)
    kv = jnp.dot(xn, w_ref[...], preferred_element_type=F32)
    kpe = _rope(kpe_ref[...], c_ref[...], su_ref[...], sd_ref[...], MLA_ROPE // 2).astype(BF16)
    nk = MLA_HEADS * MLA_NOPE
    for h in range(MLA_HEADS):
        base = h * MLA_QK_PAD
        k_ref[:, base:base + LANE] = kv[:, h * MLA_NOPE:(h + 1) * MLA_NOPE].astype(BF16)
        k_ref[:, base + LANE:base + 2 * LANE] = kpe
    v_ref[...] = kv[:, nk:].astype(BF16)


def _dsa_prep_kernel(qb_ref, kb_ref, vb_ref, qix_ref, kix_ref, wix_ref,
                     dc_ref, dsu_ref, dsd_ref, ic_ref, isu_ref, isd_ref,
                     qb_o, kb_o, vb_o, qix_o, klo_o, khi_o, w_o):
    dc, dsu, dsd = dc_ref[...], dsu_ref[...], dsd_ref[...]
    ic, isu, isd = ic_ref[...], isu_ref[...], isd_ref[...]
    for h in range(DSA_HEADS):
        sl = slice(h * LANE, (h + 1) * LANE)
        qb = _rope(qb_ref[:, sl], dc, dsu, dsd, DSA_ROT // 2)
        qb_o[:, sl] = (qb * (DSA_HEAD_DIM ** -0.5 * LOG2_E)).astype(BF16)
    for h in range(DSA_KV_HEADS):
        sl = slice(h * LANE, (h + 1) * LANE)
        kb_o[:, sl] = _rope(kb_ref[:, sl], dc, dsu, dsd, DSA_ROT // 2).astype(BF16)
    vb_o[...] = vb_ref[...].astype(BF16)
    for p in range(IDX_HEADS * IDX_DIM // LANE):
        sl = slice(p * LANE, (p + 1) * LANE)
        qix_o[:, sl] = _rope(qix_ref[:, sl], ic, isu, isd, IDX_ROT // 2).astype(BF16)
    kix = _rope(kix_ref[...], ic, isu, isd, IDX_ROT // 2)
    klo_o[...] = kix.astype(BF16)
    khi_o[...] = pltpu.roll(kix, IDX_DIM, 1).astype(BF16)
    w_o[...] = wix_ref[...] * (IDX_HEADS ** -0.5 * IDX_DIM ** -0.5)


def _softmax_steps(scores, values, carries):
    stats = []
    for s, (m, l, _) in zip(scores, carries):
        m_new = jnp.maximum(m, jnp.max(s, axis=1, keepdims=True))
        alpha = jnp.exp2(m - m_new)
        p = jnp.exp2(s - m_new)
        stats.append((m_new, alpha, alpha * l + jnp.sum(p, axis=1, keepdims=True),
                      p.astype(BF16)))
    out = []
    for (m_new, alpha, l, p), v, (_, _, acc) in zip(stats, values, carries):
        out.append((m_new, l, alpha * acc + jnp.dot(p, v, preferred_element_type=F32)))
    return tuple(out)


def _softmax_init(rows, width, chains):
    return tuple((jnp.full((rows, 1), MASK_NEG, F32), jnp.zeros((rows, 1), F32),
                  jnp.zeros((rows, width), F32)) for _ in range(chains))


def _mla_attn_kernel(q_ref, k_ref, v_ref, o_ref, *, tq, heads):
    i = pl.program_id(2)
    qs = [q_ref[:, h * MLA_QK_PAD:(h + 1) * MLA_QK_PAD] for h in range(heads)]

    def chunk(j, carries, diagonal):
        off = pl.multiple_of(j * tq, tq)
        scores = [_nt_dot(qs[h], k_ref[pl.ds(off, tq), h * MLA_QK_PAD:(h + 1) * MLA_QK_PAD])
                  for h in range(heads)]
        if diagonal:
            row = lax.broadcasted_iota(jnp.int32, (tq, tq), 0)
            col = lax.broadcasted_iota(jnp.int32, (tq, tq), 1)
            scores = [jnp.where(col <= row, s, MASK_NEG) for s in scores]
        values = [v_ref[pl.ds(off, tq), h * MLA_V:(h + 1) * MLA_V] for h in range(heads)]
        return _softmax_steps(scores, values, carries)

    carries = lax.fori_loop(0, i, lambda j, c: chunk(j, c, False),
                            _softmax_init(tq, MLA_V, heads))
    carries = chunk(i, carries, True)
    for h, (_, l, acc) in enumerate(carries):
        o_ref[:, h * MLA_V:(h + 1) * MLA_V] = (acc / l).astype(o_ref.dtype)


def _mla_attention(q, k, v, b, t_pad):
    tq = _pick(t_pad, (384, 128))
    hg = MLA_HEAD_GROUP
    q3 = q.reshape(b, t_pad, MLA_HEADS * MLA_QK_PAD)
    k3 = k.reshape(b, t_pad, MLA_HEADS * MLA_QK_PAD)
    v3 = v.reshape(b, t_pad, MLA_HEADS * MLA_V)
    out = pl.pallas_call(
        functools.partial(_mla_attn_kernel, tq=tq, heads=hg),
        grid=(b, MLA_HEADS // hg, t_pad // tq),
        in_specs=[pl.BlockSpec((None, tq, hg * MLA_QK_PAD), lambda bb, h, i: (bb, i, h)),
                  pl.BlockSpec((None, t_pad, hg * MLA_QK_PAD), lambda bb, h, i: (bb, 0, h)),
                  pl.BlockSpec((None, t_pad, hg * MLA_V), lambda bb, h, i: (bb, 0, h))],
        out_specs=pl.BlockSpec((None, tq, hg * MLA_V), lambda bb, h, i: (bb, i, h)),
        out_shape=jax.ShapeDtypeStruct((b, t_pad, MLA_HEADS * MLA_V), BF16),
        compiler_params=_params(("parallel", "parallel", "arbitrary"), 48),
        name="mla_attention",
    )(q3, k3, v3)
    return out.reshape(b * t_pad, MLA_HEADS * MLA_V)


def _dsa_kernel(qb_ref, kb_ref, vb_ref, qix_ref, klo_ref, khi_ref, w_ref, o_ref,
                key_scr, bias_scr, *, tq, tk, topk):
    i = pl.program_id(1)
    n_chunks = ((i + 1) * tq + tk - 1) // tk
    row = i * tq + lax.broadcasted_iota(jnp.int32, (tq, tk), 0)
    col0 = lax.broadcasted_iota(jnp.int32, (tq, tk), 1)
    w = w_ref[...]
    qix = qix_ref[...]

    def score_chunk(c, carry):
        off = pl.multiple_of(c * tk, tk)
        klo = klo_ref[pl.ds(off, tk), :]
        khi = khi_ref[pl.ds(off, tk), :]
        sc = jnp.zeros((tq, tk), F32)
        for h in range(IDX_HEADS):
            qp = qix[:, (h // 2) * LANE:(h // 2 + 1) * LANE]
            d = _nt_dot(qp, klo if h % 2 == 0 else khi)
            sc = sc + w[:, h:h + 1] * jnp.maximum(d, 0.0)
        sc = jnp.where(off + col0 <= row, sc, -jnp.inf)
        bits = pltpu.bitcast(sc, jnp.int32)
        key_scr[:, pl.ds(off, tk)] = bits ^ ((bits >> 31) & jnp.int32(0x7FFFFFFF))
        return carry

    lax.fori_loop(0, n_chunks, score_chunk, 0)

    def bit_step(bi, t):
        cand = t + (jnp.int32(1) << (31 - bi))

        def count_chunk(c, acc):
            off = pl.multiple_of(c * tk, tk)
            ge = jnp.where(key_scr[:, pl.ds(off, tk)] >= cand, 1.0, 0.0)
            for p in range(tk // LANE):
                acc = acc + ge[:, p * LANE:(p + 1) * LANE]
            return acc

        acc = lax.fori_loop(0, n_chunks, count_chunk, jnp.zeros((tq, LANE), F32))
        cnt = jnp.sum(acc, axis=1, keepdims=True)
        return jnp.where(cnt >= topk, cand, t)

    thr = lax.fori_loop(0, 32, bit_step, jnp.full((tq, 1), INT32_MIN, jnp.int32))

    def bias_chunk(c, carry):
        off = pl.multiple_of(c * tk, tk)
        ok = (key_scr[:, pl.ds(off, tk)] >= thr) & (off + col0 <= row)
        bias_scr[:, pl.ds(off, tk)] = jnp.where(ok, 0.0, MASK_NEG)
        return carry

    lax.fori_loop(0, n_chunks, bias_chunk, 0)

    rep = DSA_HEADS // DSA_KV_HEADS
    groups = [slice(g * LANE, (g + 1) * LANE) for g in range(DSA_KV_HEADS)]
    qgs = [jnp.concatenate(
        [qb_ref[:, (g * rep + j) * LANE:(g * rep + j + 1) * LANE] for j in range(rep)], axis=0)
        for g in range(DSA_KV_HEADS)]

    def att_chunk(c, carries):
        off = pl.multiple_of(c * tk, tk)
        bias = bias_scr[:, pl.ds(off, tk)]
        bias = jnp.concatenate([bias] * rep, axis=0)
        scores = [_nt_dot(qg, kb_ref[pl.ds(off, tk), gsl]) + bias for qg, gsl in zip(qgs, groups)]
        values = [vb_ref[pl.ds(off, tk), gsl] for gsl in groups]
        return _softmax_steps(scores, values, carries)

    carries = lax.fori_loop(0, n_chunks, att_chunk,
                            _softmax_init(rep * tq, DSA_HEAD_DIM, DSA_KV_HEADS))
    for g, (_, l, acc) in enumerate(carries):
        o = acc / l
        for j in range(rep):
            hq = g * rep + j
            o_ref[:, hq * LANE:(hq + 1) * LANE] = o[j * tq:(j + 1) * tq].astype(o_ref.dtype)


def _dsa_attention(qb, kb, vb, qix, klo, khi, w, b, t_pad, topk):
    tq = Q_BLOCK
    tk = _pick(t_pad, (384, 128))
    kvw = DSA_KV_HEADS * DSA_HEAD_DIM
    r3 = lambda a: a.reshape(b, t_pad, a.shape[-1])
    qspec = lambda wd: pl.BlockSpec((None, tq, wd), lambda bb, i: (bb, i, 0))
    kspec = lambda wd: pl.BlockSpec((None, t_pad, wd), lambda bb, i: (bb, 0, 0))
    out = pl.pallas_call(
        functools.partial(_dsa_kernel, tq=tq, tk=tk, topk=topk),
        grid=(b, t_pad // tq),
        in_specs=[qspec(DSA_HEADS * DSA_HEAD_DIM), kspec(kvw), kspec(kvw),
                  qspec(IDX_HEADS * IDX_DIM), kspec(LANE), kspec(LANE), qspec(LANE)],
        out_specs=qspec(DSA_HEADS * DSA_HEAD_DIM),
        out_shape=jax.ShapeDtypeStruct((b, t_pad, DSA_HEADS * DSA_HEAD_DIM), BF16),
        scratch_shapes=[pltpu.VMEM((tq, t_pad), jnp.int32), pltpu.VMEM((tq, t_pad), F32)],
        compiler_params=_params(("parallel", "arbitrary"), 48),
        name="dsa_attention",
    )(r3(qb), r3(kb), r3(vb), r3(qix), r3(klo), r3(khi), r3(w))
    return out.reshape(b * t_pad, DSA_HEADS * DSA_HEAD_DIM)


def _sigmoid(x):
    return 1.0 / (1.0 + jnp.exp(-x))


def _branch_kernel(oa_ref, ob_ref, wa_ref, wb_ref, ga_ref, gb_ref, o_ref):
    ya = jnp.dot(oa_ref[...], wa_ref[...], preferred_element_type=F32)
    yb = jnp.dot(ob_ref[...], wb_ref[...], preferred_element_type=F32)
    o_ref[...] = (_sigmoid(ga_ref[...]) * ya + _sigmoid(gb_ref[...]) * yb).astype(o_ref.dtype)


def _residual_mm_kernel(a_ref, b_ref, h_ref, o_ref):
    o_ref[...] = h_ref[...] + jnp.dot(a_ref[...], b_ref[...], preferred_element_type=F32)


def _pack_rows(rows):
    assert len(rows) == SUBLANE
    shape = (SUBLANE, rows[0].shape[1])
    sub = lax.broadcasted_iota(jnp.int32, shape, 0)
    out = jnp.broadcast_to(rows[-1], shape)
    for k in range(SUBLANE - 2, -1, -1):
        out = jnp.where(sub == k, rows[k], out)
    return out


def _peer_select_kernel(qt_ref, keys_ref, s1_ref, s2_ref, e1_ref, e2_ref, tau_ref):
    neg_inf = -jnp.inf
    taus = []
    for h in range(PEER_HEADS):
        scores, tops = [], []
        for c in range(2):
            sl = slice((2 * h + c) * PEER_HALF, (2 * h + c + 1) * PEER_HALF)
            s = jnp.dot(keys_ref[sl, :], qt_ref[sl, :].astype(BF16), preferred_element_type=F32)
            scores.append(s)
            vals, x = [], s
            for _ in range(PEER_TOPK):
                m = jnp.max(x, axis=0, keepdims=True)
                vals.append(m)
                x = jnp.where(x == m, neg_inf, x)
            tops.append(vals)
        a, bv = tops
        b_packs = [_pack_rows(bv[k:k + SUBLANE]) for k in range(0, PEER_TOPK, SUBLANE)]
        sub = lax.broadcasted_iota(jnp.int32, b_packs[0].shape, 0)
        cands, single = [], []
        for i in range(PEER_TOPK):
            partners = PEER_TOPK // (i + 1)
            if partners == 1:
                single.append(a[i])
                continue
            for k in range(0, partners, SUBLANE):
                cd = a[i] + b_packs[k // SUBLANE]
                if partners - k < SUBLANE:
                    cd = jnp.where(sub < partners - k, cd, neg_inf)
                cands.append(cd)
        assert len(single) % SUBLANE == 0
        for k in range(0, len(single), SUBLANE):
            cands.append(_pack_rows(single[k:k + SUBLANE]) + bv[0])
        v0 = a[0] + bv[0]
        z = jnp.zeros_like(v0)
        m = v0
        for _ in range(PEER_TOPK):
            m = jnp.max(functools.reduce(jnp.maximum, cands), axis=0, keepdims=True)
            z = z + jnp.exp(m - v0)
            cands = [jnp.where(cd == m, neg_inf, cd) for cd in cands]
        taus.append(m)
        hs = slice(h * PEER_N_KEYS, (h + 1) * PEER_N_KEYS)
        s1_ref[hs, :] = scores[0]
        s2_ref[hs, :] = scores[1]
        e1_ref[hs, :] = jnp.exp(scores[0] - a[0])
        e2_ref[hs, :] = jnp.exp(scores[1] - bv[0]) / z
    tau_ref[...] = jnp.concatenate(taus, axis=0)


def _peer_select(qt):
    rows, n = qt[0].shape
    tm = _pick(n, (256, 128))
    qt_arr, keys = qt
    hk = PEER_HEADS * PEER_N_KEYS
    big = lambda: pl.BlockSpec((hk, tm), lambda i: (0, i))
    return pl.pallas_call(
        _peer_select_kernel,
        grid=(n // tm,),
        in_specs=[pl.BlockSpec((rows, tm), lambda i: (0, i)),
                  pl.BlockSpec(keys.shape, lambda i: (0, 0))],
        out_specs=[big(), big(), big(), big(), pl.BlockSpec((PEER_HEADS, tm), lambda i: (0, i))],
        out_shape=[jax.ShapeDtypeStruct((hk, n), F32)] * 4
        + [jax.ShapeDtypeStruct((PEER_HEADS, n), F32)],
        compiler_params=_params(("parallel",), 40),
        name="peer_select",
    )(qt_arr, keys)


def _gelu_tanh(x):
    c = np.float32(np.sqrt(2.0 / np.pi))
    return x * (0.5 * (1.0 + jnp.tanh(c * (x + 0.044715 * (x * x * x)))))


def _peer_main_kernel(hn_ref, u_ref, vt_ref, s1_ref, s2_ref, e1_ref, e2_ref, tau_ref, o_ref,
                      raw_scr, a_scr, *, rows_per_tile, n_tiles):
    j = pl.program_id(1)
    last_slot = (n_tiles - 1) % 2

    def first_matmul(slot):
        raw_scr[slot] = jnp.dot(u_ref[...], hn_ref[...], preferred_element_type=F32)

    def gate_and_project(slot, init):
        tm = o_ref.shape[1]
        for rr in range(rows_per_tile):
            r = (j - 1) * rows_per_tile + rr
            s1_rows = [jnp.broadcast_to(s1_ref[pl.ds(h * PEER_N_KEYS + r, 1), :], (GATE_ROWS, tm))
                       for h in range(PEER_HEADS)]
            e1_rows = [jnp.broadcast_to(e1_ref[pl.ds(h * PEER_N_KEYS + r, 1), :], (GATE_ROWS, tm))
                       for h in range(PEER_HEADS)]
            taus = [jnp.broadcast_to(tau_ref[h:h + 1, :], (GATE_ROWS, tm))
                    for h in range(PEER_HEADS)]
            for c0 in range(0, PEER_N_KEYS, GATE_ROWS):
                gate = None
                for h in range(PEER_HEADS):
                    cs = slice(h * PEER_N_KEYS + c0, h * PEER_N_KEYS + c0 + GATE_ROWS)
                    hit = (s2_ref[cs, :] + s1_rows[h]) >= taus[h]
                    term = jnp.where(hit, e2_ref[cs, :] * e1_rows[h], 0.0)
                    gate = term if gate is None else gate + term
                es = slice(rr * PEER_N_KEYS + c0, rr * PEER_N_KEYS + c0 + GATE_ROWS)
                a_scr[es, :] = (_gelu_tanh(raw_scr[slot, es, :]) * gate).astype(BF16)
        contrib = jnp.dot(vt_ref[...], a_scr[...], preferred_element_type=F32)
        o_ref[...] = contrib if init else o_ref[...] + contrib

    @pl.when(j == 0)
    def _():
        first_matmul(0)

    @pl.when(j == 1)
    def _():
        first_matmul(1)
        gate_and_project(0, init=True)

    @pl.when((j > 1) & (j < n_tiles) & (j % 2 == 0))
    def _():
        first_matmul(0)
        gate_and_project(1, init=False)

    @pl.when((j > 1) & (j < n_tiles) & (j % 2 == 1))
    def _():
        first_matmul(1)
        gate_and_project(0, init=False)

    @pl.when(j == n_tiles)
    def _():
        gate_and_project(last_slot, init=False)


def _peer_main(hn_t, u, vt, s1, s2, e1, e2, tau):
    d, n = hn_t.shape
    n_exp = u.shape[0]
    tm = _pick(n, (512, 256, 128))
    te = 512
    hk = PEER_HEADS * PEER_N_KEYS
    tok = lambda rows: pl.BlockSpec((rows, tm), lambda i, j: (0, i), pipeline_mode=pl.Buffered(1))
    n_tiles = n_exp // te
    assert n_tiles >= 2
    return pl.pallas_call(
        functools.partial(_peer_main_kernel, rows_per_tile=te // PEER_N_KEYS, n_tiles=n_tiles),
        grid=(n // tm, n_tiles + 1),
        in_specs=[tok(d),
                  pl.BlockSpec((te, d), lambda i, j: (jnp.minimum(j, n_tiles - 1), 0)),
                  pl.BlockSpec((d, te), lambda i, j: (0, jnp.maximum(j - 1, 0))),
                  tok(hk), tok(hk), tok(hk), tok(hk), tok(PEER_HEADS)],
        out_specs=pl.BlockSpec((d, tm), lambda i, j: (0, i)),
        out_shape=jax.ShapeDtypeStruct((d, n), F32),
        scratch_shapes=[pltpu.VMEM((2, te, tm), F32), pltpu.VMEM((te, tm), BF16)],
        compiler_params=_params(("parallel", "arbitrary"), 58),
        name="peer_main",
    )(hn_t, u, vt, s1, s2, e1, e2, tau)


def _rmsnorm_t_kernel(x_ref, g_ref, o_ref):
    o_ref[...] = _rms(x_ref[...], g_ref[...]).T.astype(o_ref.dtype)


def _final_kernel(h_ref, p_ref, g_ref, o_ref):
    o_ref[...] = _rms(h_ref[...] + p_ref[...].T, g_ref[...])


def _in_proj_layout(d):
    segs = [("gate_a", d), ("gate_b", d), ("q_b", 2048), ("c_q", 1024), ("q_ix", 1024),
            ("c_kv", 512), ("k_b", 512), ("v_b", 512), ("k_pe", LANE), ("k_ix", LANE),
            ("w_ix", LANE)]
    segs.sort(key=lambda s: -s[1])
    offs, o = {}, 0
    for name, wd in segs:
        assert o % wd == 0
        offs[name] = (o, wd)
        o += wd
    total = -(-o // 1024) * 1024
    return offs, total


def _build_w_in(w_in, d):
    src = {}
    o = 0
    for name, wd in (("c_q", MLA_Q_LORA), ("c_kv", MLA_KV_LORA), ("k_pe", MLA_ROPE),
                     ("q_b", 2048), ("k_b", 512), ("v_b", 512), ("q_ix", 1024),
                     ("k_ix", IDX_DIM), ("w_ix", IDX_HEADS), ("gate_a", d), ("gate_b", d)):
        src[name] = w_in[:, o:o + wd]
        o += wd
    offs, total = _in_proj_layout(d)
    cols = []
    o = 0
    for name, (off, wd) in sorted(offs.items(), key=lambda kv: kv[1][0]):
        blk = src[name]
        cols.append(jnp.pad(blk, ((0, 0), (0, wd - blk.shape[1]))))
        o = off + wd
    cols.append(jnp.zeros((d, total - o), w_in.dtype))
    return jnp.concatenate(cols, axis=1).astype(BF16), offs


def kernel(x, meta_tokens, attn_norm_g, w_in, q_norm_g, w_uq, kv_norm_g, w_ukv, w_branch, w_out,
           ffn_norm_g, peer_w_q, peer_sub_keys, peer_u, peer_v, final_norm_g):
    b, s, d = x.shape
    assert attn_norm_g.shape[0] == 1, "single-layer block"
    assert s % Q_BLOCK == 0 and d % 512 == 0 and (d & (d - 1)) == 0
    t = s + N_META
    t_pad = -(-t // Q_BLOCK) * Q_BLOCK
    topk = min(DSA_TOPK_MAX, s // 4)
    n = b * t_pad

    meta = jnp.broadcast_to(meta_tokens.astype(x.dtype)[None], (b, N_META, d))
    h0 = jnp.concatenate([meta, x, jnp.zeros((b, t_pad - t, d), x.dtype)], axis=1).reshape(n, d)

    w_in_p, offs = _build_w_in(w_in[0], d)
    wq = w_uq[0].reshape(MLA_Q_LORA, MLA_HEADS, MLA_NOPE + MLA_ROPE)
    wq = jnp.pad(wq, ((0, 0), (0, 0), (0, MLA_QK_PAD - MLA_NOPE - MLA_ROPE)))
    wq = wq.reshape(MLA_Q_LORA, MLA_HEADS * MLA_QK_PAD).astype(BF16)
    wkv = w_ukv[0].reshape(MLA_KV_LORA, MLA_HEADS, MLA_NOPE + MLA_V)
    wkv = jnp.concatenate([wkv[:, :, :MLA_NOPE].reshape(MLA_KV_LORA, -1),
                           wkv[:, :, MLA_NOPE:].reshape(MLA_KV_LORA, -1)], axis=1).astype(BF16)
    w_br = w_branch[0].astype(BF16)
    w_o = w_out[0].astype(BF16)
    w_pq_t = peer_w_q[0].T.astype(BF16)
    keys = peer_sub_keys[0].reshape(PEER_HEADS * 2 * PEER_N_KEYS, PEER_HALF).astype(BF16)
    u_tab = peer_u[0].astype(BF16)
    v_tab_t = peer_v[0].T.astype(BF16)

    mla_tabs = _rope_tables(t_pad, LANE, MLA_ROPE)
    dsa_tabs = _rope_tables(t_pad, LANE, DSA_ROT)
    idx_tabs = _rope_tables(t_pad, IDX_DIM, IDX_ROT)

    hn = _rmsnorm(h0, attn_norm_g[0], BF16)
    proj = _matmul(hn, w_in_p, F32, "in_proj")

    tm = _pick(t_pad, (384, 128))
    n_pos = t_pad // tm
    row = lambda wd, off: pl.BlockSpec((tm, wd), lambda i, o=off // wd: (i, o))
    seg = lambda name: row(offs[name][1], offs[name][0])
    tab = pl.BlockSpec((tm, LANE), lambda i: (i % n_pos, 0))
    full = lambda a: pl.BlockSpec(a.shape, lambda i: (0,) * a.ndim)
    out_row = lambda wd: pl.BlockSpec((tm, wd), lambda i: (i, 0))
    sds = lambda wd, dt: jax.ShapeDtypeStruct((n, wd), dt)

    qg = q_norm_g[0].reshape(1, -1).astype(F32)
    q_a = pl.pallas_call(
        _mla_q_kernel, grid=(n // tm,),
        in_specs=[seg("c_q"), full(qg), full(wq), tab, tab, tab],
        out_specs=out_row(MLA_HEADS * MLA_QK_PAD),
        out_shape=sds(MLA_HEADS * MLA_QK_PAD, BF16),
        compiler_params=_params(("parallel",), 48), name="mla_q_prep",
    )(proj, qg, wq, *mla_tabs)

    kg = kv_norm_g[0].reshape(1, -1).astype(F32)
    k_a, v_a = pl.pallas_call(
        _mla_kv_kernel, grid=(n // tm,),
        in_specs=[seg("c_kv"), full(kg), full(wkv), seg("k_pe"), tab, tab, tab],
        out_specs=[out_row(MLA_HEADS * MLA_QK_PAD), out_row(MLA_HEADS * MLA_V)],
        out_shape=[sds(MLA_HEADS * MLA_QK_PAD, BF16), sds(MLA_HEADS * MLA_V, BF16)],
        compiler_params=_params(("parallel",), 48), name="mla_kv_prep",
    )(proj, kg, wkv, proj, *mla_tabs)

    kvw = DSA_KV_HEADS * DSA_HEAD_DIM
    q_b, k_b, v_b, q_ix, k_lo, k_hi, w_ix = pl.pallas_call(
        _dsa_prep_kernel, grid=(n // tm,),
        in_specs=[seg("q_b"), seg("k_b"), seg("v_b"), seg("q_ix"), seg("k_ix"), seg("w_ix")]
        + [tab] * 6,
        out_specs=[out_row(2048), out_row(kvw), out_row(kvw), out_row(1024),
                   out_row(LANE), out_row(LANE), out_row(LANE)],
        out_shape=[sds(2048, BF16), sds(kvw, BF16), sds(kvw, BF16), sds(1024, BF16),
                   sds(LANE, BF16), sds(LANE, BF16), sds(LANE, F32)],
        compiler_params=_params(("parallel",), 48), name="dsa_prep",
    )(proj, proj, proj, proj, proj, proj, *dsa_tabs, *idx_tabs)

    o_a = _mla_attention(q_a, k_a, v_a, b, t_pad)
    o_b = _dsa_attention(q_b, k_b, v_b, q_ix, k_lo, k_hi, w_ix, b, t_pad, topk)

    tmm = _pick(n, (512, 256, 128))
    tn = _pick(d, (1024, 512))
    ga_off, gb_off = offs["gate_a"][0] // tn, offs["gate_b"][0] // tn
    bw = MLA_HEADS * MLA_V
    merged = pl.pallas_call(
        _branch_kernel, grid=(d // tn, n // tmm),
        in_specs=[pl.BlockSpec((tmm, bw), lambda j, i: (i, 0)),
                  pl.BlockSpec((tmm, bw), lambda j, i: (i, 0)),
                  pl.BlockSpec((bw, tn), lambda j, i: (0, j)),
                  pl.BlockSpec((bw, tn), lambda j, i: (1, j)),
                  pl.BlockSpec((tmm, tn), lambda j, i: (i, ga_off + j)),
                  pl.BlockSpec((tmm, tn), lambda j, i: (i, gb_off + j))],
        out_specs=pl.BlockSpec((tmm, tn), lambda j, i: (i, j)),
        out_shape=jax.ShapeDtypeStruct((n, d), BF16),
        compiler_params=_params(("parallel", "parallel"), 48), name="branch_merge",
    )(o_a, o_b, w_br, w_br, proj, proj)

    h1 = pl.pallas_call(
        _residual_mm_kernel, grid=(d // tn, n // tmm),
        in_specs=[pl.BlockSpec((tmm, d), lambda j, i: (i, 0)),
                  pl.BlockSpec((d, tn), lambda j, i: (0, j)),
                  pl.BlockSpec((tmm, tn), lambda j, i: (i, j))],
        out_specs=pl.BlockSpec((tmm, tn), lambda j, i: (i, j)),
        out_shape=jax.ShapeDtypeStruct((n, d), F32),
        compiler_params=_params(("parallel", "parallel"), 48), name="out_proj",
    )(merged, w_o, h0)

    np_tok = b * s
    tf = _pick(s, (256, 128))
    tiles_per_seq = s // tf
    real_rows = pl.BlockSpec((pl.Element(tf), pl.Element(d)),
                             lambda bb, t: (pl.multiple_of(bb * t_pad + N_META + t * tf, SUBLANE), 0))
    vec = pl.BlockSpec((1, d), lambda bb, t: (0, 0))

    hn2_t = pl.pallas_call(
        _rmsnorm_t_kernel, grid=(b, tiles_per_seq),
        in_specs=[real_rows, vec],
        out_specs=pl.BlockSpec((d, tf), lambda bb, t: (0, bb * tiles_per_seq + t)),
        out_shape=jax.ShapeDtypeStruct((d, np_tok), BF16),
        compiler_params=_params(("parallel", "parallel"), 40), name="ffn_norm_t",
    )(h1, ffn_norm_g[0].reshape(1, d).astype(F32))
    q_t = _matmul(w_pq_t, hn2_t, F32, "peer_query")
    s1, s2, e1, e2, tau = _peer_select((q_t, keys))
    peer_t = _peer_main(hn2_t, u_tab, v_tab_t, s1, s2, e1, e2, tau)

    out = pl.pallas_call(
        _final_kernel, grid=(b, tiles_per_seq),
        in_specs=[real_rows,
                  pl.BlockSpec((d, tf), lambda bb, t: (0, bb * tiles_per_seq + t)), vec],
        out_specs=pl.BlockSpec((tf, d), lambda bb, t: (bb * tiles_per_seq + t, 0)),
        out_shape=jax.ShapeDtypeStruct((np_tok, d), F32),
        compiler_params=_params(("parallel", "parallel"), 40), name="final_norm",
    )(h1, peer_t, final_norm_g.reshape(1, d).astype(F32))
    return out.reshape(b, s, d)
```

```python
import functools

import jax
import jax.numpy as jnp
import numpy as np
from jax import lax
from jax.experimental import pallas as pl
from jax.experimental.pallas import tpu as pltpu

N_META = 16
ROPE_THETA = 500000.0
Q_BLOCK = 128
NORM_EPS = 1e-6
MLA_HEADS = 16
MLA_Q_LORA = 1024
MLA_KV_LORA = 512
MLA_NOPE = 128
MLA_ROPE = 64
MLA_V = 128
DSA_HEADS = 16
DSA_KV_HEADS = 4
DSA_HEAD_DIM = 128
DSA_ROT = DSA_HEAD_DIM // 4
IDX_HEADS = 16
IDX_DIM = 64
IDX_ROT = IDX_DIM // 4
DSA_TOPK_MAX = 256
PEER_HEADS = 8
PEER_N_KEYS = 128
PEER_HALF = 128
PEER_TOPK = 16

LANE = 128
SUBLANE = 8
MXU_WIDTH = 256
GATE_ROWS = 16
MLA_HEAD_GROUP = 4
LOG2_E = 1.4426950408889634
SCORE_DTYPE = jnp.bfloat16
MLA_QK_PAD = 2 * LANE
MASK_NEG = -1e30
INT32_MIN = -(2 ** 31)
MIB = 1024 * 1024

BF16 = jnp.bfloat16
F32 = jnp.float32


def _pick(n, prefs):
    for p in prefs:
        if n % p == 0:
            return p
    raise ValueError(f"no tile in {prefs} divides {n}")


def _params(sem, vmem_mib, flags=None):
    return pltpu.CompilerParams(dimension_semantics=sem, vmem_limit_bytes=vmem_mib * MIB,
                                flags=flags)


def _nt_dot(a, b, out_dtype=F32):
    out = lax.dot_general(a, b, (((1,), (1,)), ((), ())), preferred_element_type=F32)
    return out.astype(out_dtype)


def _rms(x, g):
    ms = jnp.mean(x * x, axis=-1, keepdims=True)
    return x * lax.rsqrt(ms + NORM_EPS) * g


def _rope(x, c, s_up, s_dn, half):
    return x * c + pltpu.roll(x, half, 1) * s_up + pltpu.roll(x, LANE - half, 1) * s_dn


def _rope_tables(t_pad, block, rot):
    half = rot // 2
    inv = ROPE_THETA ** (-jnp.arange(0, rot, 2, dtype=F32) / rot)
    ang = jnp.arange(t_pad, dtype=F32)[:, None] * inv[None, :]
    cos, sin = jnp.cos(ang), jnp.sin(ang)
    lane = np.arange(LANE) % block
    lo = lane < half
    hi = (lane >= half) & (lane < rot)
    idx = np.where(lo, lane, np.where(hi, lane - half, 0))
    cos_l, sin_l = cos[:, idx], sin[:, idx]
    c = jnp.where(lo | hi, cos_l, 1.0)
    s_up = jnp.where(hi, sin_l, 0.0)
    s_dn = jnp.where(lo, -sin_l, 0.0)
    return c.astype(F32), s_up.astype(F32), s_dn.astype(F32)


def _rmsnorm_kernel(x_ref, g_ref, o_ref):
    o_ref[...] = _rms(x_ref[...].astype(F32), g_ref[...]).astype(o_ref.dtype)


def _rmsnorm(x, g, out_dtype):
    m, d = x.shape
    tm = _pick(m, (256, 128))
    return pl.pallas_call(
        _rmsnorm_kernel,
        grid=(m // tm,),
        in_specs=[pl.BlockSpec((tm, d), lambda i: (i, 0)),
                  pl.BlockSpec((1, d), lambda i: (0, 0))],
        out_specs=pl.BlockSpec((tm, d), lambda i: (i, 0)),
        out_shape=jax.ShapeDtypeStruct((m, d), out_dtype),
        compiler_params=_params(("parallel",), 40),
        name="rmsnorm",
    )(x, g.reshape(1, d).astype(F32))


def _mm_kernel(a_ref, b_ref, o_ref):
    o_ref[...] = jnp.dot(a_ref[...], b_ref[...], preferred_element_type=F32).astype(o_ref.dtype)


def _matmul(a, b, out_dtype, name):
    m, k = a.shape
    _, n = b.shape
    tm = _pick(m, (512, 256, 128))
    tn = _pick(n, (1024, 512, 256, 128))
    return pl.pallas_call(
        _mm_kernel,
        grid=(n // tn, m // tm),
        in_specs=[pl.BlockSpec((tm, k), lambda j, i: (i, 0)),
                  pl.BlockSpec((k, tn), lambda j, i: (0, j))],
        out_specs=pl.BlockSpec((tm, tn), lambda j, i: (i, j)),
        out_shape=jax.ShapeDtypeStruct((m, n), out_dtype),
        compiler_params=_params(("parallel", "parallel"), 48),
        name=name,
    )(a, b)


def _mla_q_kernel(cq_ref, g_ref, w_ref, c_ref, su_ref, sd_ref, o_ref):
    xn = _rms(cq_ref[...], g_ref[...]).astype(BF16)
    q = jnp.dot(xn, w_ref[...], preferred_element_type=F32)
    q = q * ((MLA_NOPE + MLA_ROPE) ** -0.5 * LOG2_E)
    c, su, sd = c_ref[...], su_ref[...], sd_ref[...]
    for h in range(MLA_HEADS):
        base = h * MLA_QK_PAD
        o_ref[:, base:base + LANE] = q[:, base:base + LANE].astype(BF16)
        o_ref[:, base + LANE:base + 2 * LANE] = _rope(
            q[:, base + LANE:base + 2 * LANE], c, su, sd, MLA_ROPE // 2).astype(BF16)


def _mla_kv_kernel(ckv_ref, g_ref, w_ref, kpe_ref, c_ref, su_ref, sd_ref, k_ref, v_ref):
    xn = _rms(ckv_ref[...], g_ref[...]).astype(BF16)
    kv = jnp.dot(xn, w_ref[...], preferred_element_type=F32)
    kpe = _rope(kpe_ref[...], c_ref[...], su_ref[...], sd_ref[...], MLA_ROPE // 2).astype(BF16)
    nk = MLA_HEADS * MLA_NOPE
    for h in range(MLA_HEADS):
        base = h * MLA_QK_PAD
        k_ref[:, base:base + LANE] = kv[:, h * MLA_NOPE:(h + 1) * MLA_NOPE].astype(BF16)
        k_ref[:, base + LANE:base + 2 * LANE] = kpe
        v_ref[:, base:base + MLA_V] = kv[:, nk + h * MLA_V:nk + (h + 1) * MLA_V].astype(BF16)
        v_ref[:, base + MLA_V:base + 2 * MLA_V] = jnp.ones((kv.shape[0], MLA_V), BF16)


def _dsa_prep_kernel(qb_ref, kb_ref, vb_ref, qix_ref, kix_ref, wix_ref,
                     dc_ref, dsu_ref, dsd_ref, ic_ref, isu_ref, isd_ref,
                     qb_o, kb_o, vb_o, qix_o, klo_o, khi_o, w_o):
    dc, dsu, dsd = dc_ref[...], dsu_ref[...], dsd_ref[...]
    ic, isu, isd = ic_ref[...], isu_ref[...], isd_ref[...]
    for h in range(DSA_HEADS):
        sl = slice(h * LANE, (h + 1) * LANE)
        qb = _rope(qb_ref[:, sl], dc, dsu, dsd, DSA_ROT // 2)
        qb_o[:, sl] = (qb * (DSA_HEAD_DIM ** -0.5 * LOG2_E)).astype(BF16)
    for h in range(DSA_KV_HEADS):
        sl = slice(h * LANE, (h + 1) * LANE)
        kb_o[:, sl] = _rope(kb_ref[:, sl], dc, dsu, dsd, DSA_ROT // 2).astype(BF16)
    for h in range(DSA_KV_HEADS):
        vb_o[:, 2 * h * LANE:(2 * h + 1) * LANE] = vb_ref[:, h * LANE:(h + 1) * LANE].astype(BF16)
        vb_o[:, (2 * h + 1) * LANE:(2 * h + 2) * LANE] = jnp.ones((vb_ref.shape[0], LANE), BF16)
    for p in range(IDX_HEADS * IDX_DIM // LANE):
        sl = slice(p * LANE, (p + 1) * LANE)
        qix_o[:, sl] = _rope(qix_ref[:, sl], ic, isu, isd, IDX_ROT // 2).astype(BF16)
    kix = _rope(kix_ref[...], ic, isu, isd, IDX_ROT // 2)
    klo_o[...] = kix.astype(BF16)
    khi_o[...] = pltpu.roll(kix, IDX_DIM, 1).astype(BF16)
    w_o[...] = wix_ref[...] * (IDX_HEADS ** -0.5 * IDX_DIM ** -0.5)


def _softmax_steps(scores, values, carries):
    stats = []
    for s, (m, _, _) in zip(scores, carries):
        m_new = jnp.maximum(m, jnp.max(s, axis=1, keepdims=True).astype(F32))
        alpha = jnp.exp2(m - m_new)
        p = jnp.exp2(s - m_new.astype(s.dtype))
        stats.append((m_new, alpha, p.astype(BF16)))
    out = []
    for (m_new, alpha, p), v, (_, l, acc) in zip(stats, values, carries):
        pv = jnp.dot(p, v, preferred_element_type=F32)
        width = acc.shape[1]
        out.append((m_new, alpha * l + pv[:, width:width + 1], alpha * acc + pv[:, :width]))
    return tuple(out)


def _softmax_init(rows, width, chains):
    return tuple((jnp.full((rows, 1), MASK_NEG, F32), jnp.zeros((rows, 1), F32),
                  jnp.zeros((rows, width), F32)) for _ in range(chains))


def _mla_attn_kernel(q_ref, k_ref, v_ref, o_ref, *, tq, heads):
    i = pl.program_id(2)
    qs = [q_ref[:, h * MLA_QK_PAD:(h + 1) * MLA_QK_PAD] for h in range(heads)]

    def chunk(j, carries, diagonal):
        off = pl.multiple_of(j * tq, tq)
        scores = [_nt_dot(qs[h], k_ref[pl.ds(off, tq), h * MLA_QK_PAD:(h + 1) * MLA_QK_PAD],
                          SCORE_DTYPE) for h in range(heads)]
        if diagonal:
            row = lax.broadcasted_iota(jnp.int32, (tq, tq), 0)
            col = lax.broadcasted_iota(jnp.int32, (tq, tq), 1)
            scores = [jnp.where(col <= row, s, MASK_NEG) for s in scores]
        values = [v_ref[pl.ds(off, tq), h * 2 * MLA_V:(h + 1) * 2 * MLA_V] for h in range(heads)]
        return _softmax_steps(scores, values, carries)

    carries = lax.fori_loop(0, i, lambda j, c: chunk(j, c, False),
                            _softmax_init(tq, MLA_V, heads))
    carries = chunk(i, carries, True)
    for h, (_, l, acc) in enumerate(carries):
        o_ref[:, h * MLA_V:(h + 1) * MLA_V] = (acc / l).astype(o_ref.dtype)


def _mla_attention(q, k, v, b, t_pad):
    tq = _pick(t_pad, (384, 128))
    hg = MLA_HEAD_GROUP
    q3 = q.reshape(b, t_pad, MLA_HEADS * MLA_QK_PAD)
    k3 = k.reshape(b, t_pad, MLA_HEADS * MLA_QK_PAD)
    v3 = v.reshape(b, t_pad, MLA_HEADS * 2 * MLA_V)
    out = pl.pallas_call(
        functools.partial(_mla_attn_kernel, tq=tq, heads=hg),
        grid=(b, MLA_HEADS // hg, t_pad // tq),
        in_specs=[pl.BlockSpec((None, tq, hg * MLA_QK_PAD), lambda bb, h, i: (bb, i, h)),
                  pl.BlockSpec((None, t_pad, hg * MLA_QK_PAD), lambda bb, h, i: (bb, 0, h)),
                  pl.BlockSpec((None, t_pad, hg * 2 * MLA_V), lambda bb, h, i: (bb, 0, h))],
        out_specs=pl.BlockSpec((None, tq, hg * MLA_V), lambda bb, h, i: (bb, i, h)),
        out_shape=jax.ShapeDtypeStruct((b, t_pad, MLA_HEADS * MLA_V), BF16),
        compiler_params=_params(("parallel", "parallel", "arbitrary"), 48),
        name="mla_attention",
    )(q3, k3, v3)
    return out.reshape(b * t_pad, MLA_HEADS * MLA_V)


def _dsa_kernel(qb_ref, kb_ref, vb_ref, qix_ref, klo_ref, khi_ref, w_ref, o_ref,
                key_scr, bias_scr, *, tq, tk, topk):
    i = pl.program_id(1)
    n_chunks = ((i + 1) * tq + tk - 1) // tk
    row = i * tq + lax.broadcasted_iota(jnp.int32, (tq, tk), 0)
    col0 = lax.broadcasted_iota(jnp.int32, (tq, tk), 1)
    w = w_ref[...]
    qix = qix_ref[...]

    def score_chunk(c, carry):
        off = pl.multiple_of(c * tk, tk)
        klo = klo_ref[pl.ds(off, tk), :]
        khi = khi_ref[pl.ds(off, tk), :]
        sc = jnp.zeros((tq, tk), F32)
        for h in range(IDX_HEADS):
            qp = qix[:, (h // 2) * LANE:(h // 2 + 1) * LANE]
            d = _nt_dot(qp, klo if h % 2 == 0 else khi)
            sc = sc + w[:, h:h + 1] * jnp.maximum(d, 0.0)
        sc = jnp.where(off + col0 <= row, sc, -jnp.inf)
        bits = pltpu.bitcast(sc, jnp.int32)
        key_scr[:, pl.ds(off, tk)] = bits ^ ((bits >> 31) & jnp.int32(0x7FFFFFFF))
        return carry

    lax.fori_loop(0, n_chunks, score_chunk, 0)

    def bit_step(bi, t):
        cand = t + (jnp.int32(1) << (31 - bi))

        def count_chunk(c, acc):
            off = pl.multiple_of(c * tk, tk)
            ge = jnp.where(key_scr[:, pl.ds(off, tk)] >= cand, 1.0, 0.0)
            for p in range(tk // LANE):
                acc = acc + ge[:, p * LANE:(p + 1) * LANE]
            return acc

        acc = lax.fori_loop(0, n_chunks, count_chunk, jnp.zeros((tq, LANE), F32))
        cnt = jnp.sum(acc, axis=1, keepdims=True)
        return jnp.where(cnt >= topk, cand, t)

    thr = lax.fori_loop(0, 32, bit_step, jnp.full((tq, 1), INT32_MIN, jnp.int32))

    def bias_chunk(c, carry):
        off = pl.multiple_of(c * tk, tk)
        ok = (key_scr[:, pl.ds(off, tk)] >= thr) & (off + col0 <= row)
        bias_scr[:, pl.ds(off, tk)] = jnp.where(ok, 0.0, MASK_NEG)
        return carry

    lax.fori_loop(0, n_chunks, bias_chunk, 0)

    rep = DSA_HEADS // DSA_KV_HEADS
    groups = [slice(g * LANE, (g + 1) * LANE) for g in range(DSA_KV_HEADS)]
    qgs = [jnp.concatenate(
        [qb_ref[:, (g * rep + j) * LANE:(g * rep + j + 1) * LANE] for j in range(rep)], axis=0)
        for g in range(DSA_KV_HEADS)]

    def att_chunk(c, carries):
        off = pl.multiple_of(c * tk, tk)
        bias = bias_scr[:, pl.ds(off, tk)].astype(SCORE_DTYPE)
        bias = jnp.concatenate([bias] * rep, axis=0)
        scores = [_nt_dot(qg, kb_ref[pl.ds(off, tk), gsl], SCORE_DTYPE) + bias
                  for qg, gsl in zip(qgs, groups)]
        values = [vb_ref[pl.ds(off, tk), 2 * g * LANE:(2 * g + 2) * LANE]
                  for g in range(DSA_KV_HEADS)]
        return _softmax_steps(scores, values, carries)

    carries = lax.fori_loop(0, n_chunks, att_chunk,
                            _softmax_init(rep * tq, DSA_HEAD_DIM, DSA_KV_HEADS))
    for g, (_, l, acc) in enumerate(carries):
        o = acc / l
        for j in range(rep):
            hq = g * rep + j
            o_ref[:, hq * LANE:(hq + 1) * LANE] = o[j * tq:(j + 1) * tq].astype(o_ref.dtype)


def _dsa_attention(qb, kb, vb, qix, klo, khi, w, b, t_pad, topk):
    tq = Q_BLOCK
    tk = _pick(t_pad, (384, 128))
    kvw = DSA_KV_HEADS * DSA_HEAD_DIM
    r3 = lambda a: a.reshape(b, t_pad, a.shape[-1])
    qspec = lambda wd: pl.BlockSpec((None, tq, wd), lambda bb, i: (bb, i, 0))
    kspec = lambda wd: pl.BlockSpec((None, t_pad, wd), lambda bb, i: (bb, 0, 0))
    out = pl.pallas_call(
        functools.partial(_dsa_kernel, tq=tq, tk=tk, topk=topk),
        grid=(b, t_pad // tq),
        in_specs=[qspec(DSA_HEADS * DSA_HEAD_DIM), kspec(kvw), kspec(2 * kvw),
                  qspec(IDX_HEADS * IDX_DIM), kspec(LANE), kspec(LANE), qspec(LANE)],
        out_specs=qspec(DSA_HEADS * DSA_HEAD_DIM),
        out_shape=jax.ShapeDtypeStruct((b, t_pad, DSA_HEADS * DSA_HEAD_DIM), BF16),
        scratch_shapes=[pltpu.VMEM((tq, t_pad), jnp.int32), pltpu.VMEM((tq, t_pad), F32)],
        compiler_params=_params(("parallel", "arbitrary"), 48),
        name="dsa_attention",
    )(r3(qb), r3(kb), r3(vb), r3(qix), r3(klo), r3(khi), r3(w))
    return out.reshape(b * t_pad, DSA_HEADS * DSA_HEAD_DIM)


def _sigmoid(x):
    return 1.0 / (1.0 + jnp.exp(-x))


def _branch_kernel(oa_ref, ob_ref, wa_ref, wb_ref, ga_ref, gb_ref, o_ref):
    ya = jnp.dot(oa_ref[...], wa_ref[...], preferred_element_type=F32)
    yb = jnp.dot(ob_ref[...], wb_ref[...], preferred_element_type=F32)
    o_ref[...] = (_sigmoid(ga_ref[...]) * ya + _sigmoid(gb_ref[...]) * yb).astype(o_ref.dtype)


def _residual_mm_kernel(a_ref, b_ref, h_ref, o_ref):
    o_ref[...] = h_ref[...] + jnp.dot(a_ref[...], b_ref[...], preferred_element_type=F32)


def _pack_rows(rows):
    assert len(rows) == SUBLANE
    shape = (SUBLANE, rows[0].shape[1])
    sub = lax.broadcasted_iota(jnp.int32, shape, 0)
    out = jnp.broadcast_to(rows[-1], shape)
    for k in range(SUBLANE - 2, -1, -1):
        out = jnp.where(sub == k, rows[k], out)
    return out


def _peer_select_kernel(qt_ref, keys_ref, s1_ref, s2_ref, e1_ref, e2_ref, tau_ref):
    neg_inf = -jnp.inf
    taus = []
    for h in range(PEER_HEADS):
        scores, tops = [], []
        for c in range(2):
            sl = slice((2 * h + c) * PEER_HALF, (2 * h + c + 1) * PEER_HALF)
            s = jnp.dot(keys_ref[sl, :], qt_ref[sl, :].astype(BF16), preferred_element_type=F32)
            scores.append(s)
            vals, x = [], s
            for _ in range(PEER_TOPK):
                m = jnp.max(x, axis=0, keepdims=True)
                vals.append(m)
                x = jnp.where(x == m, neg_inf, x)
            tops.append(vals)
        a, bv = tops
        b_packs = [_pack_rows(bv[k:k + SUBLANE]) for k in range(0, PEER_TOPK, SUBLANE)]
        sub = lax.broadcasted_iota(jnp.int32, b_packs[0].shape, 0)
        cands, single = [], []
        for i in range(PEER_TOPK):
            partners = PEER_TOPK // (i + 1)
            if partners == 1:
                single.append(a[i])
                continue
            for k in range(0, partners, SUBLANE):
                cd = a[i] + b_packs[k // SUBLANE]
                if partners - k < SUBLANE:
                    cd = jnp.where(sub < partners - k, cd, neg_inf)
                cands.append(cd)
        assert len(single) % SUBLANE == 0
        for k in range(0, len(single), SUBLANE):
            cands.append(_pack_rows(single[k:k + SUBLANE]) + bv[0])
        v0 = a[0] + bv[0]
        z = jnp.zeros_like(v0)
        m = v0
        for _ in range(PEER_TOPK):
            m = jnp.max(functools.reduce(jnp.maximum, cands), axis=0, keepdims=True)
            z = z + jnp.exp(m - v0)
            cands = [jnp.where(cd == m, neg_inf, cd) for cd in cands]
        taus.append(m)
        hs = slice(h * PEER_N_KEYS, (h + 1) * PEER_N_KEYS)
        s1_ref[hs, :] = scores[0]
        s2_ref[hs, :] = scores[1]
        e1_ref[hs, :] = jnp.exp(scores[0] - a[0])
        e2_ref[hs, :] = jnp.exp(scores[1] - bv[0]) / z
    tau_ref[...] = jnp.concatenate(taus, axis=0)


def _peer_select(qt):
    rows, n = qt[0].shape
    tm = _pick(n, (256, 128))
    qt_arr, keys = qt
    hk = PEER_HEADS * PEER_N_KEYS
    big = lambda: pl.BlockSpec((hk, tm), lambda i: (0, i))
    return pl.pallas_call(
        _peer_select_kernel,
        grid=(n // tm,),
        in_specs=[pl.BlockSpec((rows, tm), lambda i: (0, i)),
                  pl.BlockSpec(keys.shape, lambda i: (0, 0))],
        out_specs=[big(), big(), big(), big(), pl.BlockSpec((PEER_HEADS, tm), lambda i: (0, i))],
        out_shape=[jax.ShapeDtypeStruct((hk, n), F32)] * 4
        + [jax.ShapeDtypeStruct((PEER_HEADS, n), F32)],
        compiler_params=_params(("parallel",), 40),
        name="peer_select",
    )(qt_arr, keys)


def _gelu_tanh(x):
    c = np.float32(np.sqrt(2.0 / np.pi))
    return x * (0.5 * (1.0 + jnp.tanh(c * (x + 0.044715 * (x * x * x)))))


def _peer_main_kernel(hn_ref, u_ref, vt_ref, s1_ref, s2_ref, e1_ref, e2_ref, tau_ref, o_ref,
                      raw_scr, a_scr, *, rows_per_tile, n_tiles):
    j = pl.program_id(1)
    last_slot = (n_tiles - 1) % 2

    def first_matmul(slot):
        raw_scr[slot] = jnp.dot(u_ref[...], hn_ref[...], preferred_element_type=F32)

    def gate_and_project(slot, init):
        tm = o_ref.shape[1]
        for rr in range(rows_per_tile):
            r = (j - 1) * rows_per_tile + rr
            s1_rows = [jnp.broadcast_to(s1_ref[pl.ds(h * PEER_N_KEYS + r, 1), :], (GATE_ROWS, tm))
                       for h in range(PEER_HEADS)]
            e1_rows = [jnp.broadcast_to(e1_ref[pl.ds(h * PEER_N_KEYS + r, 1), :], (GATE_ROWS, tm))
                       for h in range(PEER_HEADS)]
            taus = [jnp.broadcast_to(tau_ref[h:h + 1, :], (GATE_ROWS, tm))
                    for h in range(PEER_HEADS)]
            for c0 in range(0, PEER_N_KEYS, GATE_ROWS):
                gate = None
                for h in range(PEER_HEADS):
                    cs = slice(h * PEER_N_KEYS + c0, h * PEER_N_KEYS + c0 + GATE_ROWS)
                    hit = (s2_ref[cs, :] + s1_rows[h]) >= taus[h]
                    term = jnp.where(hit, e2_ref[cs, :] * e1_rows[h], 0.0)
                    gate = term if gate is None else gate + term
                es = slice(rr * PEER_N_KEYS + c0, rr * PEER_N_KEYS + c0 + GATE_ROWS)
                a_scr[es, :] = (_gelu_tanh(raw_scr[slot, es, :]) * gate).astype(BF16)
        contrib = jnp.dot(vt_ref[...], a_scr[...], preferred_element_type=F32)
        o_ref[...] = contrib if init else o_ref[...] + contrib

    @pl.when(j == 0)
    def _():
        first_matmul(0)

    @pl.when(j == 1)
    def _():
        first_matmul(1)
        gate_and_project(0, init=True)

    @pl.when((j > 1) & (j < n_tiles) & (j % 2 == 0))
    def _():
        first_matmul(0)
        gate_and_project(1, init=False)

    @pl.when((j > 1) & (j < n_tiles) & (j % 2 == 1))
    def _():
        first_matmul(1)
        gate_and_project(0, init=False)

    @pl.when(j == n_tiles)
    def _():
        gate_and_project(last_slot, init=False)


def _peer_main(hn_t, u, vt, s1, s2, e1, e2, tau):
    d, n = hn_t.shape
    n_exp = u.shape[0]
    tm = _pick(n, (512, 256, 128))
    te = 512
    hk = PEER_HEADS * PEER_N_KEYS
    tok = lambda rows: pl.BlockSpec((rows, tm), lambda i, j: (0, i), pipeline_mode=pl.Buffered(1))
    n_tiles = n_exp // te
    assert n_tiles >= 2
    return pl.pallas_call(
        functools.partial(_peer_main_kernel, rows_per_tile=te // PEER_N_KEYS, n_tiles=n_tiles),
        grid=(n // tm, n_tiles + 1),
        in_specs=[tok(d),
                  pl.BlockSpec((te, d), lambda i, j: (jnp.minimum(j, n_tiles - 1), 0)),
                  pl.BlockSpec((d, te), lambda i, j: (0, jnp.maximum(j - 1, 0))),
                  tok(hk), tok(hk), tok(hk), tok(hk), tok(PEER_HEADS)],
        out_specs=pl.BlockSpec((d, tm), lambda i, j: (0, i)),
        out_shape=jax.ShapeDtypeStruct((d, n), F32),
        scratch_shapes=[pltpu.VMEM((2, te, tm), F32), pltpu.VMEM((te, tm), BF16)],
        compiler_params=_params(("parallel", "arbitrary"), 58),
        name="peer_main",
    )(hn_t, u, vt, s1, s2, e1, e2, tau)


def _rmsnorm_t_kernel(x_ref, g_ref, o_ref):
    o_ref[...] = _rms(x_ref[...], g_ref[...]).T.astype(o_ref.dtype)


def _final_kernel(h_ref, p_ref, g_ref, o_ref):
    o_ref[...] = _rms(h_ref[...] + p_ref[...].T, g_ref[...])


def _in_proj_layout(d):
    segs = [("gate_a", d), ("gate_b", d), ("q_b", 2048), ("c_q", 1024), ("q_ix", 1024),
            ("c_kv", 512), ("k_b", 512), ("v_b", 512), ("k_pe", LANE), ("k_ix", LANE),
            ("w_ix", LANE)]
    segs.sort(key=lambda s: -s[1])
    offs, o = {}, 0
    for name, wd in segs:
        assert o % wd == 0
        offs[name] = (o, wd)
        o += wd
    total = -(-o // 1024) * 1024
    return offs, total


def _build_w_in(w_in, d):
    src = {}
    o = 0
    for name, wd in (("c_q", MLA_Q_LORA), ("c_kv", MLA_KV_LORA), ("k_pe", MLA_ROPE),
                     ("q_b", 2048), ("k_b", 512), ("v_b", 512), ("q_ix", 1024),
                     ("k_ix", IDX_DIM), ("w_ix", IDX_HEADS), ("gate_a", d), ("gate_b", d)):
        src[name] = w_in[:, o:o + wd]
        o += wd
    offs, total = _in_proj_layout(d)
    cols = []
    o = 0
    for name, (off, wd) in sorted(offs.items(), key=lambda kv: kv[1][0]):
        blk = src[name]
        cols.append(jnp.pad(blk, ((0, 0), (0, wd - blk.shape[1]))))
        o = off + wd
    cols.append(jnp.zeros((d, total - o), w_in.dtype))
    return jnp.concatenate(cols, axis=1).astype(BF16), offs


def kernel(x, meta_tokens, attn_norm_g, w_in, q_norm_g, w_uq, kv_norm_g, w_ukv, w_branch, w_out,
           ffn_norm_g, peer_w_q, peer_sub_keys, peer_u, peer_v, final_norm_g):
    b, s, d = x.shape
    assert attn_norm_g.shape[0] == 1, "single-layer block"
    assert s % Q_BLOCK == 0 and d % 512 == 0 and (d & (d - 1)) == 0
    assert N_META % SUBLANE == 0
    t = s + N_META
    t_pad = -(-t // Q_BLOCK) * Q_BLOCK
    topk = min(DSA_TOPK_MAX, s // 4)
    n = b * t_pad

    meta = jnp.broadcast_to(meta_tokens.astype(x.dtype)[None], (b, N_META, d))
    h0 = jnp.concatenate([meta, x, jnp.zeros((b, t_pad - t, d), x.dtype)], axis=1).reshape(n, d)

    w_in_p, offs = _build_w_in(w_in[0], d)
    wq = w_uq[0].reshape(MLA_Q_LORA, MLA_HEADS, MLA_NOPE + MLA_ROPE)
    wq = jnp.pad(wq, ((0, 0), (0, 0), (0, MLA_QK_PAD - MLA_NOPE - MLA_ROPE)))
    wq = wq.reshape(MLA_Q_LORA, MLA_HEADS * MLA_QK_PAD).astype(BF16)
    wkv = w_ukv[0].reshape(MLA_KV_LORA, MLA_HEADS, MLA_NOPE + MLA_V)
    wkv = jnp.concatenate([wkv[:, :, :MLA_NOPE].reshape(MLA_KV_LORA, -1),
                           wkv[:, :, MLA_NOPE:].reshape(MLA_KV_LORA, -1)], axis=1).astype(BF16)
    w_br = w_branch[0].astype(BF16)
    w_o = w_out[0].astype(BF16)
    w_pq_t = peer_w_q[0].T.astype(BF16)
    keys = peer_sub_keys[0].reshape(PEER_HEADS * 2 * PEER_N_KEYS, PEER_HALF).astype(BF16)
    u_tab = peer_u[0].astype(BF16)
    v_tab_t = peer_v[0].T.astype(BF16)

    mla_tabs = _rope_tables(t_pad, LANE, MLA_ROPE)
    dsa_tabs = _rope_tables(t_pad, LANE, DSA_ROT)
    idx_tabs = _rope_tables(t_pad, IDX_DIM, IDX_ROT)

    hn = _rmsnorm(h0, attn_norm_g[0], BF16)
    proj = _matmul(hn, w_in_p, F32, "in_proj")

    tm = _pick(t_pad, (384, 128))
    n_pos = t_pad // tm
    row = lambda wd, off: pl.BlockSpec((tm, wd), lambda i, o=off // wd: (i, o))
    seg = lambda name: row(offs[name][1], offs[name][0])
    tab = pl.BlockSpec((tm, LANE), lambda i: (i % n_pos, 0))
    full = lambda a: pl.BlockSpec(a.shape, lambda i: (0,) * a.ndim)
    out_row = lambda wd: pl.BlockSpec((tm, wd), lambda i: (i, 0))
    sds = lambda wd, dt: jax.ShapeDtypeStruct((n, wd), dt)

    qg = q_norm_g[0].reshape(1, -1).astype(F32)
    q_a = pl.pallas_call(
        _mla_q_kernel, grid=(n // tm,),
        in_specs=[seg("c_q"), full(qg), full(wq), tab, tab, tab],
        out_specs=out_row(MLA_HEADS * MLA_QK_PAD),
        out_shape=sds(MLA_HEADS * MLA_QK_PAD, BF16),
        compiler_params=_params(("parallel",), 48), name="mla_q_prep",
    )(proj, qg, wq, *mla_tabs)

    kg = kv_norm_g[0].reshape(1, -1).astype(F32)
    k_a, v_a = pl.pallas_call(
        _mla_kv_kernel, grid=(n // tm,),
        in_specs=[seg("c_kv"), full(kg), full(wkv), seg("k_pe"), tab, tab, tab],
        out_specs=[out_row(MLA_HEADS * MLA_QK_PAD), out_row(MLA_HEADS * 2 * MLA_V)],
        out_shape=[sds(MLA_HEADS * MLA_QK_PAD, BF16), sds(MLA_HEADS * 2 * MLA_V, BF16)],
        compiler_params=_params(("parallel",), 48), name="mla_kv_prep",
    )(proj, kg, wkv, proj, *mla_tabs)

    kvw = DSA_KV_HEADS * DSA_HEAD_DIM
    q_b, k_b, v_b, q_ix, k_lo, k_hi, w_ix = pl.pallas_call(
        _dsa_prep_kernel, grid=(n // tm,),
        in_specs=[seg("q_b"), seg("k_b"), seg("v_b"), seg("q_ix"), seg("k_ix"), seg("w_ix")]
        + [tab] * 6,
        out_specs=[out_row(2048), out_row(kvw), out_row(2 * kvw), out_row(1024),
                   out_row(LANE), out_row(LANE), out_row(LANE)],
        out_shape=[sds(2048, BF16), sds(kvw, BF16), sds(2 * kvw, BF16), sds(1024, BF16),
                   sds(LANE, BF16), sds(LANE, BF16), sds(LANE, F32)],
        compiler_params=_params(("parallel",), 48), name="dsa_prep",
    )(proj, proj, proj, proj, proj, proj, *dsa_tabs, *idx_tabs)

    o_a = _mla_attention(q_a, k_a, v_a, b, t_pad)
    o_b = _dsa_attention(q_b, k_b, v_b, q_ix, k_lo, k_hi, w_ix, b, t_pad, topk)

    tmm = _pick(n, (512, 256, 128))
    tn = _pick(d, (1024, 512))
    ga_off, gb_off = offs["gate_a"][0] // tn, offs["gate_b"][0] // tn
    bw = MLA_HEADS * MLA_V
    merged = pl.pallas_call(
        _branch_kernel, grid=(d // tn, n // tmm),
        in_specs=[pl.BlockSpec((tmm, bw), lambda j, i: (i, 0)),
                  pl.BlockSpec((tmm, bw), lambda j, i: (i, 0)),
                  pl.BlockSpec((bw, tn), lambda j, i: (0, j)),
                  pl.BlockSpec((bw, tn), lambda j, i: (1, j)),
                  pl.BlockSpec((tmm, tn), lambda j, i: (i, ga_off + j)),
                  pl.BlockSpec((tmm, tn), lambda j, i: (i, gb_off + j))],
        out_specs=pl.BlockSpec((tmm, tn), lambda j, i: (i, j)),
        out_shape=jax.ShapeDtypeStruct((n, d), BF16),
        compiler_params=_params(("parallel", "parallel"), 48), name="branch_merge",
    )(o_a, o_b, w_br, w_br, proj, proj)

    h1 = pl.pallas_call(
        _residual_mm_kernel, grid=(d // tn, n // tmm),
        in_specs=[pl.BlockSpec((tmm, d), lambda j, i: (i, 0)),
                  pl.BlockSpec((d, tn), lambda j, i: (0, j)),
                  pl.BlockSpec((tmm, tn), lambda j, i: (i, j))],
        out_specs=pl.BlockSpec((tmm, tn), lambda j, i: (i, j)),
        out_shape=jax.ShapeDtypeStruct((n, d), F32),
        compiler_params=_params(("parallel", "parallel"), 48), name="out_proj",
    )(merged, w_o, h0)

    np_tok = b * s
    tf = _pick(s, (256, 128))
    tiles_per_seq = s // tf
    real_rows = pl.BlockSpec((pl.Element(tf), pl.Element(d)),
                             lambda bb, t: (pl.multiple_of(bb * t_pad + N_META + t * tf, SUBLANE), 0))
    vec = pl.BlockSpec((1, d), lambda bb, t: (0, 0))

    hn2_t = pl.pallas_call(
        _rmsnorm_t_kernel, grid=(b, tiles_per_seq),
        in_specs=[real_rows, vec],
        out_specs=pl.BlockSpec((d, tf), lambda bb, t: (0, bb * tiles_per_seq + t)),
        out_shape=jax.ShapeDtypeStruct((d, np_tok), BF16),
        compiler_params=_params(("parallel", "parallel"), 40), name="ffn_norm_t",
    )(h1, ffn_norm_g[0].reshape(1, d).astype(F32))
    q_t = _matmul(w_pq_t, hn2_t, F32, "peer_query")
    s1, s2, e1, e2, tau = _peer_select((q_t, keys))
    peer_t = _peer_main(hn2_t, u_tab, v_tab_t, s1, s2, e1, e2, tau)

    out = pl.pallas_call(
        _final_kernel, grid=(b, tiles_per_seq),
        in_specs=[real_rows,
                  pl.BlockSpec((d, tf), lambda bb, t: (0, bb * tiles_per_seq + t)), vec],
        out_specs=pl.BlockSpec((tf, d), lambda bb, t: (bb * tiles_per_seq + t, 0)),
        out_shape=jax.ShapeDtypeStruct((np_tok, d), F32),
        compiler_params=_params(("parallel", "parallel"), 40), name="final_norm",
    )(h1, peer_t, final_norm_g.reshape(1, d).astype(F32))
    return out.reshape(b, s, d)
```

```python
import functools

import jax
import jax.numpy as jnp
import numpy as np
from jax import lax
from jax.experimental import pallas as pl
from jax.experimental.pallas import tpu as pltpu

N_META = 16
ROPE_THETA = 500000.0
Q_BLOCK = 128
NORM_EPS = 1e-6
MLA_HEADS = 16
MLA_Q_LORA = 1024
MLA_KV_LORA = 512
MLA_NOPE = 128
MLA_ROPE = 64
MLA_V = 128
DSA_HEADS = 16
DSA_KV_HEADS = 4
DSA_HEAD_DIM = 128
DSA_ROT = DSA_HEAD_DIM // 4
IDX_HEADS = 16
IDX_DIM = 64
IDX_ROT = IDX_DIM // 4
DSA_TOPK_MAX = 256
PEER_HEADS = 8
PEER_N_KEYS = 128
PEER_HALF = 128
PEER_TOPK = 16

LANE = 128
SUBLANE = 8
MXU_WIDTH = 256
GATE_ROWS = 16
MLA_HEAD_GROUP = 4
LOG2_E = 1.4426950408889634
SCORE_DTYPE = jnp.bfloat16
NOT_RANKED = 127.0
MLA_QK_PAD = 2 * LANE
MASK_NEG = -1e30
INT32_MIN = -(2 ** 31)
MIB = 1024 * 1024

BF16 = jnp.bfloat16
F32 = jnp.float32


def _pick(n, prefs):
    for p in prefs:
        if n % p == 0:
            return p
    raise ValueError(f"no tile in {prefs} divides {n}")


def _params(sem, vmem_mib, flags=None):
    return pltpu.CompilerParams(dimension_semantics=sem, vmem_limit_bytes=vmem_mib * MIB,
                                flags=flags)


def _nt_dot(a, b, out_dtype=F32):
    out = lax.dot_general(a, b, (((1,), (1,)), ((), ())), preferred_element_type=F32)
    return out.astype(out_dtype)


def _rms(x, g):
    ms = jnp.mean(x * x, axis=-1, keepdims=True)
    return x * lax.rsqrt(ms + NORM_EPS) * g


def _rope(x, c, s_up, s_dn, half):
    return x * c + pltpu.roll(x, half, 1) * s_up + pltpu.roll(x, LANE - half, 1) * s_dn


def _rope_tables(t_pad, block, rot):
    half = rot // 2
    inv = ROPE_THETA ** (-jnp.arange(0, rot, 2, dtype=F32) / rot)
    ang = jnp.arange(t_pad, dtype=F32)[:, None] * inv[None, :]
    cos, sin = jnp.cos(ang), jnp.sin(ang)
    lane = np.arange(LANE) % block
    lo = lane < half
    hi = (lane >= half) & (lane < rot)
    idx = np.where(lo, lane, np.where(hi, lane - half, 0))
    cos_l, sin_l = cos[:, idx], sin[:, idx]
    c = jnp.where(lo | hi, cos_l, 1.0)
    s_up = jnp.where(hi, sin_l, 0.0)
    s_dn = jnp.where(lo, -sin_l, 0.0)
    return c.astype(F32), s_up.astype(F32), s_dn.astype(F32)


def _rmsnorm_kernel(x_ref, g_ref, o_ref):
    o_ref[...] = _rms(x_ref[...].astype(F32), g_ref[...]).astype(o_ref.dtype)


def _rmsnorm(x, g, out_dtype):
    m, d = x.shape
    tm = _pick(m, (256, 128))
    return pl.pallas_call(
        _rmsnorm_kernel,
        grid=(m // tm,),
        in_specs=[pl.BlockSpec((tm, d), lambda i: (i, 0)),
                  pl.BlockSpec((1, d), lambda i: (0, 0))],
        out_specs=pl.BlockSpec((tm, d), lambda i: (i, 0)),
        out_shape=jax.ShapeDtypeStruct((m, d), out_dtype),
        compiler_params=_params(("parallel",), 40),
        name="rmsnorm",
    )(x, g.reshape(1, d).astype(F32))


def _mm_kernel(a_ref, b_ref, o_ref):
    o_ref[...] = jnp.dot(a_ref[...], b_ref[...], preferred_element_type=F32).astype(o_ref.dtype)


def _matmul(a, b, out_dtype, name):
    m, k = a.shape
    _, n = b.shape
    tm = _pick(m, (512, 256, 128))
    tn = _pick(n, (1024, 512, 256, 128))
    return pl.pallas_call(
        _mm_kernel,
        grid=(n // tn, m // tm),
        in_specs=[pl.BlockSpec((tm, k), lambda j, i: (i, 0)),
                  pl.BlockSpec((k, tn), lambda j, i: (0, j))],
        out_specs=pl.BlockSpec((tm, tn), lambda j, i: (i, j)),
        out_shape=jax.ShapeDtypeStruct((m, n), out_dtype),
        compiler_params=_params(("parallel", "parallel"), 48),
        name=name,
    )(a, b)


def _mla_q_kernel(cq_ref, g_ref, w_ref, c_ref, su_ref, sd_ref, o_ref):
    xn = _rms(cq_ref[...], g_ref[...]).astype(BF16)
    q = jnp.dot(xn, w_ref[...], preferred_element_type=F32)
    q = q * ((MLA_NOPE + MLA_ROPE) ** -0.5 * LOG2_E)
    c, su, sd = c_ref[...], su_ref[...], sd_ref[...]
    for h in range(MLA_HEADS):
        base = h * MLA_QK_PAD
        o_ref[:, base:base + LANE] = q[:, base:base + LANE].astype(BF16)
        o_ref[:, base + LANE:base + 2 * LANE] = _rope(
            q[:, base + LANE:base + 2 * LANE], c, su, sd, MLA_ROPE // 2).astype(BF16)


def _mla_kv_kernel(ckv_ref, g_ref, w_ref, kpe_ref, c_ref, su_ref, sd_ref, k_ref, v_ref):
    xn = _rms(ckv_ref[...], g_ref[...]).astype(BF16)
    kv = jnp.dot(xn, w_ref[...], preferred_element_type=F32)
    kpe = _rope(kpe_ref[...], c_ref[...], su_ref[...], sd_ref[...], MLA_ROPE // 2).astype(BF16)
    nk = MLA_HEADS * MLA_NOPE
    for h in range(MLA_HEADS):
        base = h * MLA_QK_PAD
        k_ref[:, base:base + LANE] = kv[:, h * MLA_NOPE:(h + 1) * MLA_NOPE].astype(BF16)
        k_ref[:, base + LANE:base + 2 * LANE] = kpe
        v_ref[:, base:base + MLA_V] = kv[:, nk + h * MLA_V:nk + (h + 1) * MLA_V].astype(BF16)
        v_ref[:, base + MLA_V:base + 2 * MLA_V] = jnp.ones((kv.shape[0], MLA_V), BF16)


def _dsa_prep_kernel(qb_ref, kb_ref, vb_ref, qix_ref, kix_ref, wix_ref,
                     dc_ref, dsu_ref, dsd_ref, ic_ref, isu_ref, isd_ref,
                     qb_o, kb_o, vb_o, qix_o, klo_o, khi_o, w_o):
    dc, dsu, dsd = dc_ref[...], dsu_ref[...], dsd_ref[...]
    ic, isu, isd = ic_ref[...], isu_ref[...], isd_ref[...]
    for h in range(DSA_HEADS):
        sl = slice(h * LANE, (h + 1) * LANE)
        qb = _rope(qb_ref[:, sl], dc, dsu, dsd, DSA_ROT // 2)
        qb_o[:, sl] = (qb * (DSA_HEAD_DIM ** -0.5 * LOG2_E)).astype(BF16)
    for h in range(DSA_KV_HEADS):
        sl = slice(h * LANE, (h + 1) * LANE)
        kb_o[:, sl] = _rope(kb_ref[:, sl], dc, dsu, dsd, DSA_ROT // 2).astype(BF16)
    for h in range(DSA_KV_HEADS):
        vb_o[:, 2 * h * LANE:(2 * h + 1) * LANE] = vb_ref[:, h * LANE:(h + 1) * LANE].astype(BF16)
        vb_o[:, (2 * h + 1) * LANE:(2 * h + 2) * LANE] = jnp.ones((vb_ref.shape[0], LANE), BF16)
    for p in range(IDX_HEADS * IDX_DIM // LANE):
        sl = slice(p * LANE, (p + 1) * LANE)
        qix_o[:, sl] = _rope(qix_ref[:, sl], ic, isu, isd, IDX_ROT // 2).astype(BF16)
    kix = _rope(kix_ref[...], ic, isu, isd, IDX_ROT // 2)
    klo_o[...] = kix.astype(BF16)
    khi_o[...] = pltpu.roll(kix, IDX_DIM, 1).astype(BF16)
    w_o[...] = wix_ref[...] * (IDX_HEADS ** -0.5 * IDX_DIM ** -0.5)


def _softmax_steps(scores, values, carries):
    stats = []
    for s, (m, _, _) in zip(scores, carries):
        m_new = jnp.maximum(m, jnp.max(s, axis=1, keepdims=True).astype(F32))
        alpha = jnp.exp2(m - m_new)
        p = jnp.exp2(s - m_new.astype(s.dtype))
        stats.append((m_new, alpha, p.astype(BF16)))
    out = []
    for (m_new, alpha, p), v, (_, l, acc) in zip(stats, values, carries):
        pv = jnp.dot(p, v, preferred_element_type=F32)
        width = acc.shape[1]
        out.append((m_new, alpha * l + pv[:, width:width + 1], alpha * acc + pv[:, :width]))
    return tuple(out)


def _softmax_init(rows, width, chains):
    return tuple((jnp.full((rows, 1), MASK_NEG, F32), jnp.zeros((rows, 1), F32),
                  jnp.zeros((rows, width), F32)) for _ in range(chains))


def _mla_attn_kernel(q_ref, k_ref, v_ref, o_ref, *, tq, heads):
    i = pl.program_id(2)
    qs = [q_ref[:, h * MLA_QK_PAD:(h + 1) * MLA_QK_PAD] for h in range(heads)]

    def chunk(j, carries, diagonal):
        off = pl.multiple_of(j * tq, tq)
        scores = [_nt_dot(qs[h], k_ref[pl.ds(off, tq), h * MLA_QK_PAD:(h + 1) * MLA_QK_PAD],
                          SCORE_DTYPE) for h in range(heads)]
        if diagonal:
            row = lax.broadcasted_iota(jnp.int32, (tq, tq), 0)
            col = lax.broadcasted_iota(jnp.int32, (tq, tq), 1)
            scores = [jnp.where(col <= row, s, MASK_NEG) for s in scores]
        values = [v_ref[pl.ds(off, tq), h * 2 * MLA_V:(h + 1) * 2 * MLA_V] for h in range(heads)]
        return _softmax_steps(scores, values, carries)

    carries = lax.fori_loop(0, i, lambda j, c: chunk(j, c, False),
                            _softmax_init(tq, MLA_V, heads))
    carries = chunk(i, carries, True)
    for h, (_, l, acc) in enumerate(carries):
        o_ref[:, h * MLA_V:(h + 1) * MLA_V] = (acc / l).astype(o_ref.dtype)


def _mla_attention(q, k, v, b, t_pad):
    tq = _pick(t_pad, (384, 128))
    hg = MLA_HEAD_GROUP
    q3 = q.reshape(b, t_pad, MLA_HEADS * MLA_QK_PAD)
    k3 = k.reshape(b, t_pad, MLA_HEADS * MLA_QK_PAD)
    v3 = v.reshape(b, t_pad, MLA_HEADS * 2 * MLA_V)
    out = pl.pallas_call(
        functools.partial(_mla_attn_kernel, tq=tq, heads=hg),
        grid=(b, MLA_HEADS // hg, t_pad // tq),
        in_specs=[pl.BlockSpec((None, tq, hg * MLA_QK_PAD), lambda bb, h, i: (bb, i, h)),
                  pl.BlockSpec((None, t_pad, hg * MLA_QK_PAD), lambda bb, h, i: (bb, 0, h)),
                  pl.BlockSpec((None, t_pad, hg * 2 * MLA_V), lambda bb, h, i: (bb, 0, h))],
        out_specs=pl.BlockSpec((None, tq, hg * MLA_V), lambda bb, h, i: (bb, i, h)),
        out_shape=jax.ShapeDtypeStruct((b, t_pad, MLA_HEADS * MLA_V), BF16),
        compiler_params=_params(("parallel", "parallel", "arbitrary"), 48),
        name="mla_attention",
    )(q3, k3, v3)
    return out.reshape(b * t_pad, MLA_HEADS * MLA_V)


def _dsa_kernel(qb_ref, kb_ref, vb_ref, qix_ref, klo_ref, khi_ref, w_ref, o_ref,
                key_scr, bias_scr, *, tq, tk, topk):
    i = pl.program_id(1)
    n_chunks = ((i + 1) * tq + tk - 1) // tk
    row = i * tq + lax.broadcasted_iota(jnp.int32, (tq, tk), 0)
    col0 = lax.broadcasted_iota(jnp.int32, (tq, tk), 1)
    w = w_ref[...]
    qix = qix_ref[...]

    def score_chunk(c, carry):
        off = pl.multiple_of(c * tk, tk)
        klo = klo_ref[pl.ds(off, tk), :]
        khi = khi_ref[pl.ds(off, tk), :]
        sc = jnp.zeros((tq, tk), F32)
        for h in range(IDX_HEADS):
            qp = qix[:, (h // 2) * LANE:(h // 2 + 1) * LANE]
            d = _nt_dot(qp, klo if h % 2 == 0 else khi)
            sc = sc + w[:, h:h + 1] * jnp.maximum(d, 0.0)
        sc = jnp.where(off + col0 <= row, sc, -jnp.inf)
        bits = pltpu.bitcast(sc, jnp.int32)
        key_scr[:, pl.ds(off, tk)] = bits ^ ((bits >> 31) & jnp.int32(0x7FFFFFFF))
        return carry

    lax.fori_loop(0, n_chunks, score_chunk, 0)

    def bit_step(bi, t):
        cand = t + (jnp.int32(1) << (31 - bi))

        def count_chunk(c, acc):
            off = pl.multiple_of(c * tk, tk)
            ge = jnp.where(key_scr[:, pl.ds(off, tk)] >= cand, 1.0, 0.0)
            for p in range(tk // LANE):
                acc = acc + ge[:, p * LANE:(p + 1) * LANE]
            return acc

        acc = lax.fori_loop(0, n_chunks, count_chunk, jnp.zeros((tq, LANE), F32))
        cnt = jnp.sum(acc, axis=1, keepdims=True)
        return jnp.where(cnt >= topk, cand, t)

    thr = lax.fori_loop(0, 32, bit_step, jnp.full((tq, 1), INT32_MIN, jnp.int32))

    def bias_chunk(c, carry):
        off = pl.multiple_of(c * tk, tk)
        ok = (key_scr[:, pl.ds(off, tk)] >= thr) & (off + col0 <= row)
        bias_scr[:, pl.ds(off, tk)] = jnp.where(ok, 0.0, MASK_NEG)
        return carry

    lax.fori_loop(0, n_chunks, bias_chunk, 0)

    rep = DSA_HEADS // DSA_KV_HEADS
    groups = [slice(g * LANE, (g + 1) * LANE) for g in range(DSA_KV_HEADS)]
    qgs = [jnp.concatenate(
        [qb_ref[:, (g * rep + j) * LANE:(g * rep + j + 1) * LANE] for j in range(rep)], axis=0)
        for g in range(DSA_KV_HEADS)]

    def att_chunk(c, carries):
        off = pl.multiple_of(c * tk, tk)
        bias = bias_scr[:, pl.ds(off, tk)].astype(SCORE_DTYPE)
        bias = jnp.concatenate([bias] * rep, axis=0)
        scores = [_nt_dot(qg, kb_ref[pl.ds(off, tk), gsl], SCORE_DTYPE) + bias
                  for qg, gsl in zip(qgs, groups)]
        values = [vb_ref[pl.ds(off, tk), 2 * g * LANE:(2 * g + 2) * LANE]
                  for g in range(DSA_KV_HEADS)]
        return _softmax_steps(scores, values, carries)

    carries = lax.fori_loop(0, n_chunks, att_chunk,
                            _softmax_init(rep * tq, DSA_HEAD_DIM, DSA_KV_HEADS))
    for g, (_, l, acc) in enumerate(carries):
        o = acc / l
        for j in range(rep):
            hq = g * rep + j
            o_ref[:, hq * LANE:(hq + 1) * LANE] = o[j * tq:(j + 1) * tq].astype(o_ref.dtype)


def _dsa_attention(qb, kb, vb, qix, klo, khi, w, b, t_pad, topk):
    tq = Q_BLOCK
    tk = _pick(t_pad, (384, 128))
    kvw = DSA_KV_HEADS * DSA_HEAD_DIM
    r3 = lambda a: a.reshape(b, t_pad, a.shape[-1])
    qspec = lambda wd: pl.BlockSpec((None, tq, wd), lambda bb, i: (bb, i, 0))
    kspec = lambda wd: pl.BlockSpec((None, t_pad, wd), lambda bb, i: (bb, 0, 0))
    out = pl.pallas_call(
        functools.partial(_dsa_kernel, tq=tq, tk=tk, topk=topk),
        grid=(b, t_pad // tq),
        in_specs=[qspec(DSA_HEADS * DSA_HEAD_DIM), kspec(kvw), kspec(2 * kvw),
                  qspec(IDX_HEADS * IDX_DIM), kspec(LANE), kspec(LANE), qspec(LANE)],
        out_specs=qspec(DSA_HEADS * DSA_HEAD_DIM),
        out_shape=jax.ShapeDtypeStruct((b, t_pad, DSA_HEADS * DSA_HEAD_DIM), BF16),
        scratch_shapes=[pltpu.VMEM((tq, t_pad), jnp.int32), pltpu.VMEM((tq, t_pad), F32)],
        compiler_params=_params(("parallel", "arbitrary"), 48),
        name="dsa_attention",
    )(r3(qb), r3(kb), r3(vb), r3(qix), r3(klo), r3(khi), r3(w))
    return out.reshape(b * t_pad, DSA_HEADS * DSA_HEAD_DIM)


def _sigmoid(x):
    return 1.0 / (1.0 + jnp.exp(-x))


def _branch_kernel(oa_ref, ob_ref, wa_ref, wb_ref, ga_ref, gb_ref, o_ref):
    ya = jnp.dot(oa_ref[...], wa_ref[...], preferred_element_type=F32)
    yb = jnp.dot(ob_ref[...], wb_ref[...], preferred_element_type=F32)
    o_ref[...] = (_sigmoid(ga_ref[...]) * ya + _sigmoid(gb_ref[...]) * yb).astype(o_ref.dtype)


def _residual_mm_kernel(a_ref, b_ref, h_ref, o_ref):
    o_ref[...] = h_ref[...] + jnp.dot(a_ref[...], b_ref[...], preferred_element_type=F32)


def _pack_rows(rows):
    assert len(rows) == SUBLANE
    shape = (SUBLANE, rows[0].shape[1])
    sub = lax.broadcasted_iota(jnp.int32, shape, 0)
    out = jnp.broadcast_to(rows[-1], shape)
    for k in range(SUBLANE - 2, -1, -1):
        out = jnp.where(sub == k, rows[k], out)
    return out


def _peer_select_kernel(qt_ref, keys_ref, lim_ref, rank_ref, e1_ref, e2_ref):
    neg_inf = -jnp.inf
    for h in range(PEER_HEADS):
        scores, tops = [], []
        rank2 = None
        for c in range(2):
            sl = slice((2 * h + c) * PEER_HALF, (2 * h + c + 1) * PEER_HALF)
            s = jnp.dot(keys_ref[sl, :], qt_ref[sl, :].astype(BF16), preferred_element_type=F32)
            scores.append(s)
            vals, x = [], s
            if c == 1:
                rank2 = jnp.full(s.shape, NOT_RANKED, F32)
            for k in range(PEER_TOPK):
                m = jnp.max(x, axis=0, keepdims=True)
                vals.append(m)
                found = x == m
                if c == 1:
                    rank2 = jnp.where(found, float(k), rank2)
                x = jnp.where(found, neg_inf, x)
            tops.append(vals)
        a, bv = tops
        b_packs = [_pack_rows(bv[k:k + SUBLANE]) for k in range(0, PEER_TOPK, SUBLANE)]
        sub = lax.broadcasted_iota(jnp.int32, b_packs[0].shape, 0)
        cands, owners, single = [], [], []
        for i in range(PEER_TOPK):
            partners = PEER_TOPK // (i + 1)
            if partners == 1:
                single.append(i)
                continue
            for k in range(0, partners, SUBLANE):
                cd = a[i] + b_packs[k // SUBLANE]
                if partners - k < SUBLANE:
                    cd = jnp.where(sub < partners - k, cd, neg_inf)
                cands.append(cd)
                owners.append([i] * SUBLANE)
        assert len(single) % SUBLANE == 0
        for k in range(0, len(single), SUBLANE):
            ids = single[k:k + SUBLANE]
            cands.append(_pack_rows([a[i] for i in ids]) + bv[0])
            owners.append(ids)
        originals = list(cands)
        v0 = a[0] + bv[0]
        z = jnp.zeros_like(v0)
        tau = v0
        for _ in range(PEER_TOPK):
            tau = jnp.max(functools.reduce(jnp.maximum, cands), axis=0, keepdims=True)
            z = z + jnp.exp(tau - v0)
            cands = [jnp.where(cd == tau, neg_inf, cd) for cd in cands]
        partners = [None] * PEER_TOPK
        for cd, ids in zip(originals, owners):
            reached = jnp.where(cd >= tau, 1.0, 0.0)
            if len(set(ids)) == 1:
                cnt = jnp.sum(reached, axis=0, keepdims=True)
                partners[ids[0]] = cnt if partners[ids[0]] is None else partners[ids[0]] + cnt
            else:
                for k, i in enumerate(ids):
                    partners[i] = jnp.sum(jnp.where(sub == k, reached, 0.0), axis=0, keepdims=True)
        lim = jnp.full(scores[0].shape, -1.0, F32)
        for i in range(PEER_TOPK):
            lim = jnp.where(scores[0] == a[i], partners[i] - 1.0, lim)
        hs = slice(h * PEER_N_KEYS, (h + 1) * PEER_N_KEYS)
        lim_ref[hs, :] = lim
        rank_ref[hs, :] = rank2.astype(rank_ref.dtype)
        e1_ref[hs, :] = jnp.exp(scores[0] - a[0])
        e2_ref[hs, :] = (jnp.exp(scores[1] - bv[0]) / z).astype(e2_ref.dtype)


def _peer_select(qt):
    rows, n = qt[0].shape
    tm = _pick(n, (256, 128))
    qt_arr, keys = qt
    hk = PEER_HEADS * PEER_N_KEYS
    big = lambda: pl.BlockSpec((hk, tm), lambda i: (0, i))
    return pl.pallas_call(
        _peer_select_kernel,
        grid=(n // tm,),
        in_specs=[pl.BlockSpec((rows, tm), lambda i: (0, i)),
                  pl.BlockSpec(keys.shape, lambda i: (0, 0))],
        out_specs=[big(), big(), big(), big()],
        out_shape=[jax.ShapeDtypeStruct((hk, n), F32), jax.ShapeDtypeStruct((hk, n), BF16),
                   jax.ShapeDtypeStruct((hk, n), F32), jax.ShapeDtypeStruct((hk, n), BF16)],
        compiler_params=_params(("parallel",), 40),
        name="peer_select",
    )(qt_arr, keys)


def _gelu_tanh(x):
    c = np.float32(np.sqrt(2.0 / np.pi))
    return x * (0.5 * (1.0 + jnp.tanh(c * (x + 0.044715 * (x * x * x)))))


def _peer_main_kernel(hn_ref, u_ref, vt_ref, lim_ref, rank_ref, e1_ref, e2_ref, o_ref,
                      raw_scr, a_scr, *, rows_per_tile, n_tiles):
    j = pl.program_id(1)
    last_slot = (n_tiles - 1) % 2

    def first_matmul(slot):
        raw_scr[slot] = jnp.dot(u_ref[...], hn_ref[...], preferred_element_type=F32)

    def gate_and_project(slot, init):
        tm = o_ref.shape[1]
        for rr in range(rows_per_tile):
            r = (j - 1) * rows_per_tile + rr
            lim_rows = [jnp.broadcast_to(lim_ref[pl.ds(h * PEER_N_KEYS + r, 1), :],
                                         (GATE_ROWS, tm)).astype(BF16) for h in range(PEER_HEADS)]
            e1_rows = [jnp.broadcast_to(e1_ref[pl.ds(h * PEER_N_KEYS + r, 1), :],
                                        (GATE_ROWS, tm)).astype(BF16) for h in range(PEER_HEADS)]
            for c0 in range(0, PEER_N_KEYS, GATE_ROWS):
                gate = None
                for h in range(PEER_HEADS):
                    cs = slice(h * PEER_N_KEYS + c0, h * PEER_N_KEYS + c0 + GATE_ROWS)
                    hit = rank_ref[cs, :] <= lim_rows[h]
                    term = jnp.where(hit, e2_ref[cs, :] * e1_rows[h], 0.0)
                    gate = term if gate is None else gate + term
                es = slice(rr * PEER_N_KEYS + c0, rr * PEER_N_KEYS + c0 + GATE_ROWS)
                act = _gelu_tanh(raw_scr[slot, es, :]) * gate.astype(F32)
                a_scr[es, :] = act.astype(BF16)
        contrib = jnp.dot(vt_ref[...], a_scr[...], preferred_element_type=F32)
        o_ref[...] = contrib if init else o_ref[...] + contrib

    @pl.when(j == 0)
    def _():
        first_matmul(0)

    @pl.when(j == 1)
    def _():
        first_matmul(1)
        gate_and_project(0, init=True)

    @pl.when((j > 1) & (j < n_tiles) & (j % 2 == 0))
    def _():
        first_matmul(0)
        gate_and_project(1, init=False)

    @pl.when((j > 1) & (j < n_tiles) & (j % 2 == 1))
    def _():
        first_matmul(1)
        gate_and_project(0, init=False)

    @pl.when(j == n_tiles)
    def _():
        gate_and_project(last_slot, init=False)


def _peer_main(hn_t, u, vt, lim, rank2, e1, e2):
    d, n = hn_t.shape
    n_exp = u.shape[0]
    tm = _pick(n, (512, 256, 128))
    te = 512
    hk = PEER_HEADS * PEER_N_KEYS
    tok = lambda rows: pl.BlockSpec((rows, tm), lambda i, j: (0, i), pipeline_mode=pl.Buffered(1))
    n_tiles = n_exp // te
    assert n_tiles >= 2
    return pl.pallas_call(
        functools.partial(_peer_main_kernel, rows_per_tile=te // PEER_N_KEYS, n_tiles=n_tiles),
        grid=(n // tm, n_tiles + 1),
        in_specs=[tok(d),
                  pl.BlockSpec((te, d), lambda i, j: (jnp.minimum(j, n_tiles - 1), 0)),
                  pl.BlockSpec((d, te), lambda i, j: (0, jnp.maximum(j - 1, 0))),
                  tok(hk), tok(hk), tok(hk), tok(hk)],
        out_specs=pl.BlockSpec((d, tm), lambda i, j: (0, i)),
        out_shape=jax.ShapeDtypeStruct((d, n), F32),
        scratch_shapes=[pltpu.VMEM((2, te, tm), F32), pltpu.VMEM((te, tm), BF16)],
        compiler_params=_params(("parallel", "arbitrary"), 58),
        name="peer_main",
    )(hn_t, u, vt, lim, rank2, e1, e2)


def _rmsnorm_t_kernel(x_ref, g_ref, o_ref):
    o_ref[...] = _rms(x_ref[...], g_ref[...]).T.astype(o_ref.dtype)


def _final_kernel(h_ref, p_ref, g_ref, o_ref):
    o_ref[...] = _rms(h_ref[...] + p_ref[...].T, g_ref[...])


def _in_proj_layout(d):
    segs = [("gate_a", d), ("gate_b", d), ("q_b", 2048), ("c_q", 1024), ("q_ix", 1024),
            ("c_kv", 512), ("k_b", 512), ("v_b", 512), ("k_pe", LANE), ("k_ix", LANE),
            ("w_ix", LANE)]
    segs.sort(key=lambda s: -s[1])
    offs, o = {}, 0
    for name, wd in segs:
        assert o % wd == 0
        offs[name] = (o, wd)
        o += wd
    total = -(-o // 1024) * 1024
    return offs, total


def _build_w_in(w_in, d):
    src = {}
    o = 0
    for name, wd in (("c_q", MLA_Q_LORA), ("c_kv", MLA_KV_LORA), ("k_pe", MLA_ROPE),
                     ("q_b", 2048), ("k_b", 512), ("v_b", 512), ("q_ix", 1024),
                     ("k_ix", IDX_DIM), ("w_ix", IDX_HEADS), ("gate_a", d), ("gate_b", d)):
        src[name] = w_in[:, o:o + wd]
        o += wd
    offs, total = _in_proj_layout(d)
    cols = []
    o = 0
    for name, (off, wd) in sorted(offs.items(), key=lambda kv: kv[1][0]):
        blk = src[name]
        cols.append(jnp.pad(blk, ((0, 0), (0, wd - blk.shape[1]))))
        o = off + wd
    cols.append(jnp.zeros((d, total - o), w_in.dtype))
    return jnp.concatenate(cols, axis=1).astype(BF16), offs


def kernel(x, meta_tokens, attn_norm_g, w_in, q_norm_g, w_uq, kv_norm_g, w_ukv, w_branch, w_out,
           ffn_norm_g, peer_w_q, peer_sub_keys, peer_u, peer_v, final_norm_g):
    b, s, d = x.shape
    assert attn_norm_g.shape[0] == 1, "single-layer block"
    assert s % Q_BLOCK == 0 and d % 512 == 0 and (d & (d - 1)) == 0
    assert N_META % SUBLANE == 0
    t = s + N_META
    t_pad = -(-t // Q_BLOCK) * Q_BLOCK
    topk = min(DSA_TOPK_MAX, s // 4)
    n = b * t_pad

    meta = jnp.broadcast_to(meta_tokens.astype(x.dtype)[None], (b, N_META, d))
    h0 = jnp.concatenate([meta, x, jnp.zeros((b, t_pad - t, d), x.dtype)], axis=1).reshape(n, d)

    w_in_p, offs = _build_w_in(w_in[0], d)
    wq = w_uq[0].reshape(MLA_Q_LORA, MLA_HEADS, MLA_NOPE + MLA_ROPE)
    wq = jnp.pad(wq, ((0, 0), (0, 0), (0, MLA_QK_PAD - MLA_NOPE - MLA_ROPE)))
    wq = wq.reshape(MLA_Q_LORA, MLA_HEADS * MLA_QK_PAD).astype(BF16)
    wkv = w_ukv[0].reshape(MLA_KV_LORA, MLA_HEADS, MLA_NOPE + MLA_V)
    wkv = jnp.concatenate([wkv[:, :, :MLA_NOPE].reshape(MLA_KV_LORA, -1),
                           wkv[:, :, MLA_NOPE:].reshape(MLA_KV_LORA, -1)], axis=1).astype(BF16)
    w_br = w_branch[0].astype(BF16)
    w_o = w_out[0].astype(BF16)
    w_pq_t = peer_w_q[0].T.astype(BF16)
    keys = peer_sub_keys[0].reshape(PEER_HEADS * 2 * PEER_N_KEYS, PEER_HALF).astype(BF16)
    u_tab = peer_u[0].astype(BF16)
    v_tab_t = peer_v[0].T.astype(BF16)

    mla_tabs = _rope_tables(t_pad, LANE, MLA_ROPE)
    dsa_tabs = _rope_tables(t_pad, LANE, DSA_ROT)
    idx_tabs = _rope_tables(t_pad, IDX_DIM, IDX_ROT)

    hn = _rmsnorm(h0, attn_norm_g[0], BF16)
    proj = _matmul(hn, w_in_p, F32, "in_proj")

    tm = _pick(t_pad, (384, 128))
    n_pos = t_pad // tm
    row = lambda wd, off: pl.BlockSpec((tm, wd), lambda i, o=off // wd: (i, o))
    seg = lambda name: row(offs[name][1], offs[name][0])
    tab = pl.BlockSpec((tm, LANE), lambda i: (i % n_pos, 0))
    full = lambda a: pl.BlockSpec(a.shape, lambda i: (0,) * a.ndim)
    out_row = lambda wd: pl.BlockSpec((tm, wd), lambda i: (i, 0))
    sds = lambda wd, dt: jax.ShapeDtypeStruct((n, wd), dt)

    qg = q_norm_g[0].reshape(1, -1).astype(F32)
    q_a = pl.pallas_call(
        _mla_q_kernel, grid=(n // tm,),
        in_specs=[seg("c_q"), full(qg), full(wq), tab, tab, tab],
        out_specs=out_row(MLA_HEADS * MLA_QK_PAD),
        out_shape=sds(MLA_HEADS * MLA_QK_PAD, BF16),
        compiler_params=_params(("parallel",), 48), name="mla_q_prep",
    )(proj, qg, wq, *mla_tabs)

    kg = kv_norm_g[0].reshape(1, -1).astype(F32)
    k_a, v_a = pl.pallas_call(
        _mla_kv_kernel, grid=(n // tm,),
        in_specs=[seg("c_kv"), full(kg), full(wkv), seg("k_pe"), tab, tab, tab],
        out_specs=[out_row(MLA_HEADS * MLA_QK_PAD), out_row(MLA_HEADS * 2 * MLA_V)],
        out_shape=[sds(MLA_HEADS * MLA_QK_PAD, BF16), sds(MLA_HEADS * 2 * MLA_V, BF16)],
        compiler_params=_params(("parallel",), 48), name="mla_kv_prep",
    )(proj, kg, wkv, proj, *mla_tabs)

    kvw = DSA_KV_HEADS * DSA_HEAD_DIM
    q_b, k_b, v_b, q_ix, k_lo, k_hi, w_ix = pl.pallas_call(
        _dsa_prep_kernel, grid=(n // tm,),
        in_specs=[seg("q_b"), seg("k_b"), seg("v_b"), seg("q_ix"), seg("k_ix"), seg("w_ix")]
        + [tab] * 6,
        out_specs=[out_row(2048), out_row(kvw), out_row(2 * kvw), out_row(1024),
                   out_row(LANE), out_row(LANE), out_row(LANE)],
        out_shape=[sds(2048, BF16), sds(kvw, BF16), sds(2 * kvw, BF16), sds(1024, BF16),
                   sds(LANE, BF16), sds(LANE, BF16), sds(LANE, F32)],
        compiler_params=_params(("parallel",), 48), name="dsa_prep",
    )(proj, proj, proj, proj, proj, proj, *dsa_tabs, *idx_tabs)

    o_a = _mla_attention(q_a, k_a, v_a, b, t_pad)
    o_b = _dsa_attention(q_b, k_b, v_b, q_ix, k_lo, k_hi, w_ix, b, t_pad, topk)

    tmm = _pick(n, (512, 256, 128))
    tn = _pick(d, (1024, 512))
    ga_off, gb_off = offs["gate_a"][0] // tn, offs["gate_b"][0] // tn
    bw = MLA_HEADS * MLA_V
    merged = pl.pallas_call(
        _branch_kernel, grid=(d // tn, n // tmm),
        in_specs=[pl.BlockSpec((tmm, bw), lambda j, i: (i, 0)),
                  pl.BlockSpec((tmm, bw), lambda j, i: (i, 0)),
                  pl.BlockSpec((bw, tn), lambda j, i: (0, j)),
                  pl.BlockSpec((bw, tn), lambda j, i: (1, j)),
                  pl.BlockSpec((tmm, tn), lambda j, i: (i, ga_off + j)),
                  pl.BlockSpec((tmm, tn), lambda j, i: (i, gb_off + j))],
        out_specs=pl.BlockSpec((tmm, tn), lambda j, i: (i, j)),
        out_shape=jax.ShapeDtypeStruct((n, d), BF16),
        compiler_params=_params(("parallel", "parallel"), 48), name="branch_merge",
    )(o_a, o_b, w_br, w_br, proj, proj)

    h1 = pl.pallas_call(
        _residual_mm_kernel, grid=(d // tn, n // tmm),
        in_specs=[pl.BlockSpec((tmm, d), lambda j, i: (i, 0)),
                  pl.BlockSpec((d, tn), lambda j, i: (0, j)),
                  pl.BlockSpec((tmm, tn), lambda j, i: (i, j))],
        out_specs=pl.BlockSpec((tmm, tn), lambda j, i: (i, j)),
        out_shape=jax.ShapeDtypeStruct((n, d), F32),
        compiler_params=_params(("parallel", "parallel"), 48), name="out_proj",
    )(merged, w_o, h0)

    np_tok = b * s
    tf = _pick(s, (256, 128))
    tiles_per_seq = s // tf
    real_rows = pl.BlockSpec((pl.Element(tf), pl.Element(d)),
                             lambda bb, t: (pl.multiple_of(bb * t_pad + N_META + t * tf, SUBLANE), 0))
    vec = pl.BlockSpec((1, d), lambda bb, t: (0, 0))

    hn2_t = pl.pallas_call(
        _rmsnorm_t_kernel, grid=(b, tiles_per_seq),
        in_specs=[real_rows, vec],
        out_specs=pl.BlockSpec((d, tf), lambda bb, t: (0, bb * tiles_per_seq + t)),
        out_shape=jax.ShapeDtypeStruct((d, np_tok), BF16),
        compiler_params=_params(("parallel", "parallel"), 40), name="ffn_norm_t",
    )(h1, ffn_norm_g[0].reshape(1, d).astype(F32))
    q_t = _matmul(w_pq_t, hn2_t, F32, "peer_query")
    lim, rank2, e1, e2 = _peer_select((q_t, keys))
    peer_t = _peer_main(hn2_t, u_tab, v_tab_t, lim, rank2, e1, e2)

    out = pl.pallas_call(
        _final_kernel, grid=(b, tiles_per_seq),
        in_specs=[real_rows,
                  pl.BlockSpec((d, tf), lambda bb, t: (0, bb * tiles_per_seq + t)), vec],
        out_specs=pl.BlockSpec((tf, d), lambda bb, t: (bb * tiles_per_seq + t, 0)),
        out_shape=jax.ShapeDtypeStruct((np_tok, d), F32),
        compiler_params=_params(("parallel", "parallel"), 40), name="final_norm",
    )(h1, peer_t, final_norm_g.reshape(1, d).astype(F32))
    return out.reshape(b, s, d)
```

```python
import functools

import jax
import jax.numpy as jnp
import numpy as np
from jax import lax
from jax.experimental import pallas as pl
from jax.experimental.pallas import tpu as pltpu

N_META = 16
ROPE_THETA = 500000.0
Q_BLOCK = 128
NORM_EPS = 1e-6
MLA_HEADS = 16
MLA_Q_LORA = 1024
MLA_KV_LORA = 512
MLA_NOPE = 128
MLA_ROPE = 64
MLA_V = 128
DSA_HEADS = 16
DSA_KV_HEADS = 4
DSA_HEAD_DIM = 128
DSA_ROT = DSA_HEAD_DIM // 4
IDX_HEADS = 16
IDX_DIM = 64
IDX_ROT = IDX_DIM // 4
DSA_TOPK_MAX = 256
PEER_HEADS = 8
PEER_N_KEYS = 128
PEER_HALF = 128
PEER_TOPK = 16

LANE = 128
SUBLANE = 8
MXU_WIDTH = 256
GATE_ROWS = 16
MLA_HEAD_GROUP = 4
LOG2_E = 1.4426950408889634
SCORE_DTYPE = jnp.bfloat16
NOT_RANKED = 127.0
MLA_QK_PAD = 2 * LANE
MASK_NEG = -1e30
INT32_MIN = -(2 ** 31)
HALF_RANGE = 2 ** 15
MIB = 1024 * 1024

BF16 = jnp.bfloat16
F32 = jnp.float32


def _pick(n, prefs):
    for p in prefs:
        if n % p == 0:
            return p
    raise ValueError(f"no tile in {prefs} divides {n}")


def _params(sem, vmem_mib, flags=None):
    return pltpu.CompilerParams(dimension_semantics=sem, vmem_limit_bytes=vmem_mib * MIB,
                                flags=flags)


def _nt_dot(a, b, out_dtype=F32):
    out = lax.dot_general(a, b, (((1,), (1,)), ((), ())), preferred_element_type=F32)
    return out.astype(out_dtype)


def _rms(x, g):
    ms = jnp.mean(x * x, axis=-1, keepdims=True)
    return x * lax.rsqrt(ms + NORM_EPS) * g


def _rope(x, c, s_up, s_dn, half):
    return x * c + pltpu.roll(x, half, 1) * s_up + pltpu.roll(x, LANE - half, 1) * s_dn


def _rope_tables(t_pad, block, rot):
    half = rot // 2
    inv = ROPE_THETA ** (-jnp.arange(0, rot, 2, dtype=F32) / rot)
    ang = jnp.arange(t_pad, dtype=F32)[:, None] * inv[None, :]
    cos, sin = jnp.cos(ang), jnp.sin(ang)
    lane = np.arange(LANE) % block
    lo = lane < half
    hi = (lane >= half) & (lane < rot)
    idx = np.where(lo, lane, np.where(hi, lane - half, 0))
    cos_l, sin_l = cos[:, idx], sin[:, idx]
    c = jnp.where(lo | hi, cos_l, 1.0)
    s_up = jnp.where(hi, sin_l, 0.0)
    s_dn = jnp.where(lo, -sin_l, 0.0)
    return c.astype(F32), s_up.astype(F32), s_dn.astype(F32)


def _rmsnorm_kernel(x_ref, g_ref, o_ref):
    o_ref[...] = _rms(x_ref[...].astype(F32), g_ref[...]).astype(o_ref.dtype)


def _rmsnorm(x, g, out_dtype):
    m, d = x.shape
    tm = _pick(m, (256, 128))
    return pl.pallas_call(
        _rmsnorm_kernel,
        grid=(m // tm,),
        in_specs=[pl.BlockSpec((tm, d), lambda i: (i, 0)),
                  pl.BlockSpec((1, d), lambda i: (0, 0))],
        out_specs=pl.BlockSpec((tm, d), lambda i: (i, 0)),
        out_shape=jax.ShapeDtypeStruct((m, d), out_dtype),
        compiler_params=_params(("parallel",), 40),
        name="rmsnorm",
    )(x, g.reshape(1, d).astype(F32))


def _mm_kernel(a_ref, b_ref, o_ref):
    o_ref[...] = jnp.dot(a_ref[...], b_ref[...], preferred_element_type=F32).astype(o_ref.dtype)


def _matmul(a, b, out_dtype, name):
    m, k = a.shape
    _, n = b.shape
    tm = _pick(m, (512, 256, 128))
    tn = _pick(n, (1024, 512, 256, 128))
    return pl.pallas_call(
        _mm_kernel,
        grid=(n // tn, m // tm),
        in_specs=[pl.BlockSpec((tm, k), lambda j, i: (i, 0)),
                  pl.BlockSpec((k, tn), lambda j, i: (0, j))],
        out_specs=pl.BlockSpec((tm, tn), lambda j, i: (i, j)),
        out_shape=jax.ShapeDtypeStruct((m, n), out_dtype),
        compiler_params=_params(("parallel", "parallel"), 48),
        name=name,
    )(a, b)


def _mla_q_kernel(cq_ref, g_ref, w_ref, c_ref, su_ref, sd_ref, o_ref):
    xn = _rms(cq_ref[...], g_ref[...]).astype(BF16)
    q = jnp.dot(xn, w_ref[...], preferred_element_type=F32)
    q = q * ((MLA_NOPE + MLA_ROPE) ** -0.5 * LOG2_E)
    c, su, sd = c_ref[...], su_ref[...], sd_ref[...]
    for h in range(MLA_HEADS):
        base = h * MLA_QK_PAD
        o_ref[:, base:base + LANE] = q[:, base:base + LANE].astype(BF16)
        o_ref[:, base + LANE:base + 2 * LANE] = _rope(
            q[:, base + LANE:base + 2 * LANE], c, su, sd, MLA_ROPE // 2).astype(BF16)


def _mla_kv_kernel(ckv_ref, g_ref, w_ref, kpe_ref, c_ref, su_ref, sd_ref, k_ref, v_ref):
    xn = _rms(ckv_ref[...], g_ref[...]).astype(BF16)
    kv = jnp.dot(xn, w_ref[...], preferred_element_type=F32)
    kpe = _rope(kpe_ref[...], c_ref[...], su_ref[...], sd_ref[...], MLA_ROPE // 2).astype(BF16)
    nk = MLA_HEADS * MLA_NOPE
    for h in range(MLA_HEADS):
        base = h * MLA_QK_PAD
        k_ref[:, base:base + LANE] = kv[:, h * MLA_NOPE:(h + 1) * MLA_NOPE].astype(BF16)
        k_ref[:, base + LANE:base + 2 * LANE] = kpe
        v_ref[:, base:base + MLA_V] = kv[:, nk + h * MLA_V:nk + (h + 1) * MLA_V].astype(BF16)
        v_ref[:, base + MLA_V:base + 2 * MLA_V] = jnp.ones((kv.shape[0], MLA_V), BF16)


def _dsa_prep_kernel(qb_ref, kb_ref, vb_ref, qix_ref, kix_ref, wix_ref,
                     dc_ref, dsu_ref, dsd_ref, ic_ref, isu_ref, isd_ref,
                     qb_o, kb_o, vb_o, qix_o, klo_o, khi_o, w_o):
    dc, dsu, dsd = dc_ref[...], dsu_ref[...], dsd_ref[...]
    ic, isu, isd = ic_ref[...], isu_ref[...], isd_ref[...]
    for h in range(DSA_HEADS):
        sl = slice(h * LANE, (h + 1) * LANE)
        qb = _rope(qb_ref[:, sl], dc, dsu, dsd, DSA_ROT // 2)
        qb_o[:, sl] = (qb * (DSA_HEAD_DIM ** -0.5 * LOG2_E)).astype(BF16)
    for h in range(DSA_KV_HEADS):
        sl = slice(h * LANE, (h + 1) * LANE)
        kb_o[:, sl] = _rope(kb_ref[:, sl], dc, dsu, dsd, DSA_ROT // 2).astype(BF16)
    for h in range(DSA_KV_HEADS):
        vb_o[:, 2 * h * LANE:(2 * h + 1) * LANE] = vb_ref[:, h * LANE:(h + 1) * LANE].astype(BF16)
        vb_o[:, (2 * h + 1) * LANE:(2 * h + 2) * LANE] = jnp.ones((vb_ref.shape[0], LANE), BF16)
    for p in range(IDX_HEADS * IDX_DIM // LANE):
        sl = slice(p * LANE, (p + 1) * LANE)
        qix_o[:, sl] = _rope(qix_ref[:, sl], ic, isu, isd, IDX_ROT // 2).astype(BF16)
    kix = _rope(kix_ref[...], ic, isu, isd, IDX_ROT // 2)
    klo_o[...] = kix.astype(BF16)
    khi_o[...] = pltpu.roll(kix, IDX_DIM, 1).astype(BF16)
    w_o[...] = wix_ref[...] * (IDX_HEADS ** -0.5 * IDX_DIM ** -0.5)


def _softmax_steps(scores, values, carries):
    stats = []
    for s, (m, _, _) in zip(scores, carries):
        m_new = jnp.maximum(m, jnp.max(s, axis=1, keepdims=True).astype(F32))
        alpha = jnp.exp2(m - m_new)
        p = jnp.exp2(s - m_new.astype(s.dtype))
        stats.append((m_new, alpha, p.astype(BF16)))
    out = []
    for (m_new, alpha, p), v, (_, l, acc) in zip(stats, values, carries):
        pv = jnp.dot(p, v, preferred_element_type=F32)
        width = acc.shape[1]
        out.append((m_new, alpha * l + pv[:, width:width + 1], alpha * acc + pv[:, :width]))
    return tuple(out)


def _softmax_init(rows, width, chains):
    return tuple((jnp.full((rows, 1), MASK_NEG, F32), jnp.zeros((rows, 1), F32),
                  jnp.zeros((rows, width), F32)) for _ in range(chains))


def _mla_attn_kernel(q_ref, k_ref, v_ref, o_ref, *, tq, heads):
    i = pl.program_id(2)
    qs = [q_ref[:, h * MLA_QK_PAD:(h + 1) * MLA_QK_PAD] for h in range(heads)]

    def chunk(j, carries, diagonal):
        off = pl.multiple_of(j * tq, tq)
        scores = [_nt_dot(qs[h], k_ref[pl.ds(off, tq), h * MLA_QK_PAD:(h + 1) * MLA_QK_PAD],
                          SCORE_DTYPE) for h in range(heads)]
        if diagonal:
            row = lax.broadcasted_iota(jnp.int32, (tq, tq), 0)
            col = lax.broadcasted_iota(jnp.int32, (tq, tq), 1)
            scores = [jnp.where(col <= row, s, MASK_NEG) for s in scores]
        values = [v_ref[pl.ds(off, tq), h * 2 * MLA_V:(h + 1) * 2 * MLA_V] for h in range(heads)]
        return _softmax_steps(scores, values, carries)

    carries = lax.fori_loop(0, i, lambda j, c: chunk(j, c, False),
                            _softmax_init(tq, MLA_V, heads))
    carries = chunk(i, carries, True)
    for h, (_, l, acc) in enumerate(carries):
        o_ref[:, h * MLA_V:(h + 1) * MLA_V] = (acc / l).astype(o_ref.dtype)


def _mla_attention(q, k, v, b, t_pad):
    tq = _pick(t_pad, (384, 128))
    hg = MLA_HEAD_GROUP
    q3 = q.reshape(b, t_pad, MLA_HEADS * MLA_QK_PAD)
    k3 = k.reshape(b, t_pad, MLA_HEADS * MLA_QK_PAD)
    v3 = v.reshape(b, t_pad, MLA_HEADS * 2 * MLA_V)
    out = pl.pallas_call(
        functools.partial(_mla_attn_kernel, tq=tq, heads=hg),
        grid=(b, MLA_HEADS // hg, t_pad // tq),
        in_specs=[pl.BlockSpec((None, tq, hg * MLA_QK_PAD), lambda bb, h, i: (bb, i, h)),
                  pl.BlockSpec((None, t_pad, hg * MLA_QK_PAD), lambda bb, h, i: (bb, 0, h)),
                  pl.BlockSpec((None, t_pad, hg * 2 * MLA_V), lambda bb, h, i: (bb, 0, h))],
        out_specs=pl.BlockSpec((None, tq, hg * MLA_V), lambda bb, h, i: (bb, i, h)),
        out_shape=jax.ShapeDtypeStruct((b, t_pad, MLA_HEADS * MLA_V), BF16),
        compiler_params=_params(("parallel", "parallel", "arbitrary"), 48),
        name="mla_attention",
    )(q3, k3, v3)
    return out.reshape(b * t_pad, MLA_HEADS * MLA_V)


def _dsa_kernel(qb_ref, kb_ref, vb_ref, qix_ref, klo_ref, khi_ref, w_ref, o_ref,
                key_scr, bias_scr, hi_scr, lo_scr, *, tq, tk, topk):
    i = pl.program_id(1)
    n_chunks = ((i + 1) * tq + tk - 1) // tk
    row = i * tq + lax.broadcasted_iota(jnp.int32, (tq, tk), 0)
    col0 = lax.broadcasted_iota(jnp.int32, (tq, tk), 1)
    w = w_ref[...]
    qix = qix_ref[...]

    def score_chunk(c, carry):
        off = pl.multiple_of(c * tk, tk)
        klo = klo_ref[pl.ds(off, tk), :]
        khi = khi_ref[pl.ds(off, tk), :]
        sc = jnp.zeros((tq, tk), F32)
        for h in range(IDX_HEADS):
            qp = qix[:, (h // 2) * LANE:(h // 2 + 1) * LANE]
            d = _nt_dot(qp, klo if h % 2 == 0 else khi)
            sc = sc + w[:, h:h + 1] * jnp.maximum(d, 0.0)
        sc = jnp.where(off + col0 <= row, sc, -jnp.inf)
        bits = pltpu.bitcast(sc, jnp.int32)
        key = bits ^ ((bits >> 31) & jnp.int32(0x7FFFFFFF))
        key_scr[:, pl.ds(off, tk)] = key
        hi_scr[:, pl.ds(off, tk)] = (key >> 16).astype(jnp.int16)
        lo_scr[:, pl.ds(off, tk)] = ((key & jnp.int32(0xFFFF)) - HALF_RANGE).astype(jnp.int16)
        return carry

    lax.fori_loop(0, n_chunks, score_chunk, 0)

    def count_lanes(pred_chunk):
        def count_chunk(c, acc):
            off = pl.multiple_of(c * tk, tk)
            hit = jnp.where(pred_chunk(off), jnp.int16(1), jnp.int16(0))
            for p in range(tk // LANE):
                acc = acc + hit[:, p * LANE:(p + 1) * LANE]
            return acc

        acc = lax.fori_loop(0, n_chunks, count_chunk, jnp.zeros((tq, LANE), jnp.int16))
        return jnp.sum(acc.astype(F32), axis=1, keepdims=True)

    def kth_largest_int16(scr, need):
        def bit_step(bi, t):
            cand = t + (jnp.int32(1) << (15 - bi))
            cand16 = cand.astype(jnp.int16)
            cnt = count_lanes(lambda off: scr[:, pl.ds(off, tk)] >= cand16)
            return jnp.where(cnt >= need, cand, t)

        return lax.fori_loop(0, 16, bit_step, jnp.full((tq, 1), -HALF_RANGE, jnp.int32))

    t_hi = kth_largest_int16(hi_scr, float(topk))
    t_hi16 = t_hi.astype(jnp.int16)
    above = count_lanes(lambda off: hi_scr[:, pl.ds(off, tk)] > t_hi16)

    def keep_tied(c, carry):
        off = pl.multiple_of(c * tk, tk)
        tied = hi_scr[:, pl.ds(off, tk)] == t_hi16
        lo_scr[:, pl.ds(off, tk)] = jnp.where(tied, lo_scr[:, pl.ds(off, tk)],
                                              jnp.int16(-HALF_RANGE))
        return carry

    lax.fori_loop(0, n_chunks, keep_tied, 0)
    t_lo = kth_largest_int16(lo_scr, float(topk) - above)
    thr = (t_hi << 16) + (t_lo + HALF_RANGE)

    def bias_chunk(c, carry):
        off = pl.multiple_of(c * tk, tk)
        ok = (key_scr[:, pl.ds(off, tk)] >= thr) & (off + col0 <= row)
        bias_scr[:, pl.ds(off, tk)] = jnp.where(ok, 0.0, MASK_NEG)
        return carry

    lax.fori_loop(0, n_chunks, bias_chunk, 0)

    rep = DSA_HEADS // DSA_KV_HEADS
    groups = [slice(g * LANE, (g + 1) * LANE) for g in range(DSA_KV_HEADS)]
    qgs = [jnp.concatenate(
        [qb_ref[:, (g * rep + j) * LANE:(g * rep + j + 1) * LANE] for j in range(rep)], axis=0)
        for g in range(DSA_KV_HEADS)]

    def att_chunk(c, carries):
        off = pl.multiple_of(c * tk, tk)
        bias = bias_scr[:, pl.ds(off, tk)].astype(SCORE_DTYPE)
        bias = jnp.concatenate([bias] * rep, axis=0)
        scores = [_nt_dot(qg, kb_ref[pl.ds(off, tk), gsl], SCORE_DTYPE) + bias
                  for qg, gsl in zip(qgs, groups)]
        values = [vb_ref[pl.ds(off, tk), 2 * g * LANE:(2 * g + 2) * LANE]
                  for g in range(DSA_KV_HEADS)]
        return _softmax_steps(scores, values, carries)

    carries = lax.fori_loop(0, n_chunks, att_chunk,
                            _softmax_init(rep * tq, DSA_HEAD_DIM, DSA_KV_HEADS))
    for g, (_, l, acc) in enumerate(carries):
        o = acc / l
        for j in range(rep):
            hq = g * rep + j
            o_ref[:, hq * LANE:(hq + 1) * LANE] = o[j * tq:(j + 1) * tq].astype(o_ref.dtype)


def _dsa_attention(qb, kb, vb, qix, klo, khi, w, b, t_pad, topk):
    tq = Q_BLOCK
    tk = _pick(t_pad, (384, 128))
    kvw = DSA_KV_HEADS * DSA_HEAD_DIM
    r3 = lambda a: a.reshape(b, t_pad, a.shape[-1])
    qspec = lambda wd: pl.BlockSpec((None, tq, wd), lambda bb, i: (bb, i, 0))
    kspec = lambda wd: pl.BlockSpec((None, t_pad, wd), lambda bb, i: (bb, 0, 0))
    out = pl.pallas_call(
        functools.partial(_dsa_kernel, tq=tq, tk=tk, topk=topk),
        grid=(b, t_pad // tq),
        in_specs=[qspec(DSA_HEADS * DSA_HEAD_DIM), kspec(kvw), kspec(2 * kvw),
                  qspec(IDX_HEADS * IDX_DIM), kspec(LANE), kspec(LANE), qspec(LANE)],
        out_specs=qspec(DSA_HEADS * DSA_HEAD_DIM),
        out_shape=jax.ShapeDtypeStruct((b, t_pad, DSA_HEADS * DSA_HEAD_DIM), BF16),
        scratch_shapes=[pltpu.VMEM((tq, t_pad), jnp.int32), pltpu.VMEM((tq, t_pad), F32),
                        pltpu.VMEM((tq, t_pad), jnp.int16), pltpu.VMEM((tq, t_pad), jnp.int16)],
        compiler_params=_params(("parallel", "arbitrary"), 48),
        name="dsa_attention",
    )(r3(qb), r3(kb), r3(vb), r3(qix), r3(klo), r3(khi), r3(w))
    return out.reshape(b * t_pad, DSA_HEADS * DSA_HEAD_DIM)


def _sigmoid(x):
    return 1.0 / (1.0 + jnp.exp(-x))


def _branch_kernel(oa_ref, ob_ref, wa_ref, wb_ref, ga_ref, gb_ref, o_ref):
    ya = jnp.dot(oa_ref[...], wa_ref[...], preferred_element_type=F32)
    yb = jnp.dot(ob_ref[...], wb_ref[...], preferred_element_type=F32)
    o_ref[...] = (_sigmoid(ga_ref[...]) * ya + _sigmoid(gb_ref[...]) * yb).astype(o_ref.dtype)


def _residual_mm_kernel(a_ref, b_ref, h_ref, o_ref):
    o_ref[...] = h_ref[...] + jnp.dot(a_ref[...], b_ref[...], preferred_element_type=F32)


def _pack_rows(rows):
    assert len(rows) == SUBLANE
    shape = (SUBLANE, rows[0].shape[1])
    sub = lax.broadcasted_iota(jnp.int32, shape, 0)
    out = jnp.broadcast_to(rows[-1], shape)
    for k in range(SUBLANE - 2, -1, -1):
        out = jnp.where(sub == k, rows[k], out)
    return out


def _peer_select_kernel(qt_ref, keys_ref, lim_ref, rank_ref, e1_ref, e2_ref):
    neg_inf = -jnp.inf
    for h in range(PEER_HEADS):
        scores, tops = [], []
        rank2 = None
        for c in range(2):
            sl = slice((2 * h + c) * PEER_HALF, (2 * h + c + 1) * PEER_HALF)
            s = jnp.dot(keys_ref[sl, :], qt_ref[sl, :].astype(BF16), preferred_element_type=F32)
            scores.append(s)
            vals, x = [], s
            if c == 1:
                rank2 = jnp.full(s.shape, NOT_RANKED, F32)
            for k in range(PEER_TOPK):
                m = jnp.max(x, axis=0, keepdims=True)
                vals.append(m)
                found = x == m
                if c == 1:
                    rank2 = jnp.where(found, float(k), rank2)
                x = jnp.where(found, neg_inf, x)
            tops.append(vals)
        a, bv = tops
        b_packs = [_pack_rows(bv[k:k + SUBLANE]) for k in range(0, PEER_TOPK, SUBLANE)]
        sub = lax.broadcasted_iota(jnp.int32, b_packs[0].shape, 0)
        cands, owners, single = [], [], []
        for i in range(PEER_TOPK):
            partners = PEER_TOPK // (i + 1)
            if partners == 1:
                single.append(i)
                continue
            for k in range(0, partners, SUBLANE):
                cd = a[i] + b_packs[k // SUBLANE]
                if partners - k < SUBLANE:
                    cd = jnp.where(sub < partners - k, cd, neg_inf)
                cands.append(cd)
                owners.append([i] * SUBLANE)
        assert len(single) % SUBLANE == 0
        for k in range(0, len(single), SUBLANE):
            ids = single[k:k + SUBLANE]
            cands.append(_pack_rows([a[i] for i in ids]) + bv[0])
            owners.append(ids)
        originals = list(cands)
        v0 = a[0] + bv[0]
        z = jnp.zeros_like(v0)
        tau = v0
        for _ in range(PEER_TOPK):
            tau = jnp.max(functools.reduce(jnp.maximum, cands), axis=0, keepdims=True)
            z = z + jnp.exp(tau - v0)
            cands = [jnp.where(cd == tau, neg_inf, cd) for cd in cands]
        partners = [None] * PEER_TOPK
        for cd, ids in zip(originals, owners):
            reached = jnp.where(cd >= tau, 1.0, 0.0)
            if len(set(ids)) == 1:
                cnt = jnp.sum(reached, axis=0, keepdims=True)
                partners[ids[0]] = cnt if partners[ids[0]] is None else partners[ids[0]] + cnt
            else:
                for k, i in enumerate(ids):
                    partners[i] = jnp.sum(jnp.where(sub == k, reached, 0.0), axis=0, keepdims=True)
        lim = jnp.full(scores[0].shape, -1.0, F32)
        for i in range(PEER_TOPK):
            lim = jnp.where(scores[0] == a[i], partners[i] - 1.0, lim)
        hs = slice(h * PEER_N_KEYS, (h + 1) * PEER_N_KEYS)
        lim_ref[hs, :] = lim
        rank_ref[hs, :] = rank2.astype(rank_ref.dtype)
        e1_ref[hs, :] = jnp.exp(scores[0] - a[0])
        e2_ref[hs, :] = (jnp.exp(scores[1] - bv[0]) / z).astype(e2_ref.dtype)


def _peer_select(qt):
    rows, n = qt[0].shape
    tm = _pick(n, (256, 128))
    qt_arr, keys = qt
    hk = PEER_HEADS * PEER_N_KEYS
    big = lambda: pl.BlockSpec((hk, tm), lambda i: (0, i))
    return pl.pallas_call(
        _peer_select_kernel,
        grid=(n // tm,),
        in_specs=[pl.BlockSpec((rows, tm), lambda i: (0, i)),
                  pl.BlockSpec(keys.shape, lambda i: (0, 0))],
        out_specs=[big(), big(), big(), big()],
        out_shape=[jax.ShapeDtypeStruct((hk, n), F32), jax.ShapeDtypeStruct((hk, n), BF16),
                   jax.ShapeDtypeStruct((hk, n), F32), jax.ShapeDtypeStruct((hk, n), BF16)],
        compiler_params=_params(("parallel",), 40),
        name="peer_select",
    )(qt_arr, keys)


def _gelu_tanh(x):
    c = np.float32(np.sqrt(2.0 / np.pi))
    return x * (0.5 * (1.0 + jnp.tanh(c * (x + 0.044715 * (x * x * x)))))


def _peer_main_kernel(hn_ref, u_ref, vt_ref, lim_ref, rank_ref, e1_ref, e2_ref, o_ref,
                      raw_scr, a_scr, *, rows_per_tile, n_tiles):
    j = pl.program_id(1)
    last_slot = (n_tiles - 1) % 2

    def first_matmul(slot):
        raw_scr[slot] = jnp.dot(u_ref[...], hn_ref[...], preferred_element_type=F32)

    def gate_and_project(slot, init):
        tm = o_ref.shape[1]
        for rr in range(rows_per_tile):
            r = (j - 1) * rows_per_tile + rr
            lim_rows = [jnp.broadcast_to(lim_ref[pl.ds(h * PEER_N_KEYS + r, 1), :],
                                         (GATE_ROWS, tm)).astype(BF16) for h in range(PEER_HEADS)]
            e1_rows = [jnp.broadcast_to(e1_ref[pl.ds(h * PEER_N_KEYS + r, 1), :],
                                        (GATE_ROWS, tm)).astype(BF16) for h in range(PEER_HEADS)]
            for c0 in range(0, PEER_N_KEYS, GATE_ROWS):
                gate = None
                for h in range(PEER_HEADS):
                    cs = slice(h * PEER_N_KEYS + c0, h * PEER_N_KEYS + c0 + GATE_ROWS)
                    hit = rank_ref[cs, :] <= lim_rows[h]
                    term = jnp.where(hit, e2_ref[cs, :] * e1_rows[h], 0.0)
                    gate = term if gate is None else gate + term
                es = slice(rr * PEER_N_KEYS + c0, rr * PEER_N_KEYS + c0 + GATE_ROWS)
                act = _gelu_tanh(raw_scr[slot, es, :]) * gate.astype(F32)
                a_scr[es, :] = act.astype(BF16)
        contrib = jnp.dot(vt_ref[...], a_scr[...], preferred_element_type=F32)
        o_ref[...] = contrib if init else o_ref[...] + contrib

    @pl.when(j == 0)
    def _():
        first_matmul(0)

    @pl.when(j == 1)
    def _():
        first_matmul(1)
        gate_and_project(0, init=True)

    @pl.when((j > 1) & (j < n_tiles) & (j % 2 == 0))
    def _():
        first_matmul(0)
        gate_and_project(1, init=False)

    @pl.when((j > 1) & (j < n_tiles) & (j % 2 == 1))
    def _():
        first_matmul(1)
        gate_and_project(0, init=False)

    @pl.when(j == n_tiles)
    def _():
        gate_and_project(last_slot, init=False)


def _peer_main(hn_t, u, vt, lim, rank2, e1, e2):
    d, n = hn_t.shape
    n_exp = u.shape[0]
    tm = _pick(n, (512, 256, 128))
    te = 512
    hk = PEER_HEADS * PEER_N_KEYS
    tok = lambda rows: pl.BlockSpec((rows, tm), lambda i, j: (0, i), pipeline_mode=pl.Buffered(1))
    n_tiles = n_exp // te
    assert n_tiles >= 2
    return pl.pallas_call(
        functools.partial(_peer_main_kernel, rows_per_tile=te // PEER_N_KEYS, n_tiles=n_tiles),
        grid=(n // tm, n_tiles + 1),
        in_specs=[tok(d),
                  pl.BlockSpec((te, d), lambda i, j: (jnp.minimum(j, n_tiles - 1), 0)),
                  pl.BlockSpec((d, te), lambda i, j: (0, jnp.maximum(j - 1, 0))),
                  tok(hk), tok(hk), tok(hk), tok(hk)],
        out_specs=pl.BlockSpec((d, tm), lambda i, j: (0, i)),
        out_shape=jax.ShapeDtypeStruct((d, n), F32),
        scratch_shapes=[pltpu.VMEM((2, te, tm), F32), pltpu.VMEM((te, tm), BF16)],
        compiler_params=_params(("parallel", "arbitrary"), 58),
        name="peer_main",
    )(hn_t, u, vt, lim, rank2, e1, e2)


def _rmsnorm_t_kernel(x_ref, g_ref, o_ref):
    o_ref[...] = _rms(x_ref[...], g_ref[...]).T.astype(o_ref.dtype)


def _final_kernel(h_ref, p_ref, g_ref, o_ref):
    o_ref[...] = _rms(h_ref[...] + p_ref[...].T, g_ref[...])


def _in_proj_layout(d):
    segs = [("gate_a", d), ("gate_b", d), ("q_b", 2048), ("c_q", 1024), ("q_ix", 1024),
            ("c_kv", 512), ("k_b", 512), ("v_b", 512), ("k_pe", LANE), ("k_ix", LANE),
            ("w_ix", LANE)]
    segs.sort(key=lambda s: -s[1])
    offs, o = {}, 0
    for name, wd in segs:
        assert o % wd == 0
        offs[name] = (o, wd)
        o += wd
    total = -(-o // 1024) * 1024
    return offs, total


def _build_w_in(w_in, d):
    src = {}
    o = 0
    for name, wd in (("c_q", MLA_Q_LORA), ("c_kv", MLA_KV_LORA), ("k_pe", MLA_ROPE),
                     ("q_b", 2048), ("k_b", 512), ("v_b", 512), ("q_ix", 1024),
                     ("k_ix", IDX_DIM), ("w_ix", IDX_HEADS), ("gate_a", d), ("gate_b", d)):
        src[name] = w_in[:, o:o + wd]
        o += wd
    offs, total = _in_proj_layout(d)
    cols = []
    o = 0
    for name, (off, wd) in sorted(offs.items(), key=lambda kv: kv[1][0]):
        blk = src[name]
        cols.append(jnp.pad(blk, ((0, 0), (0, wd - blk.shape[1]))))
        o = off + wd
    cols.append(jnp.zeros((d, total - o), w_in.dtype))
    return jnp.concatenate(cols, axis=1).astype(BF16), offs


def kernel(x, meta_tokens, attn_norm_g, w_in, q_norm_g, w_uq, kv_norm_g, w_ukv, w_branch, w_out,
           ffn_norm_g, peer_w_q, peer_sub_keys, peer_u, peer_v, final_norm_g):
    b, s, d = x.shape
    assert attn_norm_g.shape[0] == 1, "single-layer block"
    assert s % Q_BLOCK == 0 and d % 512 == 0 and (d & (d - 1)) == 0
    assert N_META % SUBLANE == 0
    t = s + N_META
    t_pad = -(-t // Q_BLOCK) * Q_BLOCK
    topk = min(DSA_TOPK_MAX, s // 4)
    n = b * t_pad

    meta = jnp.broadcast_to(meta_tokens.astype(x.dtype)[None], (b, N_META, d))
    h0 = jnp.concatenate([meta, x, jnp.zeros((b, t_pad - t, d), x.dtype)], axis=1).reshape(n, d)

    w_in_p, offs = _build_w_in(w_in[0], d)
    wq = w_uq[0].reshape(MLA_Q_LORA, MLA_HEADS, MLA_NOPE + MLA_ROPE)
    wq = jnp.pad(wq, ((0, 0), (0, 0), (0, MLA_QK_PAD - MLA_NOPE - MLA_ROPE)))
    wq = wq.reshape(MLA_Q_LORA, MLA_HEADS * MLA_QK_PAD).astype(BF16)
    wkv = w_ukv[0].reshape(MLA_KV_LORA, MLA_HEADS, MLA_NOPE + MLA_V)
    wkv = jnp.concatenate([wkv[:, :, :MLA_NOPE].reshape(MLA_KV_LORA, -1),
                           wkv[:, :, MLA_NOPE:].reshape(MLA_KV_LORA, -1)], axis=1).astype(BF16)
    w_br = w_branch[0].astype(BF16)
    w_o = w_out[0].astype(BF16)
    w_pq_t = peer_w_q[0].T.astype(BF16)
    keys = peer_sub_keys[0].reshape(PEER_HEADS * 2 * PEER_N_KEYS, PEER_HALF).astype(BF16)
    u_tab = peer_u[0].astype(BF16)
    v_tab_t = peer_v[0].T.astype(BF16)

    mla_tabs = _rope_tables(t_pad, LANE, MLA_ROPE)
    dsa_tabs = _rope_tables(t_pad, LANE, DSA_ROT)
    idx_tabs = _rope_tables(t_pad, IDX_DIM, IDX_ROT)

    hn = _rmsnorm(h0, attn_norm_g[0], BF16)
    proj = _matmul(hn, w_in_p, F32, "in_proj")

    tm = _pick(t_pad, (384, 128))
    n_pos = t_pad // tm
    row = lambda wd, off: pl.BlockSpec((tm, wd), lambda i, o=off // wd: (i, o))
    seg = lambda name: row(offs[name][1], offs[name][0])
    tab = pl.BlockSpec((tm, LANE), lambda i: (i % n_pos, 0))
    full = lambda a: pl.BlockSpec(a.shape, lambda i: (0,) * a.ndim)
    out_row = lambda wd: pl.BlockSpec((tm, wd), lambda i: (i, 0))
    sds = lambda wd, dt: jax.ShapeDtypeStruct((n, wd), dt)

    qg = q_norm_g[0].reshape(1, -1).astype(F32)
    q_a = pl.pallas_call(
        _mla_q_kernel, grid=(n // tm,),
        in_specs=[seg("c_q"), full(qg), full(wq), tab, tab, tab],
        out_specs=out_row(MLA_HEADS * MLA_QK_PAD),
        out_shape=sds(MLA_HEADS * MLA_QK_PAD, BF16),
        compiler_params=_params(("parallel",), 48), name="mla_q_prep",
    )(proj, qg, wq, *mla_tabs)

    kg = kv_norm_g[0].reshape(1, -1).astype(F32)
    k_a, v_a = pl.pallas_call(
        _mla_kv_kernel, grid=(n // tm,),
        in_specs=[seg("c_kv"), full(kg), full(wkv), seg("k_pe"), tab, tab, tab],
        out_specs=[out_row(MLA_HEADS * MLA_QK_PAD), out_row(MLA_HEADS * 2 * MLA_V)],
        out_shape=[sds(MLA_HEADS * MLA_QK_PAD, BF16), sds(MLA_HEADS * 2 * MLA_V, BF16)],
        compiler_params=_params(("parallel",), 48), name="mla_kv_prep",
    )(proj, kg, wkv, proj, *mla_tabs)

    kvw = DSA_KV_HEADS * DSA_HEAD_DIM
    q_b, k_b, v_b, q_ix, k_lo, k_hi, w_ix = pl.pallas_call(
        _dsa_prep_kernel, grid=(n // tm,),
        in_specs=[seg("q_b"), seg("k_b"), seg("v_b"), seg("q_ix"), seg("k_ix"), seg("w_ix")]
        + [tab] * 6,
        out_specs=[out_row(2048), out_row(kvw), out_row(2 * kvw), out_row(1024),
                   out_row(LANE), out_row(LANE), out_row(LANE)],
        out_shape=[sds(2048, BF16), sds(kvw, BF16), sds(2 * kvw, BF16), sds(1024, BF16),
                   sds(LANE, BF16), sds(LANE, BF16), sds(LANE, F32)],
        compiler_params=_params(("parallel",), 48), name="dsa_prep",
    )(proj, proj, proj, proj, proj, proj, *dsa_tabs, *idx_tabs)

    o_a = _mla_attention(q_a, k_a, v_a, b, t_pad)
    o_b = _dsa_attention(q_b, k_b, v_b, q_ix, k_lo, k_hi, w_ix, b, t_pad, topk)

    tmm = _pick(n, (512, 256, 128))
    tn = _pick(d, (1024, 512))
    ga_off, gb_off = offs["gate_a"][0] // tn, offs["gate_b"][0] // tn
    bw = MLA_HEADS * MLA_V
    merged = pl.pallas_call(
        _branch_kernel, grid=(d // tn, n // tmm),
        in_specs=[pl.BlockSpec((tmm, bw), lambda j, i: (i, 0)),
                  pl.BlockSpec((tmm, bw), lambda j, i: (i, 0)),
                  pl.BlockSpec((bw, tn), lambda j, i: (0, j)),
                  pl.BlockSpec((bw, tn), lambda j, i: (1, j)),
                  pl.BlockSpec((tmm, tn), lambda j, i: (i, ga_off + j)),
                  pl.BlockSpec((tmm, tn), lambda j, i: (i, gb_off + j))],
        out_specs=pl.BlockSpec((tmm, tn), lambda j, i: (i, j)),
        out_shape=jax.ShapeDtypeStruct((n, d), BF16),
        compiler_params=_params(("parallel", "parallel"), 48), name="branch_merge",
    )(o_a, o_b, w_br, w_br, proj, proj)

    h1 = pl.pallas_call(
        _residual_mm_kernel, grid=(d // tn, n // tmm),
        in_specs=[pl.BlockSpec((tmm, d), lambda j, i: (i, 0)),
                  pl.BlockSpec((d, tn), lambda j, i: (0, j)),
                  pl.BlockSpec((tmm, tn), lambda j, i: (i, j))],
        out_specs=pl.BlockSpec((tmm, tn), lambda j, i: (i, j)),
        out_shape=jax.ShapeDtypeStruct((n, d), F32),
        compiler_params=_params(("parallel", "parallel"), 48), name="out_proj",
    )(merged, w_o, h0)

    np_tok = b * s
    tf = _pick(s, (256, 128))
    tiles_per_seq = s // tf
    real_rows = pl.BlockSpec((pl.Element(tf), pl.Element(d)),
                             lambda bb, t: (pl.multiple_of(bb * t_pad + N_META + t * tf, SUBLANE), 0))
    vec = pl.BlockSpec((1, d), lambda bb, t: (0, 0))

    hn2_t = pl.pallas_call(
        _rmsnorm_t_kernel, grid=(b, tiles_per_seq),
        in_specs=[real_rows, vec],
        out_specs=pl.BlockSpec((d, tf), lambda bb, t: (0, bb * tiles_per_seq + t)),
        out_shape=jax.ShapeDtypeStruct((d, np_tok), BF16),
        compiler_params=_params(("parallel", "parallel"), 40), name="ffn_norm_t",
    )(h1, ffn_norm_g[0].reshape(1, d).astype(F32))
    q_t = _matmul(w_pq_t, hn2_t, F32, "peer_query")
    lim, rank2, e1, e2 = _peer_select((q_t, keys))
    peer_t = _peer_main(hn2_t, u_tab, v_tab_t, lim, rank2, e1, e2)

    out = pl.pallas_call(
        _final_kernel, grid=(b, tiles_per_seq),
        in_specs=[real_rows,
                  pl.BlockSpec((d, tf), lambda bb, t: (0, bb * tiles_per_seq + t)), vec],
        out_specs=pl.BlockSpec((tf, d), lambda bb, t: (bb * tiles_per_seq + t, 0)),
        out_shape=jax.ShapeDtypeStruct((np_tok, d), F32),
        compiler_params=_params(("parallel", "parallel"), 40), name="final_norm",
    )(h1, peer_t, final_norm_g.reshape(1, d).astype(F32))
    return out.reshape(b, s, d)
```

```python
import functools

import jax
import jax.numpy as jnp
import numpy as np
from jax import lax
from jax.experimental import pallas as pl
from jax.experimental.pallas import tpu as pltpu

N_META = 16
ROPE_THETA = 500000.0
Q_BLOCK = 128
NORM_EPS = 1e-6
MLA_HEADS = 16
MLA_Q_LORA = 1024
MLA_KV_LORA = 512
MLA_NOPE = 128
MLA_ROPE = 64
MLA_V = 128
DSA_HEADS = 16
DSA_KV_HEADS = 4
DSA_HEAD_DIM = 128
DSA_ROT = DSA_HEAD_DIM // 4
IDX_HEADS = 16
IDX_DIM = 64
IDX_ROT = IDX_DIM // 4
DSA_TOPK_MAX = 256
PEER_HEADS = 8
PEER_N_KEYS = 128
PEER_HALF = 128
PEER_TOPK = 16

LANE = 128
SUBLANE = 8
MXU_WIDTH = 256
GATE_ROWS = 16
MLA_HEAD_GROUP = 4
LOG2_E = 1.4426950408889634
SCORE_DTYPE = jnp.bfloat16
NOT_RANKED = 127.0
MLA_QK_PAD = 2 * LANE
MASK_NEG = -1e30
INT32_MIN = -(2 ** 31)
MIB = 1024 * 1024

BF16 = jnp.bfloat16
F32 = jnp.float32


def _pick(n, prefs):
    for p in prefs:
        if n % p == 0:
            return p
    raise ValueError(f"no tile in {prefs} divides {n}")


def _params(sem, vmem_mib, flags=None):
    return pltpu.CompilerParams(dimension_semantics=sem, vmem_limit_bytes=vmem_mib * MIB,
                                flags=flags)


def _nt_dot(a, b, out_dtype=F32):
    out = lax.dot_general(a, b, (((1,), (1,)), ((), ())), preferred_element_type=F32)
    return out.astype(out_dtype)


def _rms(x, g):
    ms = jnp.mean(x * x, axis=-1, keepdims=True)
    return x * lax.rsqrt(ms + NORM_EPS) * g


def _rope(x, c, s_up, s_dn, half):
    return x * c + pltpu.roll(x, half, 1) * s_up + pltpu.roll(x, LANE - half, 1) * s_dn


def _rope_tables(t_pad, block, rot):
    half = rot // 2
    inv = ROPE_THETA ** (-jnp.arange(0, rot, 2, dtype=F32) / rot)
    ang = jnp.arange(t_pad, dtype=F32)[:, None] * inv[None, :]
    cos, sin = jnp.cos(ang), jnp.sin(ang)
    lane = np.arange(LANE) % block
    lo = lane < half
    hi = (lane >= half) & (lane < rot)
    idx = np.where(lo, lane, np.where(hi, lane - half, 0))
    cos_l, sin_l = cos[:, idx], sin[:, idx]
    c = jnp.where(lo | hi, cos_l, 1.0)
    s_up = jnp.where(hi, sin_l, 0.0)
    s_dn = jnp.where(lo, -sin_l, 0.0)
    return c.astype(F32), s_up.astype(F32), s_dn.astype(F32)


def _rmsnorm_kernel(x_ref, g_ref, o_ref):
    o_ref[...] = _rms(x_ref[...].astype(F32), g_ref[...]).astype(o_ref.dtype)


def _rmsnorm(x, g, out_dtype):
    m, d = x.shape
    tm = _pick(m, (256, 128))
    return pl.pallas_call(
        _rmsnorm_kernel,
        grid=(m // tm,),
        in_specs=[pl.BlockSpec((tm, d), lambda i: (i, 0)),
                  pl.BlockSpec((1, d), lambda i: (0, 0))],
        out_specs=pl.BlockSpec((tm, d), lambda i: (i, 0)),
        out_shape=jax.ShapeDtypeStruct((m, d), out_dtype),
        compiler_params=_params(("parallel",), 40),
        name="rmsnorm",
    )(x, g.reshape(1, d).astype(F32))


def _mm_kernel(a_ref, b_ref, o_ref):
    o_ref[...] = jnp.dot(a_ref[...], b_ref[...], preferred_element_type=F32).astype(o_ref.dtype)


def _matmul(a, b, out_dtype, name):
    m, k = a.shape
    _, n = b.shape
    tm = _pick(m, (512, 256, 128))
    tn = _pick(n, (1024, 512, 256, 128))
    return pl.pallas_call(
        _mm_kernel,
        grid=(n // tn, m // tm),
        in_specs=[pl.BlockSpec((tm, k), lambda j, i: (i, 0)),
                  pl.BlockSpec((k, tn), lambda j, i: (0, j))],
        out_specs=pl.BlockSpec((tm, tn), lambda j, i: (i, j)),
        out_shape=jax.ShapeDtypeStruct((m, n), out_dtype),
        compiler_params=_params(("parallel", "parallel"), 48),
        name=name,
    )(a, b)


def _mla_q_kernel(cq_ref, g_ref, w_ref, c_ref, su_ref, sd_ref, o_ref):
    xn = _rms(cq_ref[...], g_ref[...]).astype(BF16)
    q = jnp.dot(xn, w_ref[...], preferred_element_type=F32)
    q = q * ((MLA_NOPE + MLA_ROPE) ** -0.5 * LOG2_E)
    c, su, sd = c_ref[...], su_ref[...], sd_ref[...]
    for h in range(MLA_HEADS):
        base = h * MLA_QK_PAD
        o_ref[:, base:base + LANE] = q[:, base:base + LANE].astype(BF16)
        o_ref[:, base + LANE:base + 2 * LANE] = _rope(
            q[:, base + LANE:base + 2 * LANE], c, su, sd, MLA_ROPE // 2).astype(BF16)


def _mla_kv_kernel(ckv_ref, g_ref, w_ref, kpe_ref, c_ref, su_ref, sd_ref, k_ref, v_ref):
    xn = _rms(ckv_ref[...], g_ref[...]).astype(BF16)
    kv = jnp.dot(xn, w_ref[...], preferred_element_type=F32)
    kpe = _rope(kpe_ref[...], c_ref[...], su_ref[...], sd_ref[...], MLA_ROPE // 2).astype(BF16)
    nk = MLA_HEADS * MLA_NOPE
    for h in range(MLA_HEADS):
        base = h * MLA_QK_PAD
        k_ref[:, base:base + LANE] = kv[:, h * MLA_NOPE:(h + 1) * MLA_NOPE].astype(BF16)
        k_ref[:, base + LANE:base + 2 * LANE] = kpe
        v_ref[:, base:base + MLA_V] = kv[:, nk + h * MLA_V:nk + (h + 1) * MLA_V].astype(BF16)
        v_ref[:, base + MLA_V:base + 2 * MLA_V] = jnp.ones((kv.shape[0], MLA_V), BF16)


def _dsa_prep_kernel(qb_ref, kb_ref, vb_ref, qix_ref, kix_ref, wix_ref,
                     dc_ref, dsu_ref, dsd_ref, ic_ref, isu_ref, isd_ref,
                     qb_o, kb_o, vb_o, qix_o, klo_o, khi_o, w_o):
    dc, dsu, dsd = dc_ref[...], dsu_ref[...], dsd_ref[...]
    ic, isu, isd = ic_ref[...], isu_ref[...], isd_ref[...]
    for h in range(DSA_HEADS):
        sl = slice(h * LANE, (h + 1) * LANE)
        qb = _rope(qb_ref[:, sl], dc, dsu, dsd, DSA_ROT // 2)
        qb_o[:, sl] = (qb * (DSA_HEAD_DIM ** -0.5 * LOG2_E)).astype(BF16)
    for h in range(DSA_KV_HEADS):
        sl = slice(h * LANE, (h + 1) * LANE)
        kb_o[:, sl] = _rope(kb_ref[:, sl], dc, dsu, dsd, DSA_ROT // 2).astype(BF16)
    for h in range(DSA_KV_HEADS):
        vb_o[:, 2 * h * LANE:(2 * h + 1) * LANE] = vb_ref[:, h * LANE:(h + 1) * LANE].astype(BF16)
        vb_o[:, (2 * h + 1) * LANE:(2 * h + 2) * LANE] = jnp.ones((vb_ref.shape[0], LANE), BF16)
    for p in range(IDX_HEADS * IDX_DIM // LANE):
        sl = slice(p * LANE, (p + 1) * LANE)
        qix_o[:, sl] = _rope(qix_ref[:, sl], ic, isu, isd, IDX_ROT // 2).astype(BF16)
    kix = _rope(kix_ref[...], ic, isu, isd, IDX_ROT // 2)
    klo_o[...] = kix.astype(BF16)
    khi_o[...] = pltpu.roll(kix, IDX_DIM, 1).astype(BF16)
    w_o[...] = wix_ref[...] * (IDX_HEADS ** -0.5 * IDX_DIM ** -0.5)


def _softmax_steps(scores, values, carries):
    stats = []
    for s, (m, _, _) in zip(scores, carries):
        m_new = jnp.maximum(m, jnp.max(s, axis=1, keepdims=True).astype(F32))
        alpha = jnp.exp2(m - m_new)
        p = jnp.exp2(s - m_new.astype(s.dtype))
        stats.append((m_new, alpha, p.astype(BF16)))
    out = []
    for (m_new, alpha, p), v, (_, l, acc) in zip(stats, values, carries):
        pv = jnp.dot(p, v, preferred_element_type=F32)
        width = acc.shape[1]
        out.append((m_new, alpha * l + pv[:, width:width + 1], alpha * acc + pv[:, :width]))
    return tuple(out)


def _softmax_init(rows, width, chains):
    return tuple((jnp.full((rows, 1), MASK_NEG, F32), jnp.zeros((rows, 1), F32),
                  jnp.zeros((rows, width), F32)) for _ in range(chains))


def _mla_attn_kernel(q_ref, k_ref, v_ref, o_ref, *, tq, heads):
    i = pl.program_id(2)
    qs = [q_ref[:, h * MLA_QK_PAD:(h + 1) * MLA_QK_PAD] for h in range(heads)]

    def chunk(j, carries, diagonal):
        off = pl.multiple_of(j * tq, tq)
        scores = [_nt_dot(qs[h], k_ref[pl.ds(off, tq), h * MLA_QK_PAD:(h + 1) * MLA_QK_PAD],
                          SCORE_DTYPE) for h in range(heads)]
        if diagonal:
            row = lax.broadcasted_iota(jnp.int32, (tq, tq), 0)
            col = lax.broadcasted_iota(jnp.int32, (tq, tq), 1)
            scores = [jnp.where(col <= row, s, MASK_NEG) for s in scores]
        values = [v_ref[pl.ds(off, tq), h * 2 * MLA_V:(h + 1) * 2 * MLA_V] for h in range(heads)]
        return _softmax_steps(scores, values, carries)

    carries = lax.fori_loop(0, i, lambda j, c: chunk(j, c, False),
                            _softmax_init(tq, MLA_V, heads))
    carries = chunk(i, carries, True)
    for h, (_, l, acc) in enumerate(carries):
        o_ref[:, h * MLA_V:(h + 1) * MLA_V] = (acc / l).astype(o_ref.dtype)


def _mla_attention(q, k, v, b, t_pad):
    tq = _pick(t_pad, (384, 128))
    hg = MLA_HEAD_GROUP
    q3 = q.reshape(b, t_pad, MLA_HEADS * MLA_QK_PAD)
    k3 = k.reshape(b, t_pad, MLA_HEADS * MLA_QK_PAD)
    v3 = v.reshape(b, t_pad, MLA_HEADS * 2 * MLA_V)
    out = pl.pallas_call(
        functools.partial(_mla_attn_kernel, tq=tq, heads=hg),
        grid=(b, MLA_HEADS // hg, t_pad // tq),
        in_specs=[pl.BlockSpec((None, tq, hg * MLA_QK_PAD), lambda bb, h, i: (bb, i, h)),
                  pl.BlockSpec((None, t_pad, hg * MLA_QK_PAD), lambda bb, h, i: (bb, 0, h)),
                  pl.BlockSpec((None, t_pad, hg * 2 * MLA_V), lambda bb, h, i: (bb, 0, h))],
        out_specs=pl.BlockSpec((None, tq, hg * MLA_V), lambda bb, h, i: (bb, i, h)),
        out_shape=jax.ShapeDtypeStruct((b, t_pad, MLA_HEADS * MLA_V), BF16),
        compiler_params=_params(("parallel", "parallel", "arbitrary"), 48),
        name="mla_attention",
    )(q3, k3, v3)
    return out.reshape(b * t_pad, MLA_HEADS * MLA_V)


def _dsa_kernel(qb_ref, kb_ref, vb_ref, qix_ref, klo_ref, khi_ref, w_ref, o_ref,
                key_scr, bias_scr, *, tq, tk, topk):
    i = pl.program_id(1)
    n_chunks = ((i + 1) * tq + tk - 1) // tk
    row = i * tq + lax.broadcasted_iota(jnp.int32, (tq, tk), 0)
    col0 = lax.broadcasted_iota(jnp.int32, (tq, tk), 1)
    w = w_ref[...]
    qix = qix_ref[...]

    def score_chunk(c, carry):
        off = pl.multiple_of(c * tk, tk)
        klo = klo_ref[pl.ds(off, tk), :]
        khi = khi_ref[pl.ds(off, tk), :]
        sc = jnp.zeros((tq, tk), F32)
        for h in range(IDX_HEADS):
            qp = qix[:, (h // 2) * LANE:(h // 2 + 1) * LANE]
            d = _nt_dot(qp, klo if h % 2 == 0 else khi)
            sc = sc + w[:, h:h + 1] * jnp.maximum(d, 0.0)
        sc = jnp.where(off + col0 <= row, sc, -jnp.inf)
        bits = pltpu.bitcast(sc, jnp.int32)
        key_scr[:, pl.ds(off, tk)] = bits ^ ((bits >> 31) & jnp.int32(0x7FFFFFFF))
        return carry

    lax.fori_loop(0, n_chunks, score_chunk, 0)

    def bit_step(bi, t):
        cand = t + (jnp.int32(1) << (31 - bi))

        def count_chunk(c, acc):
            off = pl.multiple_of(c * tk, tk)
            ge = jnp.where(key_scr[:, pl.ds(off, tk)] >= cand, 1.0, 0.0).astype(BF16)
            for p in range(tk // LANE):
                acc = acc + ge[:, p * LANE:(p + 1) * LANE]
            return acc

        acc = lax.fori_loop(0, n_chunks, count_chunk, jnp.zeros((tq, LANE), BF16))
        cnt = jnp.sum(acc.astype(F32), axis=1, keepdims=True)
        return jnp.where(cnt >= topk, cand, t)

    thr = lax.fori_loop(0, 32, bit_step, jnp.full((tq, 1), INT32_MIN, jnp.int32))

    def bias_chunk(c, carry):
        off = pl.multiple_of(c * tk, tk)
        ok = (key_scr[:, pl.ds(off, tk)] >= thr) & (off + col0 <= row)
        bias_scr[:, pl.ds(off, tk)] = jnp.where(ok, 0.0, MASK_NEG)
        return carry

    lax.fori_loop(0, n_chunks, bias_chunk, 0)

    rep = DSA_HEADS // DSA_KV_HEADS
    groups = [slice(g * LANE, (g + 1) * LANE) for g in range(DSA_KV_HEADS)]
    for r0 in range(0, tq, Q_BLOCK):
        rows = slice(r0, r0 + Q_BLOCK)
        qgs = [jnp.concatenate(
            [qb_ref[rows, (g * rep + j) * LANE:(g * rep + j + 1) * LANE] for j in range(rep)],
            axis=0) for g in range(DSA_KV_HEADS)]

        def att_chunk(c, carries, rows=rows, qgs=qgs):
            off = pl.multiple_of(c * tk, tk)
            bias = bias_scr[rows, pl.ds(off, tk)].astype(SCORE_DTYPE)
            bias = jnp.concatenate([bias] * rep, axis=0)
            scores = [_nt_dot(qg, kb_ref[pl.ds(off, tk), gsl], SCORE_DTYPE) + bias
                      for qg, gsl in zip(qgs, groups)]
            values = [vb_ref[pl.ds(off, tk), 2 * g * LANE:(2 * g + 2) * LANE]
                      for g in range(DSA_KV_HEADS)]
            return _softmax_steps(scores, values, carries)

        group_chunks = ((i * tq + r0 + Q_BLOCK) + tk - 1) // tk
        carries = lax.fori_loop(0, group_chunks, att_chunk,
                                _softmax_init(rep * Q_BLOCK, DSA_HEAD_DIM, DSA_KV_HEADS))
        for g, (_, l, acc) in enumerate(carries):
            o = acc / l
            for j in range(rep):
                hq = g * rep + j
                o_ref[rows, hq * LANE:(hq + 1) * LANE] = (
                    o[j * Q_BLOCK:(j + 1) * Q_BLOCK].astype(o_ref.dtype))


def _dsa_attention(qb, kb, vb, qix, klo, khi, w, b, t_pad, topk):
    tq = _pick(t_pad, (384, 128))
    tk = _pick(t_pad, (384, 128))
    assert t_pad // LANE <= 256, "bf16 partial counts must stay exact"
    kvw = DSA_KV_HEADS * DSA_HEAD_DIM
    r3 = lambda a: a.reshape(b, t_pad, a.shape[-1])
    qspec = lambda wd: pl.BlockSpec((None, tq, wd), lambda bb, i: (bb, i, 0))
    kspec = lambda wd: pl.BlockSpec((None, t_pad, wd), lambda bb, i: (bb, 0, 0),
                                    pipeline_mode=pl.Buffered(1))
    out = pl.pallas_call(
        functools.partial(_dsa_kernel, tq=tq, tk=tk, topk=topk),
        grid=(b, t_pad // tq),
        in_specs=[qspec(DSA_HEADS * DSA_HEAD_DIM), kspec(kvw), kspec(2 * kvw),
                  qspec(IDX_HEADS * IDX_DIM), kspec(LANE), kspec(LANE), qspec(LANE)],
        out_specs=qspec(DSA_HEADS * DSA_HEAD_DIM),
        out_shape=jax.ShapeDtypeStruct((b, t_pad, DSA_HEADS * DSA_HEAD_DIM), BF16),
        scratch_shapes=[pltpu.VMEM((tq, t_pad), jnp.int32), pltpu.VMEM((tq, t_pad), F32)],
        compiler_params=_params(("parallel", "arbitrary"), 58),
        name="dsa_attention",
    )(r3(qb), r3(kb), r3(vb), r3(qix), r3(klo), r3(khi), r3(w))
    return out.reshape(b * t_pad, DSA_HEADS * DSA_HEAD_DIM)


def _sigmoid(x):
    return 1.0 / (1.0 + jnp.exp(-x))


def _branch_kernel(oa_ref, ob_ref, wa_ref, wb_ref, ga_ref, gb_ref, o_ref):
    ya = jnp.dot(oa_ref[...], wa_ref[...], preferred_element_type=F32)
    yb = jnp.dot(ob_ref[...], wb_ref[...], preferred_element_type=F32)
    o_ref[...] = (_sigmoid(ga_ref[...]) * ya + _sigmoid(gb_ref[...]) * yb).astype(o_ref.dtype)


def _residual_mm_kernel(a_ref, b_ref, h_ref, o_ref):
    o_ref[...] = h_ref[...] + jnp.dot(a_ref[...], b_ref[...], preferred_element_type=F32)


def _pack_rows(rows):
    assert len(rows) == SUBLANE
    shape = (SUBLANE, rows[0].shape[1])
    sub = lax.broadcasted_iota(jnp.int32, shape, 0)
    out = jnp.broadcast_to(rows[-1], shape)
    for k in range(SUBLANE - 2, -1, -1):
        out = jnp.where(sub == k, rows[k], out)
    return out


def _peer_select_kernel(qt_ref, keys_ref, lim_ref, rank_ref, e1_ref, e2_ref):
    neg_inf = -jnp.inf
    for h in range(PEER_HEADS):
        scores, tops = [], []
        rank2 = None
        for c in range(2):
            sl = slice((2 * h + c) * PEER_HALF, (2 * h + c + 1) * PEER_HALF)
            s = jnp.dot(keys_ref[sl, :], qt_ref[sl, :].astype(BF16), preferred_element_type=F32)
            scores.append(s)
            vals, x = [], s
            if c == 1:
                rank2 = jnp.full(s.shape, NOT_RANKED, F32)
            for k in range(PEER_TOPK):
                m = jnp.max(x, axis=0, keepdims=True)
                vals.append(m)
                found = x == m
                if c == 1:
                    rank2 = jnp.where(found, float(k), rank2)
                x = jnp.where(found, neg_inf, x)
            tops.append(vals)
        a, bv = tops
        b_packs = [_pack_rows(bv[k:k + SUBLANE]) for k in range(0, PEER_TOPK, SUBLANE)]
        sub = lax.broadcasted_iota(jnp.int32, b_packs[0].shape, 0)
        cands, owners, single = [], [], []
        for i in range(PEER_TOPK):
            partners = PEER_TOPK // (i + 1)
            if partners == 1:
                single.append(i)
                continue
            for k in range(0, partners, SUBLANE):
                cd = a[i] + b_packs[k // SUBLANE]
                if partners - k < SUBLANE:
                    cd = jnp.where(sub < partners - k, cd, neg_inf)
                cands.append(cd)
                owners.append([i] * SUBLANE)
        assert len(single) % SUBLANE == 0
        for k in range(0, len(single), SUBLANE):
            ids = single[k:k + SUBLANE]
            cands.append(_pack_rows([a[i] for i in ids]) + bv[0])
            owners.append(ids)
        originals = list(cands)
        v0 = a[0] + bv[0]
        z = jnp.zeros_like(v0)
        tau = v0
        for _ in range(PEER_TOPK):
            tau = jnp.max(functools.reduce(jnp.maximum, cands), axis=0, keepdims=True)
            z = z + jnp.exp(tau - v0)
            cands = [jnp.where(cd == tau, neg_inf, cd) for cd in cands]
        partners = [None] * PEER_TOPK
        for cd, ids in zip(originals, owners):
            reached = jnp.where(cd >= tau, 1.0, 0.0)
            if len(set(ids)) == 1:
                cnt = jnp.sum(reached, axis=0, keepdims=True)
                partners[ids[0]] = cnt if partners[ids[0]] is None else partners[ids[0]] + cnt
            else:
                for k, i in enumerate(ids):
                    partners[i] = jnp.sum(jnp.where(sub == k, reached, 0.0), axis=0, keepdims=True)
        lim = jnp.full(scores[0].shape, -1.0, F32)
        for i in range(PEER_TOPK):
            lim = jnp.where(scores[0] == a[i], partners[i] - 1.0, lim)
        hs = slice(h * PEER_N_KEYS, (h + 1) * PEER_N_KEYS)
        lim_ref[hs, :] = lim
        rank_ref[hs, :] = rank2.astype(rank_ref.dtype)
        e1_ref[hs, :] = jnp.exp(scores[0] - a[0])
        e2_ref[hs, :] = (jnp.exp(scores[1] - bv[0]) / z).astype(e2_ref.dtype)


def _peer_select(qt):
    rows, n = qt[0].shape
    tm = _pick(n, (256, 128))
    qt_arr, keys = qt
    hk = PEER_HEADS * PEER_N_KEYS
    big = lambda: pl.BlockSpec((hk, tm), lambda i: (0, i))
    return pl.pallas_call(
        _peer_select_kernel,
        grid=(n // tm,),
        in_specs=[pl.BlockSpec((rows, tm), lambda i: (0, i)),
                  pl.BlockSpec(keys.shape, lambda i: (0, 0))],
        out_specs=[big(), big(), big(), big()],
        out_shape=[jax.ShapeDtypeStruct((hk, n), F32), jax.ShapeDtypeStruct((hk, n), BF16),
                   jax.ShapeDtypeStruct((hk, n), F32), jax.ShapeDtypeStruct((hk, n), BF16)],
        compiler_params=_params(("parallel",), 40),
        name="peer_select",
    )(qt_arr, keys)


def _gelu_tanh(x):
    c = np.float32(np.sqrt(2.0 / np.pi))
    return x * (0.5 * (1.0 + jnp.tanh(c * (x + 0.044715 * (x * x * x)))))


def _peer_main_kernel(hn_ref, u_ref, vt_ref, lim_ref, rank_ref, e1_ref, e2_ref, o_ref,
                      raw_scr, a_scr, *, rows_per_tile, n_tiles):
    j = pl.program_id(1)
    last_slot = (n_tiles - 1) % 2

    def first_matmul(slot):
        raw_scr[slot] = jnp.dot(u_ref[...], hn_ref[...], preferred_element_type=F32)

    def gate_and_project(slot, init):
        tm = o_ref.shape[1]
        for rr in range(rows_per_tile):
            r = (j - 1) * rows_per_tile + rr
            lim_rows = [jnp.broadcast_to(lim_ref[pl.ds(h * PEER_N_KEYS + r, 1), :],
                                         (GATE_ROWS, tm)).astype(BF16) for h in range(PEER_HEADS)]
            e1_rows = [jnp.broadcast_to(e1_ref[pl.ds(h * PEER_N_KEYS + r, 1), :],
                                        (GATE_ROWS, tm)).astype(BF16) for h in range(PEER_HEADS)]
            for c0 in range(0, PEER_N_KEYS, GATE_ROWS):
                gate = None
                for h in range(PEER_HEADS):
                    cs = slice(h * PEER_N_KEYS + c0, h * PEER_N_KEYS + c0 + GATE_ROWS)
                    hit = rank_ref[cs, :] <= lim_rows[h]
                    term = jnp.where(hit, e2_ref[cs, :] * e1_rows[h], 0.0)
                    gate = term if gate is None else gate + term
                es = slice(rr * PEER_N_KEYS + c0, rr * PEER_N_KEYS + c0 + GATE_ROWS)
                act = _gelu_tanh(raw_scr[slot, es, :]) * gate.astype(F32)
                a_scr[es, :] = act.astype(BF16)
        contrib = jnp.dot(vt_ref[...], a_scr[...], preferred_element_type=F32)
        o_ref[...] = contrib if init else o_ref[...] + contrib

    @pl.when(j == 0)
    def _():
        first_matmul(0)

    @pl.when(j == 1)
    def _():
        first_matmul(1)
        gate_and_project(0, init=True)

    @pl.when((j > 1) & (j < n_tiles) & (j % 2 == 0))
    def _():
        first_matmul(0)
        gate_and_project(1, init=False)

    @pl.when((j > 1) & (j < n_tiles) & (j % 2 == 1))
    def _():
        first_matmul(1)
        gate_and_project(0, init=False)

    @pl.when(j == n_tiles)
    def _():
        gate_and_project(last_slot, init=False)


def _peer_main(hn_t, u, vt, lim, rank2, e1, e2):
    d, n = hn_t.shape
    n_exp = u.shape[0]
    tm = _pick(n, (512, 256, 128))
    te = 512
    hk = PEER_HEADS * PEER_N_KEYS
    tok = lambda rows: pl.BlockSpec((rows, tm), lambda i, j: (0, i), pipeline_mode=pl.Buffered(1))
    n_tiles = n_exp // te
    assert n_tiles >= 2
    return pl.pallas_call(
        functools.partial(_peer_main_kernel, rows_per_tile=te // PEER_N_KEYS, n_tiles=n_tiles),
        grid=(n // tm, n_tiles + 1),
        in_specs=[tok(d),
                  pl.BlockSpec((te, d), lambda i, j: (jnp.minimum(j, n_tiles - 1), 0)),
                  pl.BlockSpec((d, te), lambda i, j: (0, jnp.maximum(j - 1, 0))),
                  tok(hk), tok(hk), tok(hk), tok(hk)],
        out_specs=pl.BlockSpec((d, tm), lambda i, j: (0, i)),
        out_shape=jax.ShapeDtypeStruct((d, n), F32),
        scratch_shapes=[pltpu.VMEM((2, te, tm), F32), pltpu.VMEM((te, tm), BF16)],
        compiler_params=_params(("parallel", "arbitrary"), 58),
        name="peer_main",
    )(hn_t, u, vt, lim, rank2, e1, e2)


def _rmsnorm_t_kernel(x_ref, g_ref, o_ref):
    o_ref[...] = _rms(x_ref[...], g_ref[...]).T.astype(o_ref.dtype)


def _final_kernel(h_ref, p_ref, g_ref, o_ref):
    o_ref[...] = _rms(h_ref[...] + p_ref[...].T, g_ref[...])


def _in_proj_layout(d):
    segs = [("gate_a", d), ("gate_b", d), ("q_b", 2048), ("c_q", 1024), ("q_ix", 1024),
            ("c_kv", 512), ("k_b", 512), ("v_b", 512), ("k_pe", LANE), ("k_ix", LANE),
            ("w_ix", LANE)]
    segs.sort(key=lambda s: -s[1])
    offs, o = {}, 0
    for name, wd in segs:
        assert o % wd == 0
        offs[name] = (o, wd)
        o += wd
    total = -(-o // 1024) * 1024
    return offs, total


def _build_w_in(w_in, d):
    src = {}
    o = 0
    for name, wd in (("c_q", MLA_Q_LORA), ("c_kv", MLA_KV_LORA), ("k_pe", MLA_ROPE),
                     ("q_b", 2048), ("k_b", 512), ("v_b", 512), ("q_ix", 1024),
                     ("k_ix", IDX_DIM), ("w_ix", IDX_HEADS), ("gate_a", d), ("gate_b", d)):
        src[name] = w_in[:, o:o + wd]
        o += wd
    offs, total = _in_proj_layout(d)
    cols = []
    o = 0
    for name, (off, wd) in sorted(offs.items(), key=lambda kv: kv[1][0]):
        blk = src[name]
        cols.append(jnp.pad(blk, ((0, 0), (0, wd - blk.shape[1]))))
        o = off + wd
    cols.append(jnp.zeros((d, total - o), w_in.dtype))
    return jnp.concatenate(cols, axis=1).astype(BF16), offs


def kernel(x, meta_tokens, attn_norm_g, w_in, q_norm_g, w_uq, kv_norm_g, w_ukv, w_branch, w_out,
           ffn_norm_g, peer_w_q, peer_sub_keys, peer_u, peer_v, final_norm_g):
    b, s, d = x.shape
    assert attn_norm_g.shape[0] == 1, "single-layer block"
    assert s % Q_BLOCK == 0 and d % 512 == 0 and (d & (d - 1)) == 0
    assert N_META % SUBLANE == 0
    t = s + N_META
    t_pad = -(-t // Q_BLOCK) * Q_BLOCK
    topk = min(DSA_TOPK_MAX, s // 4)
    n = b * t_pad

    meta = jnp.broadcast_to(meta_tokens.astype(x.dtype)[None], (b, N_META, d))
    h0 = jnp.concatenate([meta, x, jnp.zeros((b, t_pad - t, d), x.dtype)], axis=1).reshape(n, d)

    w_in_p, offs = _build_w_in(w_in[0], d)
    wq = w_uq[0].reshape(MLA_Q_LORA, MLA_HEADS, MLA_NOPE + MLA_ROPE)
    wq = jnp.pad(wq, ((0, 0), (0, 0), (0, MLA_QK_PAD - MLA_NOPE - MLA_ROPE)))
    wq = wq.reshape(MLA_Q_LORA, MLA_HEADS * MLA_QK_PAD).astype(BF16)
    wkv = w_ukv[0].reshape(MLA_KV_LORA, MLA_HEADS, MLA_NOPE + MLA_V)
    wkv = jnp.concatenate([wkv[:, :, :MLA_NOPE].reshape(MLA_KV_LORA, -1),
                           wkv[:, :, MLA_NOPE:].reshape(MLA_KV_LORA, -1)], axis=1).astype(BF16)
    w_br = w_branch[0].astype(BF16)
    w_o = w_out[0].astype(BF16)
    w_pq_t = peer_w_q[0].T.astype(BF16)
    keys = peer_sub_keys[0].reshape(PEER_HEADS * 2 * PEER_N_KEYS, PEER_HALF).astype(BF16)
    u_tab = peer_u[0].astype(BF16)
    v_tab_t = peer_v[0].T.astype(BF16)

    mla_tabs = _rope_tables(t_pad, LANE, MLA_ROPE)
    dsa_tabs = _rope_tables(t_pad, LANE, DSA_ROT)
    idx_tabs = _rope_tables(t_pad, IDX_DIM, IDX_ROT)

    hn = _rmsnorm(h0, attn_norm_g[0], BF16)
    proj = _matmul(hn, w_in_p, F32, "in_proj")

    tm = _pick(t_pad, (384, 128))
    n_pos = t_pad // tm
    row = lambda wd, off: pl.BlockSpec((tm, wd), lambda i, o=off // wd: (i, o))
    seg = lambda name: row(offs[name][1], offs[name][0])
    tab = pl.BlockSpec((tm, LANE), lambda i: (i % n_pos, 0))
    full = lambda a: pl.BlockSpec(a.shape, lambda i: (0,) * a.ndim)
    out_row = lambda wd: pl.BlockSpec((tm, wd), lambda i: (i, 0))
    sds = lambda wd, dt: jax.ShapeDtypeStruct((n, wd), dt)

    qg = q_norm_g[0].reshape(1, -1).astype(F32)
    q_a = pl.pallas_call(
        _mla_q_kernel, grid=(n // tm,),
        in_specs=[seg("c_q"), full(qg), full(wq), tab, tab, tab],
        out_specs=out_row(MLA_HEADS * MLA_QK_PAD),
        out_shape=sds(MLA_HEADS * MLA_QK_PAD, BF16),
        compiler_params=_params(("parallel",), 48), name="mla_q_prep",
    )(proj, qg, wq, *mla_tabs)

    kg = kv_norm_g[0].reshape(1, -1).astype(F32)
    k_a, v_a = pl.pallas_call(
        _mla_kv_kernel, grid=(n // tm,),
        in_specs=[seg("c_kv"), full(kg), full(wkv), seg("k_pe"), tab, tab, tab],
        out_specs=[out_row(MLA_HEADS * MLA_QK_PAD), out_row(MLA_HEADS * 2 * MLA_V)],
        out_shape=[sds(MLA_HEADS * MLA_QK_PAD, BF16), sds(MLA_HEADS * 2 * MLA_V, BF16)],
        compiler_params=_params(("parallel",), 48), name="mla_kv_prep",
    )(proj, kg, wkv, proj, *mla_tabs)

    kvw = DSA_KV_HEADS * DSA_HEAD_DIM
    q_b, k_b, v_b, q_ix, k_lo, k_hi, w_ix = pl.pallas_call(
        _dsa_prep_kernel, grid=(n // tm,),
        in_specs=[seg("q_b"), seg("k_b"), seg("v_b"), seg("q_ix"), seg("k_ix"), seg("w_ix")]
        + [tab] * 6,
        out_specs=[out_row(2048), out_row(kvw), out_row(2 * kvw), out_row(1024),
                   out_row(LANE), out_row(LANE), out_row(LANE)],
        out_shape=[sds(2048, BF16), sds(kvw, BF16), sds(2 * kvw, BF16), sds(1024, BF16),
                   sds(LANE, BF16), sds(LANE, BF16), sds(LANE, F32)],
        compiler_params=_params(("parallel",), 48), name="dsa_prep",
    )(proj, proj, proj, proj, proj, proj, *dsa_tabs, *idx_tabs)

    o_a = _mla_attention(q_a, k_a, v_a, b, t_pad)
    o_b = _dsa_attention(q_b, k_b, v_b, q_ix, k_lo, k_hi, w_ix, b, t_pad, topk)

    tmm = _pick(n, (512, 256, 128))
    tn = _pick(d, (1024, 512))
    ga_off, gb_off = offs["gate_a"][0] // tn, offs["gate_b"][0] // tn
    bw = MLA_HEADS * MLA_V
    merged = pl.pallas_call(
        _branch_kernel, grid=(d // tn, n // tmm),
        in_specs=[pl.BlockSpec((tmm, bw), lambda j, i: (i, 0)),
                  pl.BlockSpec((tmm, bw), lambda j, i: (i, 0)),
                  pl.BlockSpec((bw, tn), lambda j, i: (0, j)),
                  pl.BlockSpec((bw, tn), lambda j, i: (1, j)),
                  pl.BlockSpec((tmm, tn), lambda j, i: (i, ga_off + j)),
                  pl.BlockSpec((tmm, tn), lambda j, i: (i, gb_off + j))],
        out_specs=pl.BlockSpec((tmm, tn), lambda j, i: (i, j)),
        out_shape=jax.ShapeDtypeStruct((n, d), BF16),
        compiler_params=_params(("parallel", "parallel"), 48), name="branch_merge",
    )(o_a, o_b, w_br, w_br, proj, proj)

    h1 = pl.pallas_call(
        _residual_mm_kernel, grid=(d // tn, n // tmm),
        in_specs=[pl.BlockSpec((tmm, d), lambda j, i: (i, 0)),
                  pl.BlockSpec((d, tn), lambda j, i: (0, j)),
                  pl.BlockSpec((tmm, tn), lambda j, i: (i, j))],
        out_specs=pl.BlockSpec((tmm, tn), lambda j, i: (i, j)),
        out_shape=jax.ShapeDtypeStruct((n, d), F32),
        compiler_params=_params(("parallel", "parallel"), 48), name="out_proj",
    )(merged, w_o, h0)

    np_tok = b * s
    tf = _pick(s, (256, 128))
    tiles_per_seq = s // tf
    real_rows = pl.BlockSpec((pl.Element(tf), pl.Element(d)),
                             lambda bb, t: (pl.multiple_of(bb * t_pad + N_META + t * tf, SUBLANE), 0))
    vec = pl.BlockSpec((1, d), lambda bb, t: (0, 0))

    hn2_t = pl.pallas_call(
        _rmsnorm_t_kernel, grid=(b, tiles_per_seq),
        in_specs=[real_rows, vec],
        out_specs=pl.BlockSpec((d, tf), lambda bb, t: (0, bb * tiles_per_seq + t)),
        out_shape=jax.ShapeDtypeStruct((d, np_tok), BF16),
        compiler_params=_params(("parallel", "parallel"), 40), name="ffn_norm_t",
    )(h1, ffn_norm_g[0].reshape(1, d).astype(F32))
    q_t = _matmul(w_pq_t, hn2_t, F32, "peer_query")
    lim, rank2, e1, e2 = _peer_select((q_t, keys))
    peer_t = _peer_main(hn2_t, u_tab, v_tab_t, lim, rank2, e1, e2)

    out = pl.pallas_call(
        _final_kernel, grid=(b, tiles_per_seq),
        in_specs=[real_rows,
                  pl.BlockSpec((d, tf), lambda bb, t: (0, bb * tiles_per_seq + t)), vec],
        out_specs=pl.BlockSpec((tf, d), lambda bb, t: (bb * tiles_per_seq + t, 0)),
        out_shape=jax.ShapeDtypeStruct((np_tok, d), F32),
        compiler_params=_params(("parallel", "parallel"), 40), name="final_norm",
    )(h1, peer_t, final_norm_g.reshape(1, d).astype(F32))
    return out.reshape(b, s, d)
```

```python
import functools

import jax
import jax.numpy as jnp
import numpy as np
from jax import lax
from jax.experimental import pallas as pl
from jax.experimental.pallas import tpu as pltpu

N_META = 16
ROPE_THETA = 500000.0
Q_BLOCK = 128
NORM_EPS = 1e-6
MLA_HEADS = 16
MLA_Q_LORA = 1024
MLA_KV_LORA = 512
MLA_NOPE = 128
MLA_ROPE = 64
MLA_V = 128
DSA_HEADS = 16
DSA_KV_HEADS = 4
DSA_HEAD_DIM = 128
DSA_ROT = DSA_HEAD_DIM // 4
IDX_HEADS = 16
IDX_DIM = 64
IDX_ROT = IDX_DIM // 4
DSA_TOPK_MAX = 256
PEER_HEADS = 8
PEER_N_KEYS = 128
PEER_HALF = 128
PEER_TOPK = 16

LANE = 128
SUBLANE = 8
MXU_WIDTH = 256
GATE_ROWS = 16
MLA_HEAD_GROUP = 4
LOG2_E = 1.4426950408889634
SCORE_DTYPE = jnp.bfloat16
PEER_EXPERT_TILE = 512
NOT_RANKED = 127.0
MLA_QK_PAD = 2 * LANE
MASK_NEG = -1e30
INT32_MIN = -(2 ** 31)
MIB = 1024 * 1024

BF16 = jnp.bfloat16
F32 = jnp.float32


def _pick(n, prefs):
    for p in prefs:
        if n % p == 0:
            return p
    raise ValueError(f"no tile in {prefs} divides {n}")


def _params(sem, vmem_mib, flags=None):
    return pltpu.CompilerParams(dimension_semantics=sem, vmem_limit_bytes=vmem_mib * MIB,
                                flags=flags)


def _nt_dot(a, b, out_dtype=F32):
    out = lax.dot_general(a, b, (((1,), (1,)), ((), ())), preferred_element_type=F32)
    return out.astype(out_dtype)


def _rms(x, g):
    ms = jnp.mean(x * x, axis=-1, keepdims=True)
    return x * lax.rsqrt(ms + NORM_EPS) * g


def _rope(x, c, s_up, s_dn, half):
    return x * c + pltpu.roll(x, half, 1) * s_up + pltpu.roll(x, LANE - half, 1) * s_dn


def _rope_tables(t_pad, block, rot):
    half = rot // 2
    inv = ROPE_THETA ** (-jnp.arange(0, rot, 2, dtype=F32) / rot)
    ang = jnp.arange(t_pad, dtype=F32)[:, None] * inv[None, :]
    cos, sin = jnp.cos(ang), jnp.sin(ang)
    lane = np.arange(LANE) % block
    lo = lane < half
    hi = (lane >= half) & (lane < rot)
    idx = np.where(lo, lane, np.where(hi, lane - half, 0))
    cos_l, sin_l = cos[:, idx], sin[:, idx]
    c = jnp.where(lo | hi, cos_l, 1.0)
    s_up = jnp.where(hi, sin_l, 0.0)
    s_dn = jnp.where(lo, -sin_l, 0.0)
    return c.astype(F32), s_up.astype(F32), s_dn.astype(F32)


def _rmsnorm_kernel(x_ref, g_ref, o_ref):
    o_ref[...] = _rms(x_ref[...].astype(F32), g_ref[...]).astype(o_ref.dtype)


def _rmsnorm(x, g, out_dtype):
    m, d = x.shape
    tm = _pick(m, (256, 128))
    return pl.pallas_call(
        _rmsnorm_kernel,
        grid=(m // tm,),
        in_specs=[pl.BlockSpec((tm, d), lambda i: (i, 0)),
                  pl.BlockSpec((1, d), lambda i: (0, 0))],
        out_specs=pl.BlockSpec((tm, d), lambda i: (i, 0)),
        out_shape=jax.ShapeDtypeStruct((m, d), out_dtype),
        compiler_params=_params(("parallel",), 40),
        name="rmsnorm",
    )(x, g.reshape(1, d).astype(F32))


def _mm_kernel(a_ref, b_ref, o_ref):
    o_ref[...] = jnp.dot(a_ref[...], b_ref[...], preferred_element_type=F32).astype(o_ref.dtype)


def _matmul(a, b, out_dtype, name):
    m, k = a.shape
    _, n = b.shape
    tm = _pick(m, (512, 256, 128))
    tn = _pick(n, (1024, 512, 256, 128))
    return pl.pallas_call(
        _mm_kernel,
        grid=(n // tn, m // tm),
        in_specs=[pl.BlockSpec((tm, k), lambda j, i: (i, 0)),
                  pl.BlockSpec((k, tn), lambda j, i: (0, j))],
        out_specs=pl.BlockSpec((tm, tn), lambda j, i: (i, j)),
        out_shape=jax.ShapeDtypeStruct((m, n), out_dtype),
        compiler_params=_params(("parallel", "parallel"), 48),
        name=name,
    )(a, b)


def _mla_q_kernel(cq_ref, g_ref, w_ref, c_ref, su_ref, sd_ref, o_ref):
    xn = _rms(cq_ref[...], g_ref[...]).astype(BF16)
    q = jnp.dot(xn, w_ref[...], preferred_element_type=F32)
    q = q * ((MLA_NOPE + MLA_ROPE) ** -0.5 * LOG2_E)
    c, su, sd = c_ref[...], su_ref[...], sd_ref[...]
    for h in range(MLA_HEADS):
        base = h * MLA_QK_PAD
        o_ref[:, base:base + LANE] = q[:, base:base + LANE].astype(BF16)
        o_ref[:, base + LANE:base + 2 * LANE] = _rope(
            q[:, base + LANE:base + 2 * LANE], c, su, sd, MLA_ROPE // 2).astype(BF16)


def _mla_kv_kernel(ckv_ref, g_ref, w_ref, kpe_ref, c_ref, su_ref, sd_ref, k_ref, v_ref):
    xn = _rms(ckv_ref[...], g_ref[...]).astype(BF16)
    kv = jnp.dot(xn, w_ref[...], preferred_element_type=F32)
    kpe = _rope(kpe_ref[...], c_ref[...], su_ref[...], sd_ref[...], MLA_ROPE // 2).astype(BF16)
    nk = MLA_HEADS * MLA_NOPE
    for h in range(MLA_HEADS):
        base = h * MLA_QK_PAD
        k_ref[:, base:base + LANE] = kv[:, h * MLA_NOPE:(h + 1) * MLA_NOPE].astype(BF16)
        k_ref[:, base + LANE:base + 2 * LANE] = kpe
        v_ref[:, base:base + MLA_V] = kv[:, nk + h * MLA_V:nk + (h + 1) * MLA_V].astype(BF16)
        v_ref[:, base + MLA_V:base + 2 * MLA_V] = jnp.ones((kv.shape[0], MLA_V), BF16)


def _dsa_prep_kernel(qb_ref, kb_ref, vb_ref, qix_ref, kix_ref, wix_ref,
                     dc_ref, dsu_ref, dsd_ref, ic_ref, isu_ref, isd_ref,
                     qb_o, kb_o, vb_o, qix_o, klo_o, khi_o, w_o):
    dc, dsu, dsd = dc_ref[...], dsu_ref[...], dsd_ref[...]
    ic, isu, isd = ic_ref[...], isu_ref[...], isd_ref[...]
    for h in range(DSA_HEADS):
        sl = slice(h * LANE, (h + 1) * LANE)
        qb = _rope(qb_ref[:, sl], dc, dsu, dsd, DSA_ROT // 2)
        qb_o[:, sl] = (qb * (DSA_HEAD_DIM ** -0.5 * LOG2_E)).astype(BF16)
    for h in range(DSA_KV_HEADS):
        sl = slice(h * LANE, (h + 1) * LANE)
        kb_o[:, sl] = _rope(kb_ref[:, sl], dc, dsu, dsd, DSA_ROT // 2).astype(BF16)
    for h in range(DSA_KV_HEADS):
        vb_o[:, 2 * h * LANE:(2 * h + 1) * LANE] = vb_ref[:, h * LANE:(h + 1) * LANE].astype(BF16)
        vb_o[:, (2 * h + 1) * LANE:(2 * h + 2) * LANE] = jnp.ones((vb_ref.shape[0], LANE), BF16)
    for p in range(IDX_HEADS * IDX_DIM // LANE):
        sl = slice(p * LANE, (p + 1) * LANE)
        qix_o[:, sl] = _rope(qix_ref[:, sl], ic, isu, isd, IDX_ROT // 2).astype(BF16)
    kix = _rope(kix_ref[...], ic, isu, isd, IDX_ROT // 2)
    klo_o[...] = kix.astype(BF16)
    khi_o[...] = pltpu.roll(kix, IDX_DIM, 1).astype(BF16)
    w_o[...] = wix_ref[...] * (IDX_HEADS ** -0.5 * IDX_DIM ** -0.5)


def _softmax_steps(scores, values, carries):
    stats = []
    for s, (m, _, _) in zip(scores, carries):
        m_new = jnp.maximum(m, jnp.max(s, axis=1, keepdims=True).astype(F32))
        alpha = jnp.exp2(m - m_new)
        p = jnp.exp2(s - m_new.astype(s.dtype))
        stats.append((m_new, alpha, p.astype(BF16)))
    out = []
    for (m_new, alpha, p), v, (_, l, acc) in zip(stats, values, carries):
        pv = jnp.dot(p, v, preferred_element_type=F32)
        width = acc.shape[1]
        out.append((m_new, alpha * l + pv[:, width:width + 1], alpha * acc + pv[:, :width]))
    return tuple(out)


def _softmax_init(rows, width, chains):
    return tuple((jnp.full((rows, 1), MASK_NEG, F32), jnp.zeros((rows, 1), F32),
                  jnp.zeros((rows, width), F32)) for _ in range(chains))


def _mla_attn_kernel(q_ref, k_ref, v_ref, o_ref, *, tq, heads):
    i = pl.program_id(2)
    qs = [q_ref[:, h * MLA_QK_PAD:(h + 1) * MLA_QK_PAD] for h in range(heads)]

    def chunk(j, carries, diagonal):
        off = pl.multiple_of(j * tq, tq)
        scores = [_nt_dot(qs[h], k_ref[pl.ds(off, tq), h * MLA_QK_PAD:(h + 1) * MLA_QK_PAD],
                          SCORE_DTYPE) for h in range(heads)]
        if diagonal:
            row = lax.broadcasted_iota(jnp.int32, (tq, tq), 0)
            col = lax.broadcasted_iota(jnp.int32, (tq, tq), 1)
            scores = [jnp.where(col <= row, s, MASK_NEG) for s in scores]
        values = [v_ref[pl.ds(off, tq), h * 2 * MLA_V:(h + 1) * 2 * MLA_V] for h in range(heads)]
        return _softmax_steps(scores, values, carries)

    carries = lax.fori_loop(0, i, lambda j, c: chunk(j, c, False),
                            _softmax_init(tq, MLA_V, heads))
    carries = chunk(i, carries, True)
    for h, (_, l, acc) in enumerate(carries):
        o_ref[:, h * MLA_V:(h + 1) * MLA_V] = (acc / l).astype(o_ref.dtype)


def _mla_attention(q, k, v, b, t_pad):
    tq = _pick(t_pad, (384, 128))
    hg = MLA_HEAD_GROUP
    q3 = q.reshape(b, t_pad, MLA_HEADS * MLA_QK_PAD)
    k3 = k.reshape(b, t_pad, MLA_HEADS * MLA_QK_PAD)
    v3 = v.reshape(b, t_pad, MLA_HEADS * 2 * MLA_V)
    out = pl.pallas_call(
        functools.partial(_mla_attn_kernel, tq=tq, heads=hg),
        grid=(b, MLA_HEADS // hg, t_pad // tq),
        in_specs=[pl.BlockSpec((None, tq, hg * MLA_QK_PAD), lambda bb, h, i: (bb, i, h)),
                  pl.BlockSpec((None, t_pad, hg * MLA_QK_PAD), lambda bb, h, i: (bb, 0, h)),
                  pl.BlockSpec((None, t_pad, hg * 2 * MLA_V), lambda bb, h, i: (bb, 0, h))],
        out_specs=pl.BlockSpec((None, tq, hg * MLA_V), lambda bb, h, i: (bb, i, h)),
        out_shape=jax.ShapeDtypeStruct((b, t_pad, MLA_HEADS * MLA_V), BF16),
        compiler_params=_params(("parallel", "parallel", "arbitrary"), 48),
        name="mla_attention",
    )(q3, k3, v3)
    return out.reshape(b * t_pad, MLA_HEADS * MLA_V)


def _dsa_kernel(qb_ref, kb_ref, vb_ref, qix_ref, klo_ref, khi_ref, w_ref, o_ref,
                key_scr, bias_scr, *, tq, tk, topk):
    i = pl.program_id(1)
    n_chunks = ((i + 1) * tq + tk - 1) // tk
    row = i * tq + lax.broadcasted_iota(jnp.int32, (tq, tk), 0)
    col0 = lax.broadcasted_iota(jnp.int32, (tq, tk), 1)
    w = w_ref[...]
    qix = qix_ref[...]

    def score_chunk(c, carry):
        off = pl.multiple_of(c * tk, tk)
        klo = klo_ref[pl.ds(off, tk), :]
        khi = khi_ref[pl.ds(off, tk), :]
        sc = jnp.zeros((tq, tk), F32)
        for h in range(IDX_HEADS):
            qp = qix[:, (h // 2) * LANE:(h // 2 + 1) * LANE]
            d = _nt_dot(qp, klo if h % 2 == 0 else khi)
            sc = sc + w[:, h:h + 1] * jnp.maximum(d, 0.0)
        sc = jnp.where(off + col0 <= row, sc, -jnp.inf)
        bits = pltpu.bitcast(sc, jnp.int32)
        key_scr[:, pl.ds(off, tk)] = bits ^ ((bits >> 31) & jnp.int32(0x7FFFFFFF))
        return carry

    lax.fori_loop(0, n_chunks, score_chunk, 0)

    def bit_step(bi, t):
        cand = t + (jnp.int32(1) << (31 - bi))

        def count_chunk(c, acc):
            off = pl.multiple_of(c * tk, tk)
            ge = jnp.where(key_scr[:, pl.ds(off, tk)] >= cand, 1.0, 0.0).astype(BF16)
            for p in range(tk // LANE):
                acc = acc + ge[:, p * LANE:(p + 1) * LANE]
            return acc

        acc = lax.fori_loop(0, n_chunks, count_chunk, jnp.zeros((tq, LANE), BF16))
        cnt = jnp.sum(acc.astype(F32), axis=1, keepdims=True)
        return jnp.where(cnt >= topk, cand, t)

    thr = lax.fori_loop(0, 32, bit_step, jnp.full((tq, 1), INT32_MIN, jnp.int32))

    def bias_chunk(c, carry):
        off = pl.multiple_of(c * tk, tk)
        ok = (key_scr[:, pl.ds(off, tk)] >= thr) & (off + col0 <= row)
        bias_scr[:, pl.ds(off, tk)] = jnp.where(ok, 0.0, MASK_NEG)
        return carry

    lax.fori_loop(0, n_chunks, bias_chunk, 0)

    rep = DSA_HEADS // DSA_KV_HEADS
    groups = [slice(g * LANE, (g + 1) * LANE) for g in range(DSA_KV_HEADS)]
    for r0 in range(0, tq, Q_BLOCK):
        rows = slice(r0, r0 + Q_BLOCK)
        qgs = [jnp.concatenate(
            [qb_ref[rows, (g * rep + j) * LANE:(g * rep + j + 1) * LANE] for j in range(rep)],
            axis=0) for g in range(DSA_KV_HEADS)]

        def att_chunk(c, carries, rows=rows, qgs=qgs):
            off = pl.multiple_of(c * tk, tk)
            bias = bias_scr[rows, pl.ds(off, tk)].astype(SCORE_DTYPE)
            bias = jnp.concatenate([bias] * rep, axis=0)
            scores = [_nt_dot(qg, kb_ref[pl.ds(off, tk), gsl], SCORE_DTYPE) + bias
                      for qg, gsl in zip(qgs, groups)]
            values = [vb_ref[pl.ds(off, tk), 2 * g * LANE:(2 * g + 2) * LANE]
                      for g in range(DSA_KV_HEADS)]
            return _softmax_steps(scores, values, carries)

        group_chunks = ((i * tq + r0 + Q_BLOCK) + tk - 1) // tk
        carries = lax.fori_loop(0, group_chunks, att_chunk,
                                _softmax_init(rep * Q_BLOCK, DSA_HEAD_DIM, DSA_KV_HEADS))
        for g, (_, l, acc) in enumerate(carries):
            o = acc / l
            for j in range(rep):
                hq = g * rep + j
                o_ref[rows, hq * LANE:(hq + 1) * LANE] = (
                    o[j * Q_BLOCK:(j + 1) * Q_BLOCK].astype(o_ref.dtype))


def _dsa_attention(qb, kb, vb, qix, klo, khi, w, b, t_pad, topk):
    tq = _pick(t_pad, (384, 128))
    tk = _pick(t_pad, (384, 128))
    assert t_pad // LANE <= 256, "bf16 partial counts must stay exact"
    kvw = DSA_KV_HEADS * DSA_HEAD_DIM
    r3 = lambda a: a.reshape(b, t_pad, a.shape[-1])
    qspec = lambda wd: pl.BlockSpec((None, tq, wd), lambda bb, i: (bb, i, 0))
    kspec = lambda wd: pl.BlockSpec((None, t_pad, wd), lambda bb, i: (bb, 0, 0),
                                    pipeline_mode=pl.Buffered(1))
    out = pl.pallas_call(
        functools.partial(_dsa_kernel, tq=tq, tk=tk, topk=topk),
        grid=(b, t_pad // tq),
        in_specs=[qspec(DSA_HEADS * DSA_HEAD_DIM), kspec(kvw), kspec(2 * kvw),
                  qspec(IDX_HEADS * IDX_DIM), kspec(LANE), kspec(LANE), qspec(LANE)],
        out_specs=qspec(DSA_HEADS * DSA_HEAD_DIM),
        out_shape=jax.ShapeDtypeStruct((b, t_pad, DSA_HEADS * DSA_HEAD_DIM), BF16),
        scratch_shapes=[pltpu.VMEM((tq, t_pad), jnp.int32), pltpu.VMEM((tq, t_pad), F32)],
        compiler_params=_params(("parallel", "arbitrary"), 58),
        name="dsa_attention",
    )(r3(qb), r3(kb), r3(vb), r3(qix), r3(klo), r3(khi), r3(w))
    return out.reshape(b * t_pad, DSA_HEADS * DSA_HEAD_DIM)


def _sigmoid(x):
    return 1.0 / (1.0 + jnp.exp(-x))


def _branch_kernel(oa_ref, ob_ref, wa_ref, wb_ref, ga_ref, gb_ref, o_ref):
    ya = jnp.dot(oa_ref[...], wa_ref[...], preferred_element_type=F32)
    yb = jnp.dot(ob_ref[...], wb_ref[...], preferred_element_type=F32)
    o_ref[...] = (_sigmoid(ga_ref[...]) * ya + _sigmoid(gb_ref[...]) * yb).astype(o_ref.dtype)


def _residual_mm_kernel(a_ref, b_ref, h_ref, o_ref):
    o_ref[...] = h_ref[...] + jnp.dot(a_ref[...], b_ref[...], preferred_element_type=F32)


def _pack_rows(rows):
    assert len(rows) == SUBLANE
    shape = (SUBLANE, rows[0].shape[1])
    sub = lax.broadcasted_iota(jnp.int32, shape, 0)
    out = jnp.broadcast_to(rows[-1], shape)
    for k in range(SUBLANE - 2, -1, -1):
        out = jnp.where(sub == k, rows[k], out)
    return out


def _peer_select_kernel(qt_ref, keys_ref, lim_ref, rank_ref, e1_ref, e2_ref):
    neg_inf = -jnp.inf
    for h in range(PEER_HEADS):
        scores, tops = [], []
        rank2 = None
        for c in range(2):
            sl = slice((2 * h + c) * PEER_HALF, (2 * h + c + 1) * PEER_HALF)
            s = jnp.dot(keys_ref[sl, :], qt_ref[sl, :].astype(BF16), preferred_element_type=F32)
            scores.append(s)
            vals, x = [], s
            if c == 1:
                rank2 = jnp.full(s.shape, NOT_RANKED, F32)
            for k in range(PEER_TOPK):
                m = jnp.max(x, axis=0, keepdims=True)
                vals.append(m)
                found = x == m
                if c == 1:
                    rank2 = jnp.where(found, float(k), rank2)
                x = jnp.where(found, neg_inf, x)
            tops.append(vals)
        a, bv = tops
        b_packs = [_pack_rows(bv[k:k + SUBLANE]) for k in range(0, PEER_TOPK, SUBLANE)]
        sub = lax.broadcasted_iota(jnp.int32, b_packs[0].shape, 0)
        cands, owners, single = [], [], []
        for i in range(PEER_TOPK):
            partners = PEER_TOPK // (i + 1)
            if partners == 1:
                single.append(i)
                continue
            for k in range(0, partners, SUBLANE):
                cd = a[i] + b_packs[k // SUBLANE]
                if partners - k < SUBLANE:
                    cd = jnp.where(sub < partners - k, cd, neg_inf)
                cands.append(cd)
                owners.append([i] * SUBLANE)
        assert len(single) % SUBLANE == 0
        for k in range(0, len(single), SUBLANE):
            ids = single[k:k + SUBLANE]
            cands.append(_pack_rows([a[i] for i in ids]) + bv[0])
            owners.append(ids)
        originals = list(cands)
        v0 = a[0] + bv[0]
        z = jnp.zeros_like(v0)
        tau = v0
        for _ in range(PEER_TOPK):
            tau = jnp.max(functools.reduce(jnp.maximum, cands), axis=0, keepdims=True)
            z = z + jnp.exp(tau - v0)
            cands = [jnp.where(cd == tau, neg_inf, cd) for cd in cands]
        partners = [None] * PEER_TOPK
        for cd, ids in zip(originals, owners):
            reached = jnp.where(cd >= tau, 1.0, 0.0)
            if len(set(ids)) == 1:
                cnt = jnp.sum(reached, axis=0, keepdims=True)
                partners[ids[0]] = cnt if partners[ids[0]] is None else partners[ids[0]] + cnt
            else:
                for k, i in enumerate(ids):
                    partners[i] = jnp.sum(jnp.where(sub == k, reached, 0.0), axis=0, keepdims=True)
        lim = jnp.full(scores[0].shape, -1.0, F32)
        for i in range(PEER_TOPK):
            lim = jnp.where(scores[0] == a[i], partners[i] - 1.0, lim)
        hs = slice(h * PEER_N_KEYS, (h + 1) * PEER_N_KEYS)
        lim_ref[hs, :] = lim
        rank_ref[hs, :] = rank2.astype(rank_ref.dtype)
        e1_ref[hs, :] = jnp.exp(scores[0] - a[0])
        e2_ref[hs, :] = (jnp.exp(scores[1] - bv[0]) / z).astype(e2_ref.dtype)


def _peer_select(qt, tile):
    rows, n = qt[0].shape
    tm = _pick(tile, (256, 128))
    per_tile = tile // tm
    qt_arr, keys = qt
    hk = PEER_HEADS * PEER_N_KEYS
    big = lambda: pl.BlockSpec((None, hk, tm), lambda i: (i // per_tile, 0, i % per_tile))
    tiled = lambda dt: jax.ShapeDtypeStruct((n // tile, hk, tile), dt)
    return pl.pallas_call(
        _peer_select_kernel,
        grid=(n // tm,),
        in_specs=[pl.BlockSpec((rows, tm), lambda i: (0, i)),
                  pl.BlockSpec(keys.shape, lambda i: (0, 0))],
        out_specs=[big(), big(), big(), big()],
        out_shape=[tiled(F32), tiled(BF16), tiled(F32), tiled(BF16)],
        compiler_params=_params(("parallel",), 40),
        name="peer_select",
    )(qt_arr, keys)


def _gelu_tanh(x):
    c = np.float32(np.sqrt(2.0 / np.pi))
    return x * (0.5 * (1.0 + jnp.tanh(c * (x + 0.044715 * (x * x * x)))))


def _peer_main_kernel(hn_ref, u_ref, vt_ref, lim_ref, rank_ref, e1_ref, e2_ref, o_ref,
                      raw_scr, a_scr, *, rows_per_tile, n_tiles):
    j = pl.program_id(1)
    last_slot = (n_tiles - 1) % 2

    def first_matmul(slot):
        raw_scr[slot] = jnp.dot(u_ref[...], hn_ref[...], preferred_element_type=F32)

    def gate_and_project(slot, init):
        tm = o_ref.shape[1]
        for rr in range(rows_per_tile):
            r = (j - 1) * rows_per_tile + rr
            lim_rows = [jnp.broadcast_to(lim_ref[pl.ds(h * PEER_N_KEYS + r, 1), :],
                                         (GATE_ROWS, tm)).astype(BF16) for h in range(PEER_HEADS)]
            e1_rows = [jnp.broadcast_to(e1_ref[pl.ds(h * PEER_N_KEYS + r, 1), :],
                                        (GATE_ROWS, tm)).astype(BF16) for h in range(PEER_HEADS)]
            for c0 in range(0, PEER_N_KEYS, GATE_ROWS):
                gate = None
                for h in range(PEER_HEADS):
                    cs = slice(h * PEER_N_KEYS + c0, h * PEER_N_KEYS + c0 + GATE_ROWS)
                    hit = rank_ref[cs, :] <= lim_rows[h]
                    term = jnp.where(hit, e2_ref[cs, :] * e1_rows[h], 0.0)
                    gate = term if gate is None else gate + term
                es = slice(rr * PEER_N_KEYS + c0, rr * PEER_N_KEYS + c0 + GATE_ROWS)
                act = _gelu_tanh(raw_scr[slot, es, :]) * gate.astype(F32)
                a_scr[es, :] = act.astype(BF16)
        contrib = jnp.dot(vt_ref[...], a_scr[...], preferred_element_type=F32)
        o_ref[...] = contrib if init else o_ref[...] + contrib

    @pl.when(j == 0)
    def _():
        first_matmul(0)

    @pl.when(j == 1)
    def _():
        first_matmul(1)
        gate_and_project(0, init=True)

    @pl.when((j > 1) & (j < n_tiles) & (j % 2 == 0))
    def _():
        first_matmul(0)
        gate_and_project(1, init=False)

    @pl.when((j > 1) & (j < n_tiles) & (j % 2 == 1))
    def _():
        first_matmul(1)
        gate_and_project(0, init=False)

    @pl.when(j == n_tiles)
    def _():
        gate_and_project(last_slot, init=False)


def _peer_main(hn_t, u, vt, lim, rank2, e1, e2):
    n_tok_tiles, d, tm = hn_t.shape
    n_exp = u.shape[0]
    n_tiles, _, te = vt.shape
    hk = PEER_HEADS * PEER_N_KEYS
    tok = lambda rows: pl.BlockSpec((None, rows, tm), lambda i, j: (i, 0, 0),
                                    pipeline_mode=pl.Buffered(1))
    assert n_tiles >= 2 and n_tiles * te == n_exp
    return pl.pallas_call(
        functools.partial(_peer_main_kernel, rows_per_tile=te // PEER_N_KEYS, n_tiles=n_tiles),
        grid=(n_tok_tiles, n_tiles + 1),
        in_specs=[tok(d),
                  pl.BlockSpec((te, d), lambda i, j: (jnp.minimum(j, n_tiles - 1), 0)),
                  pl.BlockSpec((None, d, te), lambda i, j: (jnp.maximum(j - 1, 0), 0, 0)),
                  tok(hk), tok(hk), tok(hk), tok(hk)],
        out_specs=pl.BlockSpec((None, d, tm), lambda i, j: (i, 0, 0)),
        out_shape=jax.ShapeDtypeStruct((n_tok_tiles, d, tm), F32),
        scratch_shapes=[pltpu.VMEM((2, te, tm), F32), pltpu.VMEM((te, tm), BF16)],
        compiler_params=_params(("parallel", "arbitrary"), 58),
        name="peer_main",
    )(hn_t, u, vt, lim, rank2, e1, e2)


def _rmsnorm_t_kernel(x_ref, g_ref, o_ref):
    o_ref[...] = _rms(x_ref[...], g_ref[...]).T.astype(o_ref.dtype)


def _final_kernel(h_ref, p_ref, g_ref, o_ref):
    o_ref[...] = _rms(h_ref[...] + p_ref[...].T, g_ref[...])


def _in_proj_layout(d):
    segs = [("gate_a", d), ("gate_b", d), ("q_b", 2048), ("c_q", 1024), ("q_ix", 1024),
            ("c_kv", 512), ("k_b", 512), ("v_b", 512), ("k_pe", LANE), ("k_ix", LANE),
            ("w_ix", LANE)]
    segs.sort(key=lambda s: -s[1])
    offs, o = {}, 0
    for name, wd in segs:
        assert o % wd == 0
        offs[name] = (o, wd)
        o += wd
    total = -(-o // 1024) * 1024
    return offs, total


def _build_w_in(w_in, d):
    src = {}
    o = 0
    for name, wd in (("c_q", MLA_Q_LORA), ("c_kv", MLA_KV_LORA), ("k_pe", MLA_ROPE),
                     ("q_b", 2048), ("k_b", 512), ("v_b", 512), ("q_ix", 1024),
                     ("k_ix", IDX_DIM), ("w_ix", IDX_HEADS), ("gate_a", d), ("gate_b", d)):
        src[name] = w_in[:, o:o + wd]
        o += wd
    offs, total = _in_proj_layout(d)
    cols = []
    o = 0
    for name, (off, wd) in sorted(offs.items(), key=lambda kv: kv[1][0]):
        blk = src[name]
        cols.append(jnp.pad(blk, ((0, 0), (0, wd - blk.shape[1]))))
        o = off + wd
    cols.append(jnp.zeros((d, total - o), w_in.dtype))
    return jnp.concatenate(cols, axis=1).astype(BF16), offs


def kernel(x, meta_tokens, attn_norm_g, w_in, q_norm_g, w_uq, kv_norm_g, w_ukv, w_branch, w_out,
           ffn_norm_g, peer_w_q, peer_sub_keys, peer_u, peer_v, final_norm_g):
    b, s, d = x.shape
    assert attn_norm_g.shape[0] == 1, "single-layer block"
    assert s % Q_BLOCK == 0 and d % 512 == 0 and (d & (d - 1)) == 0
    assert N_META % SUBLANE == 0
    t = s + N_META
    t_pad = -(-t // Q_BLOCK) * Q_BLOCK
    topk = min(DSA_TOPK_MAX, s // 4)
    n = b * t_pad

    meta = jnp.broadcast_to(meta_tokens.astype(x.dtype)[None], (b, N_META, d))
    h0 = jnp.concatenate([meta, x, jnp.zeros((b, t_pad - t, d), x.dtype)], axis=1).reshape(n, d)

    w_in_p, offs = _build_w_in(w_in[0], d)
    wq = w_uq[0].reshape(MLA_Q_LORA, MLA_HEADS, MLA_NOPE + MLA_ROPE)
    wq = jnp.pad(wq, ((0, 0), (0, 0), (0, MLA_QK_PAD - MLA_NOPE - MLA_ROPE)))
    wq = wq.reshape(MLA_Q_LORA, MLA_HEADS * MLA_QK_PAD).astype(BF16)
    wkv = w_ukv[0].reshape(MLA_KV_LORA, MLA_HEADS, MLA_NOPE + MLA_V)
    wkv = jnp.concatenate([wkv[:, :, :MLA_NOPE].reshape(MLA_KV_LORA, -1),
                           wkv[:, :, MLA_NOPE:].reshape(MLA_KV_LORA, -1)], axis=1).astype(BF16)
    w_br = w_branch[0].astype(BF16)
    w_o = w_out[0].astype(BF16)
    w_pq_t = peer_w_q[0].T.astype(BF16)
    keys = peer_sub_keys[0].reshape(PEER_HEADS * 2 * PEER_N_KEYS, PEER_HALF).astype(BF16)
    u_tab = peer_u[0].astype(BF16)
    n_exp = peer_v.shape[1]
    v_tiles = peer_v[0].reshape(n_exp // PEER_EXPERT_TILE, PEER_EXPERT_TILE, d)
    v_tiles = v_tiles.transpose(0, 2, 1).astype(BF16)

    mla_tabs = _rope_tables(t_pad, LANE, MLA_ROPE)
    dsa_tabs = _rope_tables(t_pad, LANE, DSA_ROT)
    idx_tabs = _rope_tables(t_pad, IDX_DIM, IDX_ROT)

    hn = _rmsnorm(h0, attn_norm_g[0], BF16)
    proj = _matmul(hn, w_in_p, F32, "in_proj")

    tm = _pick(t_pad, (384, 128))
    n_pos = t_pad // tm
    row = lambda wd, off: pl.BlockSpec((tm, wd), lambda i, o=off // wd: (i, o))
    seg = lambda name: row(offs[name][1], offs[name][0])
    tab = pl.BlockSpec((tm, LANE), lambda i: (i % n_pos, 0))
    full = lambda a: pl.BlockSpec(a.shape, lambda i: (0,) * a.ndim)
    out_row = lambda wd: pl.BlockSpec((tm, wd), lambda i: (i, 0))
    sds = lambda wd, dt: jax.ShapeDtypeStruct((n, wd), dt)

    qg = q_norm_g[0].reshape(1, -1).astype(F32)
    q_a = pl.pallas_call(
        _mla_q_kernel, grid=(n // tm,),
        in_specs=[seg("c_q"), full(qg), full(wq), tab, tab, tab],
        out_specs=out_row(MLA_HEADS * MLA_QK_PAD),
        out_shape=sds(MLA_HEADS * MLA_QK_PAD, BF16),
        compiler_params=_params(("parallel",), 48), name="mla_q_prep",
    )(proj, qg, wq, *mla_tabs)

    kg = kv_norm_g[0].reshape(1, -1).astype(F32)
    k_a, v_a = pl.pallas_call(
        _mla_kv_kernel, grid=(n // tm,),
        in_specs=[seg("c_kv"), full(kg), full(wkv), seg("k_pe"), tab, tab, tab],
        out_specs=[out_row(MLA_HEADS * MLA_QK_PAD), out_row(MLA_HEADS * 2 * MLA_V)],
        out_shape=[sds(MLA_HEADS * MLA_QK_PAD, BF16), sds(MLA_HEADS * 2 * MLA_V, BF16)],
        compiler_params=_params(("parallel",), 48), name="mla_kv_prep",
    )(proj, kg, wkv, proj, *mla_tabs)

    kvw = DSA_KV_HEADS * DSA_HEAD_DIM
    q_b, k_b, v_b, q_ix, k_lo, k_hi, w_ix = pl.pallas_call(
        _dsa_prep_kernel, grid=(n // tm,),
        in_specs=[seg("q_b"), seg("k_b"), seg("v_b"), seg("q_ix"), seg("k_ix"), seg("w_ix")]
        + [tab] * 6,
        out_specs=[out_row(2048), out_row(kvw), out_row(2 * kvw), out_row(1024),
                   out_row(LANE), out_row(LANE), out_row(LANE)],
        out_shape=[sds(2048, BF16), sds(kvw, BF16), sds(2 * kvw, BF16), sds(1024, BF16),
                   sds(LANE, BF16), sds(LANE, BF16), sds(LANE, F32)],
        compiler_params=_params(("parallel",), 48), name="dsa_prep",
    )(proj, proj, proj, proj, proj, proj, *dsa_tabs, *idx_tabs)

    o_a = _mla_attention(q_a, k_a, v_a, b, t_pad)
    o_b = _dsa_attention(q_b, k_b, v_b, q_ix, k_lo, k_hi, w_ix, b, t_pad, topk)

    tmm = _pick(n, (512, 256, 128))
    tn = _pick(d, (1024, 512))
    ga_off, gb_off = offs["gate_a"][0] // tn, offs["gate_b"][0] // tn
    bw = MLA_HEADS * MLA_V
    merged = pl.pallas_call(
        _branch_kernel, grid=(d // tn, n // tmm),
        in_specs=[pl.BlockSpec((tmm, bw), lambda j, i: (i, 0)),
                  pl.BlockSpec((tmm, bw), lambda j, i: (i, 0)),
                  pl.BlockSpec((bw, tn), lambda j, i: (0, j)),
                  pl.BlockSpec((bw, tn), lambda j, i: (1, j)),
                  pl.BlockSpec((tmm, tn), lambda j, i: (i, ga_off + j)),
                  pl.BlockSpec((tmm, tn), lambda j, i: (i, gb_off + j))],
        out_specs=pl.BlockSpec((tmm, tn), lambda j, i: (i, j)),
        out_shape=jax.ShapeDtypeStruct((n, d), BF16),
        compiler_params=_params(("parallel", "parallel"), 48), name="branch_merge",
    )(o_a, o_b, w_br, w_br, proj, proj)

    h1 = pl.pallas_call(
        _residual_mm_kernel, grid=(d // tn, n // tmm),
        in_specs=[pl.BlockSpec((tmm, d), lambda j, i: (i, 0)),
                  pl.BlockSpec((d, tn), lambda j, i: (0, j)),
                  pl.BlockSpec((tmm, tn), lambda j, i: (i, j))],
        out_specs=pl.BlockSpec((tmm, tn), lambda j, i: (i, j)),
        out_shape=jax.ShapeDtypeStruct((n, d), F32),
        compiler_params=_params(("parallel", "parallel"), 48), name="out_proj",
    )(merged, w_o, h0)

    np_tok = b * s
    tf = _pick(s, (256, 128))
    tiles_per_seq = s // tf
    real_rows = pl.BlockSpec((pl.Element(tf), pl.Element(d)),
                             lambda bb, t: (pl.multiple_of(bb * t_pad + N_META + t * tf, SUBLANE), 0))
    vec = pl.BlockSpec((1, d), lambda bb, t: (0, 0))
    tp = _pick(np_tok, (512, 256, 128))
    per_tile = tp // tf
    tile = lambda bb, t: (bb * tiles_per_seq + t) // per_tile
    sub = lambda bb, t: (bb * tiles_per_seq + t) % per_tile
    tiled_cols = pl.BlockSpec((None, d, tf), lambda bb, t: (tile(bb, t), 0, sub(bb, t)))

    hn2_t = pl.pallas_call(
        _rmsnorm_t_kernel, grid=(b, tiles_per_seq),
        in_specs=[real_rows, vec],
        out_specs=tiled_cols,
        out_shape=jax.ShapeDtypeStruct((np_tok // tp, d, tp), BF16),
        compiler_params=_params(("parallel", "parallel"), 40), name="ffn_norm_t",
    )(h1, ffn_norm_g[0].reshape(1, d).astype(F32))
    tq_rows = _pick(w_pq_t.shape[0], (512, 256, 128))
    q_t = pl.pallas_call(
        _mm_kernel, grid=(np_tok // tp, w_pq_t.shape[0] // tq_rows),
        in_specs=[pl.BlockSpec((tq_rows, d), lambda j, i: (i, 0)),
                  pl.BlockSpec((None, d, tp), lambda j, i: (j, 0, 0))],
        out_specs=pl.BlockSpec((tq_rows, tp), lambda j, i: (i, j)),
        out_shape=jax.ShapeDtypeStruct((w_pq_t.shape[0], np_tok), F32),
        compiler_params=_params(("parallel", "parallel"), 48), name="peer_query",
    )(w_pq_t, hn2_t)
    lim, rank2, e1, e2 = _peer_select((q_t, keys), tp)
    peer_t = _peer_main(hn2_t, u_tab, v_tiles, lim, rank2, e1, e2)

    out = pl.pallas_call(
        _final_kernel, grid=(b, tiles_per_seq),
        in_specs=[real_rows, tiled_cols, vec],
        out_specs=pl.BlockSpec((tf, d), lambda bb, t: (bb * tiles_per_seq + t, 0)),
        out_shape=jax.ShapeDtypeStruct((np_tok, d), F32),
        compiler_params=_params(("parallel", "parallel"), 40), name="final_norm",
    )(h1, peer_t, final_norm_g.reshape(1, d).astype(F32))
    return out.reshape(b, s, d)
```

```python
import functools

import jax
import jax.numpy as jnp
import numpy as np
from jax import lax
from jax.experimental import pallas as pl
from jax.experimental.pallas import tpu as pltpu

N_META = 16
ROPE_THETA = 500000.0
Q_BLOCK = 128
NORM_EPS = 1e-6
MLA_HEADS = 16
MLA_Q_LORA = 1024
MLA_KV_LORA = 512
MLA_NOPE = 128
MLA_ROPE = 64
MLA_V = 128
DSA_HEADS = 16
DSA_KV_HEADS = 4
DSA_HEAD_DIM = 128
DSA_ROT = DSA_HEAD_DIM // 4
IDX_HEADS = 16
IDX_DIM = 64
IDX_ROT = IDX_DIM // 4
DSA_TOPK_MAX = 256
PEER_HEADS = 8
PEER_N_KEYS = 128
PEER_HALF = 128
PEER_TOPK = 16
DSA_Q_WIDTH = DSA_HEADS * DSA_HEAD_DIM
DSA_KV_WIDTH = DSA_KV_HEADS * DSA_HEAD_DIM
IDX_Q_WIDTH = IDX_HEADS * IDX_DIM

LANE = 128
SUBLANE = 8
MXU_WIDTH = 256
GATE_ROWS = 16
MLA_HEAD_GROUP = 4
LOG2_E = 1.4426950408889634
SCORE_DTYPE = jnp.bfloat16
NOT_RANKED = 127.0
MLA_QK_PAD = 2 * LANE
MASK_NEG = -1e30
INT32_MIN = -(2 ** 31)
MIB = 1024 * 1024
VMEM_ROWWISE_MIB = 40
VMEM_MATMUL_MIB = 48
VMEM_RESIDENT_MIB = 58
ROW_TILES = (512, 256, 128)
COL_TILES = (1024, 512, 256, 128)
PEER_EXPERT_TILE = 512

BF16 = jnp.bfloat16
F32 = jnp.float32


def _pick(n, prefs):
    for p in prefs:
        if n % p == 0:
            return p
    raise ValueError(f"no tile in {prefs} divides {n}")


def _params(sem, vmem_mib):
    return pltpu.CompilerParams(dimension_semantics=sem, vmem_limit_bytes=vmem_mib * MIB)


def _nt_dot(a, b, out_dtype=F32):
    out = lax.dot_general(a, b, (((1,), (1,)), ((), ())), preferred_element_type=F32)
    return out.astype(out_dtype)


def _rms(x, g):
    ms = jnp.mean(x * x, axis=-1, keepdims=True)
    return x * lax.rsqrt(ms + NORM_EPS) * g


def _rope(x, c, s_up, s_dn, half):
    return x * c + pltpu.roll(x, half, 1) * s_up + pltpu.roll(x, LANE - half, 1) * s_dn


def _rope_tables(t_pad, block, rot):
    half = rot // 2
    inv = ROPE_THETA ** (-jnp.arange(0, rot, 2, dtype=F32) / rot)
    ang = jnp.arange(t_pad, dtype=F32)[:, None] * inv[None, :]
    cos, sin = jnp.cos(ang), jnp.sin(ang)
    lane = np.arange(LANE) % block
    lo = lane < half
    hi = (lane >= half) & (lane < rot)
    idx = np.where(lo, lane, np.where(hi, lane - half, 0))
    cos_l, sin_l = cos[:, idx], sin[:, idx]
    c = jnp.where(lo | hi, cos_l, 1.0)
    s_up = jnp.where(hi, sin_l, 0.0)
    s_dn = jnp.where(lo, -sin_l, 0.0)
    return c.astype(F32), s_up.astype(F32), s_dn.astype(F32)


def _rmsnorm_kernel(x_ref, g_ref, o_ref):
    o_ref[...] = _rms(x_ref[...].astype(F32), g_ref[...]).astype(o_ref.dtype)


def _rmsnorm(x, g, out_dtype):
    m, d = x.shape
    tm = _pick(m, (256, 128))
    return pl.pallas_call(
        _rmsnorm_kernel,
        grid=(m // tm,),
        in_specs=[pl.BlockSpec((tm, d), lambda i: (i, 0)),
                  pl.BlockSpec((1, d), lambda i: (0, 0))],
        out_specs=pl.BlockSpec((tm, d), lambda i: (i, 0)),
        out_shape=jax.ShapeDtypeStruct((m, d), out_dtype),
        compiler_params=_params(("parallel",), VMEM_ROWWISE_MIB),
        name="rmsnorm",
    )(x, g.reshape(1, d).astype(F32))


def _mm_kernel(a_ref, b_ref, o_ref):
    o_ref[...] = jnp.dot(a_ref[...], b_ref[...], preferred_element_type=F32).astype(o_ref.dtype)


def _matmul(a, b, out_dtype, name):
    m, k = a.shape
    _, n = b.shape
    tm = _pick(m, ROW_TILES)
    tn = _pick(n, COL_TILES)
    return pl.pallas_call(
        _mm_kernel,
        grid=(n // tn, m // tm),
        in_specs=[pl.BlockSpec((tm, k), lambda j, i: (i, 0)),
                  pl.BlockSpec((k, tn), lambda j, i: (0, j))],
        out_specs=pl.BlockSpec((tm, tn), lambda j, i: (i, j)),
        out_shape=jax.ShapeDtypeStruct((m, n), out_dtype),
        compiler_params=_params(("parallel", "parallel"), VMEM_MATMUL_MIB),
        name=name,
    )(a, b)


def _mla_q_kernel(cq_ref, g_ref, w_ref, c_ref, su_ref, sd_ref, o_ref):
    xn = _rms(cq_ref[...], g_ref[...]).astype(BF16)
    q = jnp.dot(xn, w_ref[...], preferred_element_type=F32)
    q = q * ((MLA_NOPE + MLA_ROPE) ** -0.5 * LOG2_E)
    c, su, sd = c_ref[...], su_ref[...], sd_ref[...]
    for h in range(MLA_HEADS):
        base = h * MLA_QK_PAD
        o_ref[:, base:base + LANE] = q[:, base:base + LANE].astype(BF16)
        o_ref[:, base + LANE:base + 2 * LANE] = _rope(
            q[:, base + LANE:base + 2 * LANE], c, su, sd, MLA_ROPE // 2).astype(BF16)


def _mla_kv_kernel(ckv_ref, g_ref, w_ref, kpe_ref, c_ref, su_ref, sd_ref, k_ref, v_ref):
    xn = _rms(ckv_ref[...], g_ref[...]).astype(BF16)
    kv = jnp.dot(xn, w_ref[...], preferred_element_type=F32)
    kpe = _rope(kpe_ref[...], c_ref[...], su_ref[...], sd_ref[...], MLA_ROPE // 2).astype(BF16)
    nk = MLA_HEADS * MLA_NOPE
    for h in range(MLA_HEADS):
        base = h * MLA_QK_PAD
        k_ref[:, base:base + LANE] = kv[:, h * MLA_NOPE:(h + 1) * MLA_NOPE].astype(BF16)
        k_ref[:, base + LANE:base + 2 * LANE] = kpe
        v_ref[:, base:base + MLA_V] = kv[:, nk + h * MLA_V:nk + (h + 1) * MLA_V].astype(BF16)
        v_ref[:, base + MLA_V:base + 2 * MLA_V] = jnp.ones((kv.shape[0], MLA_V), BF16)


def _dsa_prep_kernel(qb_ref, kb_ref, vb_ref, qix_ref, kix_ref, wix_ref,
                     dc_ref, dsu_ref, dsd_ref, ic_ref, isu_ref, isd_ref,
                     qb_o, kb_o, vb_o, qix_o, klo_o, khi_o, w_o):
    dc, dsu, dsd = dc_ref[...], dsu_ref[...], dsd_ref[...]
    ic, isu, isd = ic_ref[...], isu_ref[...], isd_ref[...]
    for h in range(DSA_HEADS):
        sl = slice(h * LANE, (h + 1) * LANE)
        qb = _rope(qb_ref[:, sl], dc, dsu, dsd, DSA_ROT // 2)
        qb_o[:, sl] = (qb * (DSA_HEAD_DIM ** -0.5 * LOG2_E)).astype(BF16)
    for h in range(DSA_KV_HEADS):
        sl = slice(h * LANE, (h + 1) * LANE)
        kb_o[:, sl] = _rope(kb_ref[:, sl], dc, dsu, dsd, DSA_ROT // 2).astype(BF16)
    for h in range(DSA_KV_HEADS):
        vb_o[:, 2 * h * LANE:(2 * h + 1) * LANE] = vb_ref[:, h * LANE:(h + 1) * LANE].astype(BF16)
        vb_o[:, (2 * h + 1) * LANE:(2 * h + 2) * LANE] = jnp.ones((vb_ref.shape[0], LANE), BF16)
    for p in range(IDX_HEADS * IDX_DIM // LANE):
        sl = slice(p * LANE, (p + 1) * LANE)
        qix_o[:, sl] = _rope(qix_ref[:, sl], ic, isu, isd, IDX_ROT // 2).astype(BF16)
    kix = _rope(kix_ref[...], ic, isu, isd, IDX_ROT // 2)
    klo_o[...] = kix.astype(BF16)
    khi_o[...] = pltpu.roll(kix, IDX_DIM, 1).astype(BF16)
    w_o[...] = wix_ref[...] * (IDX_HEADS ** -0.5 * IDX_DIM ** -0.5)


def _softmax_steps(scores, values, carries):
    stats = []
    for s, (m, _, _) in zip(scores, carries):
        m_new = jnp.maximum(m, jnp.max(s, axis=1, keepdims=True).astype(F32))
        alpha = jnp.exp2(m - m_new)
        p = jnp.exp2(s - m_new.astype(s.dtype))
        stats.append((m_new, alpha, p.astype(BF16)))
    out = []
    for (m_new, alpha, p), v, (_, l, acc) in zip(stats, values, carries):
        pv = jnp.dot(p, v, preferred_element_type=F32)
        width = acc.shape[1]
        out.append((m_new, alpha * l + pv[:, width:width + 1], alpha * acc + pv[:, :width]))
    return tuple(out)


def _softmax_init(rows, width, chains):
    return tuple((jnp.full((rows, 1), MASK_NEG, F32), jnp.zeros((rows, 1), F32),
                  jnp.zeros((rows, width), F32)) for _ in range(chains))


def _mla_attn_kernel(q_ref, k_ref, v_ref, o_ref, *, tq, heads):
    i = pl.program_id(2)
    qs = [q_ref[:, h * MLA_QK_PAD:(h + 1) * MLA_QK_PAD] for h in range(heads)]

    def chunk(j, carries, diagonal):
        off = pl.multiple_of(j * tq, tq)
        scores = [_nt_dot(qs[h], k_ref[pl.ds(off, tq), h * MLA_QK_PAD:(h + 1) * MLA_QK_PAD],
                          SCORE_DTYPE) for h in range(heads)]
        if diagonal:
            row = lax.broadcasted_iota(jnp.int32, (tq, tq), 0)
            col = lax.broadcasted_iota(jnp.int32, (tq, tq), 1)
            scores = [jnp.where(col <= row, s, MASK_NEG) for s in scores]
        values = [v_ref[pl.ds(off, tq), h * 2 * MLA_V:(h + 1) * 2 * MLA_V] for h in range(heads)]
        return _softmax_steps(scores, values, carries)

    carries = lax.fori_loop(0, i, lambda j, c: chunk(j, c, False),
                            _softmax_init(tq, MLA_V, heads))
    carries = chunk(i, carries, True)
    for h, (_, l, acc) in enumerate(carries):
        o_ref[:, h * MLA_V:(h + 1) * MLA_V] = (acc / l).astype(o_ref.dtype)


def _mla_attention(q, k, v, b, t_pad):
    tq = _pick(t_pad, (384, 128))
    hg = MLA_HEAD_GROUP
    q3 = q.reshape(b, t_pad, MLA_HEADS * MLA_QK_PAD)
    k3 = k.reshape(b, t_pad, MLA_HEADS * MLA_QK_PAD)
    v3 = v.reshape(b, t_pad, MLA_HEADS * 2 * MLA_V)
    out = pl.pallas_call(
        functools.partial(_mla_attn_kernel, tq=tq, heads=hg),
        grid=(b, MLA_HEADS // hg, t_pad // tq),
        in_specs=[pl.BlockSpec((None, tq, hg * MLA_QK_PAD), lambda bb, h, i: (bb, i, h)),
                  pl.BlockSpec((None, t_pad, hg * MLA_QK_PAD), lambda bb, h, i: (bb, 0, h)),
                  pl.BlockSpec((None, t_pad, hg * 2 * MLA_V), lambda bb, h, i: (bb, 0, h))],
        out_specs=pl.BlockSpec((None, tq, hg * MLA_V), lambda bb, h, i: (bb, i, h)),
        out_shape=jax.ShapeDtypeStruct((b, t_pad, MLA_HEADS * MLA_V), BF16),
        compiler_params=_params(("parallel", "parallel", "arbitrary"), VMEM_MATMUL_MIB),
        name="mla_attention",
    )(q3, k3, v3)
    return out.reshape(b * t_pad, MLA_HEADS * MLA_V)


def _dsa_kernel(qb_ref, kb_ref, vb_ref, qix_ref, klo_ref, khi_ref, w_ref, o_ref,
                key_scr, bias_scr, *, tq, tk, topk):
    i = pl.program_id(1)
    n_chunks = ((i + 1) * tq + tk - 1) // tk
    row = i * tq + lax.broadcasted_iota(jnp.int32, (tq, tk), 0)
    col0 = lax.broadcasted_iota(jnp.int32, (tq, tk), 1)
    w = w_ref[...]
    qix = qix_ref[...]

    def score_chunk(c, carry):
        off = pl.multiple_of(c * tk, tk)
        klo = klo_ref[pl.ds(off, tk), :]
        khi = khi_ref[pl.ds(off, tk), :]
        sc = jnp.zeros((tq, tk), F32)
        for h in range(IDX_HEADS):
            qp = qix[:, (h // 2) * LANE:(h // 2 + 1) * LANE]
            d = _nt_dot(qp, klo if h % 2 == 0 else khi)
            sc = sc + w[:, h:h + 1] * jnp.maximum(d, 0.0)
        sc = jnp.where(off + col0 <= row, sc, -jnp.inf)
        bits = pltpu.bitcast(sc, jnp.int32)
        key_scr[:, pl.ds(off, tk)] = bits ^ ((bits >> 31) & jnp.int32(0x7FFFFFFF))
        return carry

    lax.fori_loop(0, n_chunks, score_chunk, 0)

    def bit_step(bi, t):
        cand = t + (jnp.int32(1) << (31 - bi))

        def count_chunk(c, acc):
            off = pl.multiple_of(c * tk, tk)
            ge = jnp.where(key_scr[:, pl.ds(off, tk)] >= cand, 1.0, 0.0).astype(BF16)
            for p in range(tk // LANE):
                acc = acc + ge[:, p * LANE:(p + 1) * LANE]
            return acc

        acc = lax.fori_loop(0, n_chunks, count_chunk, jnp.zeros((tq, LANE), BF16))
        cnt = jnp.sum(acc.astype(F32), axis=1, keepdims=True)
        return jnp.where(cnt >= topk, cand, t)

    thr = lax.fori_loop(0, 32, bit_step, jnp.full((tq, 1), INT32_MIN, jnp.int32))

    def bias_chunk(c, carry):
        off = pl.multiple_of(c * tk, tk)
        ok = (key_scr[:, pl.ds(off, tk)] >= thr) & (off + col0 <= row)
        bias_scr[:, pl.ds(off, tk)] = jnp.where(ok, 0.0, MASK_NEG)
        return carry

    lax.fori_loop(0, n_chunks, bias_chunk, 0)

    rep = DSA_HEADS // DSA_KV_HEADS
    groups = [slice(g * LANE, (g + 1) * LANE) for g in range(DSA_KV_HEADS)]
    for r0 in range(0, tq, Q_BLOCK):
        rows = slice(r0, r0 + Q_BLOCK)
        qgs = [jnp.concatenate(
            [qb_ref[rows, (g * rep + j) * LANE:(g * rep + j + 1) * LANE] for j in range(rep)],
            axis=0) for g in range(DSA_KV_HEADS)]

        def att_chunk(c, carries, rows=rows, qgs=qgs):
            off = pl.multiple_of(c * tk, tk)
            bias = bias_scr[rows, pl.ds(off, tk)].astype(SCORE_DTYPE)
            bias = jnp.concatenate([bias] * rep, axis=0)
            scores = [_nt_dot(qg, kb_ref[pl.ds(off, tk), gsl], SCORE_DTYPE) + bias
                      for qg, gsl in zip(qgs, groups)]
            values = [vb_ref[pl.ds(off, tk), 2 * g * LANE:(2 * g + 2) * LANE]
                      for g in range(DSA_KV_HEADS)]
            return _softmax_steps(scores, values, carries)

        group_chunks = ((i * tq + r0 + Q_BLOCK) + tk - 1) // tk
        carries = lax.fori_loop(0, group_chunks, att_chunk,
                                _softmax_init(rep * Q_BLOCK, DSA_HEAD_DIM, DSA_KV_HEADS))
        for g, (_, l, acc) in enumerate(carries):
            o = acc / l
            for j in range(rep):
                hq = g * rep + j
                o_ref[rows, hq * LANE:(hq + 1) * LANE] = (
                    o[j * Q_BLOCK:(j + 1) * Q_BLOCK].astype(o_ref.dtype))


def _dsa_attention(qb, kb, vb, qix, klo, khi, w, b, t_pad, topk):
    tq = _pick(t_pad, (384, 128))
    tk = _pick(t_pad, (384, 128))
    assert t_pad // LANE <= 256, "bf16 partial counts must stay exact"
    kvw = DSA_KV_HEADS * DSA_HEAD_DIM
    r3 = lambda a: a.reshape(b, t_pad, a.shape[-1])
    qspec = lambda wd: pl.BlockSpec((None, tq, wd), lambda bb, i: (bb, i, 0))
    kspec = lambda wd: pl.BlockSpec((None, t_pad, wd), lambda bb, i: (bb, 0, 0),
                                    pipeline_mode=pl.Buffered(1))
    out = pl.pallas_call(
        functools.partial(_dsa_kernel, tq=tq, tk=tk, topk=topk),
        grid=(b, t_pad // tq),
        in_specs=[qspec(DSA_HEADS * DSA_HEAD_DIM), kspec(kvw), kspec(2 * kvw),
                  qspec(IDX_HEADS * IDX_DIM), kspec(LANE), kspec(LANE), qspec(LANE)],
        out_specs=qspec(DSA_HEADS * DSA_HEAD_DIM),
        out_shape=jax.ShapeDtypeStruct((b, t_pad, DSA_HEADS * DSA_HEAD_DIM), BF16),
        scratch_shapes=[pltpu.VMEM((tq, t_pad), jnp.int32), pltpu.VMEM((tq, t_pad), F32)],
        compiler_params=_params(("parallel", "arbitrary"), VMEM_RESIDENT_MIB),
        name="dsa_attention",
    )(r3(qb), r3(kb), r3(vb), r3(qix), r3(klo), r3(khi), r3(w))
    return out.reshape(b * t_pad, DSA_HEADS * DSA_HEAD_DIM)


def _sigmoid(x):
    return 1.0 / (1.0 + jnp.exp(-x))


def _branch_kernel(oa_ref, ob_ref, wa_ref, wb_ref, ga_ref, gb_ref, o_ref):
    ya = jnp.dot(oa_ref[...], wa_ref[...], preferred_element_type=F32)
    yb = jnp.dot(ob_ref[...], wb_ref[...], preferred_element_type=F32)
    o_ref[...] = (_sigmoid(ga_ref[...]) * ya + _sigmoid(gb_ref[...]) * yb).astype(o_ref.dtype)


def _residual_mm_kernel(a_ref, b_ref, h_ref, o_ref):
    o_ref[...] = h_ref[...] + jnp.dot(a_ref[...], b_ref[...], preferred_element_type=F32)


def _pack_rows(rows):
    assert len(rows) == SUBLANE
    shape = (SUBLANE, rows[0].shape[1])
    sub = lax.broadcasted_iota(jnp.int32, shape, 0)
    out = jnp.broadcast_to(rows[-1], shape)
    for k in range(SUBLANE - 2, -1, -1):
        out = jnp.where(sub == k, rows[k], out)
    return out


def _peer_select_kernel(qt_ref, keys_ref, lim_ref, rank_ref, e1_ref, e2_ref):
    neg_inf = -jnp.inf
    for h in range(PEER_HEADS):
        scores, tops = [], []
        rank2 = None
        for c in range(2):
            sl = slice((2 * h + c) * PEER_HALF, (2 * h + c + 1) * PEER_HALF)
            s = jnp.dot(keys_ref[sl, :], qt_ref[sl, :].astype(BF16), preferred_element_type=F32)
            scores.append(s)
            vals, x = [], s
            if c == 1:
                rank2 = jnp.full(s.shape, NOT_RANKED, F32)
            for k in range(PEER_TOPK):
                m = jnp.max(x, axis=0, keepdims=True)
                vals.append(m)
                found = x == m
                if c == 1:
                    rank2 = jnp.where(found, float(k), rank2)
                x = jnp.where(found, neg_inf, x)
            tops.append(vals)
        a, bv = tops
        b_packs = [_pack_rows(bv[k:k + SUBLANE]) for k in range(0, PEER_TOPK, SUBLANE)]
        sub = lax.broadcasted_iota(jnp.int32, b_packs[0].shape, 0)
        cands, owners, single = [], [], []
        for i in range(PEER_TOPK):
            partners = PEER_TOPK // (i + 1)
            if partners == 1:
                single.append(i)
                continue
            for k in range(0, partners, SUBLANE):
                cd = a[i] + b_packs[k // SUBLANE]
                if partners - k < SUBLANE:
                    cd = jnp.where(sub < partners - k, cd, neg_inf)
                cands.append(cd)
                owners.append([i] * SUBLANE)
        assert len(single) % SUBLANE == 0
        for k in range(0, len(single), SUBLANE):
            ids = single[k:k + SUBLANE]
            cands.append(_pack_rows([a[i] for i in ids]) + bv[0])
            owners.append(ids)
        originals = list(cands)
        v0 = a[0] + bv[0]
        z = jnp.zeros_like(v0)
        tau = v0
        for _ in range(PEER_TOPK):
            tau = jnp.max(functools.reduce(jnp.maximum, cands), axis=0, keepdims=True)
            z = z + jnp.exp(tau - v0)
            cands = [jnp.where(cd == tau, neg_inf, cd) for cd in cands]
        partners = [None] * PEER_TOPK
        for cd, ids in zip(originals, owners):
            reached = jnp.where(cd >= tau, 1.0, 0.0)
            if len(set(ids)) == 1:
                cnt = jnp.sum(reached, axis=0, keepdims=True)
                partners[ids[0]] = cnt if partners[ids[0]] is None else partners[ids[0]] + cnt
            else:
                for k, i in enumerate(ids):
                    partners[i] = jnp.sum(jnp.where(sub == k, reached, 0.0), axis=0, keepdims=True)
        lim = jnp.full(scores[0].shape, -1.0, F32)
        for i in range(PEER_TOPK):
            lim = jnp.where(scores[0] == a[i], partners[i] - 1.0, lim)
        hs = slice(h * PEER_N_KEYS, (h + 1) * PEER_N_KEYS)
        lim_ref[hs, :] = lim
        rank_ref[hs, :] = rank2.astype(rank_ref.dtype)
        e1_ref[hs, :] = jnp.exp(scores[0] - a[0])
        e2_ref[hs, :] = (jnp.exp(scores[1] - bv[0]) / z).astype(e2_ref.dtype)


def _peer_select(qt_arr, keys):
    rows, n = qt_arr.shape
    tm = _pick(n, (256, 128))
    hk = PEER_HEADS * PEER_N_KEYS
    big = lambda: pl.BlockSpec((hk, tm), lambda i: (0, i))
    return pl.pallas_call(
        _peer_select_kernel,
        grid=(n // tm,),
        in_specs=[pl.BlockSpec((rows, tm), lambda i: (0, i)),
                  pl.BlockSpec(keys.shape, lambda i: (0, 0))],
        out_specs=[big(), big(), big(), big()],
        out_shape=[jax.ShapeDtypeStruct((hk, n), F32), jax.ShapeDtypeStruct((hk, n), BF16),
                   jax.ShapeDtypeStruct((hk, n), F32), jax.ShapeDtypeStruct((hk, n), BF16)],
        compiler_params=_params(("parallel",), VMEM_ROWWISE_MIB),
        name="peer_select",
    )(qt_arr, keys)


def _gelu_tanh(x):
    c = np.float32(np.sqrt(2.0 / np.pi))
    return x * (0.5 * (1.0 + jnp.tanh(c * (x + 0.044715 * (x * x * x)))))


def _peer_main_kernel(hn_ref, u_ref, vt_ref, lim_ref, rank_ref, e1_ref, e2_ref, o_ref,
                      raw_scr, a_scr, *, rows_per_tile, n_tiles):
    j = pl.program_id(1)
    last_slot = (n_tiles - 1) % 2

    def first_matmul(slot):
        raw_scr[slot] = jnp.dot(u_ref[...], hn_ref[...], preferred_element_type=F32)

    def gate_and_project(slot, init):
        tm = o_ref.shape[1]
        for rr in range(rows_per_tile):
            r = (j - 1) * rows_per_tile + rr
            lim_rows = [jnp.broadcast_to(lim_ref[pl.ds(h * PEER_N_KEYS + r, 1), :],
                                         (GATE_ROWS, tm)).astype(BF16) for h in range(PEER_HEADS)]
            e1_rows = [jnp.broadcast_to(e1_ref[pl.ds(h * PEER_N_KEYS + r, 1), :],
                                        (GATE_ROWS, tm)).astype(BF16) for h in range(PEER_HEADS)]
            for c0 in range(0, PEER_N_KEYS, GATE_ROWS):
                gate = None
                for h in range(PEER_HEADS):
                    cs = slice(h * PEER_N_KEYS + c0, h * PEER_N_KEYS + c0 + GATE_ROWS)
                    hit = rank_ref[cs, :] <= lim_rows[h]
                    term = jnp.where(hit, e2_ref[cs, :] * e1_rows[h], 0.0)
                    gate = term if gate is None else gate + term
                es = slice(rr * PEER_N_KEYS + c0, rr * PEER_N_KEYS + c0 + GATE_ROWS)
                act = _gelu_tanh(raw_scr[slot, es, :]) * gate.astype(F32)
                a_scr[es, :] = act.astype(BF16)
        contrib = jnp.dot(vt_ref[...], a_scr[...], preferred_element_type=F32)
        o_ref[...] = contrib if init else o_ref[...] + contrib

    @pl.when(j == 0)
    def _():
        first_matmul(0)

    @pl.when(j == 1)
    def _():
        first_matmul(1)
        gate_and_project(0, init=True)

    @pl.when((j > 1) & (j < n_tiles) & (j % 2 == 0))
    def _():
        first_matmul(0)
        gate_and_project(1, init=False)

    @pl.when((j > 1) & (j < n_tiles) & (j % 2 == 1))
    def _():
        first_matmul(1)
        gate_and_project(0, init=False)

    @pl.when(j == n_tiles)
    def _():
        gate_and_project(last_slot, init=False)


def _peer_main(hn_t, u, vt, lim, rank2, e1, e2):
    d, n = hn_t.shape
    n_exp = u.shape[0]
    tm = _pick(n, ROW_TILES)
    te = PEER_EXPERT_TILE
    hk = PEER_HEADS * PEER_N_KEYS
    tok = lambda rows: pl.BlockSpec((rows, tm), lambda i, j: (0, i), pipeline_mode=pl.Buffered(1))
    n_tiles = n_exp // te
    assert n_tiles >= 2
    return pl.pallas_call(
        functools.partial(_peer_main_kernel, rows_per_tile=te // PEER_N_KEYS, n_tiles=n_tiles),
        grid=(n // tm, n_tiles + 1),
        in_specs=[tok(d),
                  pl.BlockSpec((te, d), lambda i, j: (jnp.minimum(j, n_tiles - 1), 0)),
                  pl.BlockSpec((d, te), lambda i, j: (0, jnp.maximum(j - 1, 0))),
                  tok(hk), tok(hk), tok(hk), tok(hk)],
        out_specs=pl.BlockSpec((d, tm), lambda i, j: (0, i)),
        out_shape=jax.ShapeDtypeStruct((d, n), F32),
        scratch_shapes=[pltpu.VMEM((2, te, tm), F32), pltpu.VMEM((te, tm), BF16)],
        compiler_params=_params(("parallel", "arbitrary"), VMEM_RESIDENT_MIB),
        name="peer_main",
    )(hn_t, u, vt, lim, rank2, e1, e2)


def _rmsnorm_t_kernel(x_ref, g_ref, o_ref):
    o_ref[...] = _rms(x_ref[...], g_ref[...]).T.astype(o_ref.dtype)


def _final_kernel(h_ref, p_ref, g_ref, o_ref):
    o_ref[...] = _rms(h_ref[...] + p_ref[...].T, g_ref[...])


def _in_proj_layout(d):
    segs = [("gate_a", d), ("gate_b", d), ("q_b", DSA_Q_WIDTH), ("c_q", MLA_Q_LORA),
            ("q_ix", IDX_Q_WIDTH), ("c_kv", MLA_KV_LORA), ("k_b", DSA_KV_WIDTH),
            ("v_b", DSA_KV_WIDTH), ("k_pe", LANE), ("k_ix", LANE), ("w_ix", LANE)]
    segs.sort(key=lambda s: -s[1])
    offs, o = {}, 0
    for name, wd in segs:
        assert o % wd == 0
        offs[name] = (o, wd)
        o += wd
    total = -(-o // COL_TILES[0]) * COL_TILES[0]
    return offs, total


def _build_w_in(w_in, d):
    src = {}
    o = 0
    for name, wd in (("c_q", MLA_Q_LORA), ("c_kv", MLA_KV_LORA), ("k_pe", MLA_ROPE),
                     ("q_b", DSA_Q_WIDTH), ("k_b", DSA_KV_WIDTH), ("v_b", DSA_KV_WIDTH),
                     ("q_ix", IDX_Q_WIDTH),
                     ("k_ix", IDX_DIM), ("w_ix", IDX_HEADS), ("gate_a", d), ("gate_b", d)):
        src[name] = w_in[:, o:o + wd]
        o += wd
    offs, total = _in_proj_layout(d)
    cols = []
    o = 0
    for name, (off, wd) in sorted(offs.items(), key=lambda kv: kv[1][0]):
        blk = src[name]
        cols.append(jnp.pad(blk, ((0, 0), (0, wd - blk.shape[1]))))
        o = off + wd
    cols.append(jnp.zeros((d, total - o), w_in.dtype))
    return jnp.concatenate(cols, axis=1).astype(BF16), offs


def kernel(x, meta_tokens, attn_norm_g, w_in, q_norm_g, w_uq, kv_norm_g, w_ukv, w_branch, w_out,
           ffn_norm_g, peer_w_q, peer_sub_keys, peer_u, peer_v, final_norm_g):
    b, s, d = x.shape
    assert attn_norm_g.shape[0] == 1, "single-layer block"
    assert s % Q_BLOCK == 0 and d % COL_TILES[1] == 0 and (d & (d - 1)) == 0
    assert N_META % SUBLANE == 0
    t = s + N_META
    t_pad = -(-t // Q_BLOCK) * Q_BLOCK
    topk = min(DSA_TOPK_MAX, s // 4)
    n = b * t_pad

    meta = jnp.broadcast_to(meta_tokens.astype(x.dtype)[None], (b, N_META, d))
    h0 = jnp.concatenate([meta, x, jnp.zeros((b, t_pad - t, d), x.dtype)], axis=1).reshape(n, d)

    w_in_p, offs = _build_w_in(w_in[0], d)
    wq = w_uq[0].reshape(MLA_Q_LORA, MLA_HEADS, MLA_NOPE + MLA_ROPE)
    wq = jnp.pad(wq, ((0, 0), (0, 0), (0, MLA_QK_PAD - MLA_NOPE - MLA_ROPE)))
    wq = wq.reshape(MLA_Q_LORA, MLA_HEADS * MLA_QK_PAD).astype(BF16)
    wkv = w_ukv[0].reshape(MLA_KV_LORA, MLA_HEADS, MLA_NOPE + MLA_V)
    wkv = jnp.concatenate([wkv[:, :, :MLA_NOPE].reshape(MLA_KV_LORA, -1),
                           wkv[:, :, MLA_NOPE:].reshape(MLA_KV_LORA, -1)], axis=1).astype(BF16)
    w_br = w_branch[0].astype(BF16)
    w_o = w_out[0].astype(BF16)
    w_pq_t = peer_w_q[0].T.astype(BF16)
    keys = peer_sub_keys[0].reshape(PEER_HEADS * 2 * PEER_N_KEYS, PEER_HALF).astype(BF16)
    u_tab = peer_u[0].astype(BF16)
    v_tab_t = peer_v[0].T.astype(BF16)

    mla_tabs = _rope_tables(t_pad, LANE, MLA_ROPE)
    dsa_tabs = _rope_tables(t_pad, LANE, DSA_ROT)
    idx_tabs = _rope_tables(t_pad, IDX_DIM, IDX_ROT)

    hn = _rmsnorm(h0, attn_norm_g[0], BF16)
    proj = _matmul(hn, w_in_p, F32, "in_proj")

    tm = _pick(t_pad, (384, 128))
    n_pos = t_pad // tm
    row = lambda wd, off: pl.BlockSpec((tm, wd), lambda i, o=off // wd: (i, o))
    seg = lambda name: row(offs[name][1], offs[name][0])
    tab = pl.BlockSpec((tm, LANE), lambda i: (i % n_pos, 0))
    full = lambda a: pl.BlockSpec(a.shape, lambda i: (0,) * a.ndim)
    out_row = lambda wd: pl.BlockSpec((tm, wd), lambda i: (i, 0))
    sds = lambda wd, dt: jax.ShapeDtypeStruct((n, wd), dt)

    qg = q_norm_g[0].reshape(1, -1).astype(F32)
    q_a = pl.pallas_call(
        _mla_q_kernel, grid=(n // tm,),
        in_specs=[seg("c_q"), full(qg), full(wq), tab, tab, tab],
        out_specs=out_row(MLA_HEADS * MLA_QK_PAD),
        out_shape=sds(MLA_HEADS * MLA_QK_PAD, BF16),
        compiler_params=_params(("parallel",), VMEM_MATMUL_MIB), name="mla_q_prep",
    )(proj, qg, wq, *mla_tabs)

    kg = kv_norm_g[0].reshape(1, -1).astype(F32)
    k_a, v_a = pl.pallas_call(
        _mla_kv_kernel, grid=(n // tm,),
        in_specs=[seg("c_kv"), full(kg), full(wkv), seg("k_pe"), tab, tab, tab],
        out_specs=[out_row(MLA_HEADS * MLA_QK_PAD), out_row(MLA_HEADS * 2 * MLA_V)],
        out_shape=[sds(MLA_HEADS * MLA_QK_PAD, BF16), sds(MLA_HEADS * 2 * MLA_V, BF16)],
        compiler_params=_params(("parallel",), VMEM_MATMUL_MIB), name="mla_kv_prep",
    )(proj, kg, wkv, proj, *mla_tabs)

    kvw = DSA_KV_HEADS * DSA_HEAD_DIM
    q_b, k_b, v_b, q_ix, k_lo, k_hi, w_ix = pl.pallas_call(
        _dsa_prep_kernel, grid=(n // tm,),
        in_specs=[seg("q_b"), seg("k_b"), seg("v_b"), seg("q_ix"), seg("k_ix"), seg("w_ix")]
        + [tab] * 6,
        out_specs=[out_row(DSA_Q_WIDTH), out_row(kvw), out_row(2 * kvw), out_row(IDX_Q_WIDTH),
                   out_row(LANE), out_row(LANE), out_row(LANE)],
        out_shape=[sds(DSA_Q_WIDTH, BF16), sds(kvw, BF16), sds(2 * kvw, BF16),
                   sds(IDX_Q_WIDTH, BF16),
                   sds(LANE, BF16), sds(LANE, BF16), sds(LANE, F32)],
        compiler_params=_params(("parallel",), VMEM_MATMUL_MIB), name="dsa_prep",
    )(proj, proj, proj, proj, proj, proj, *dsa_tabs, *idx_tabs)

    o_a = _mla_attention(q_a, k_a, v_a, b, t_pad)
    o_b = _dsa_attention(q_b, k_b, v_b, q_ix, k_lo, k_hi, w_ix, b, t_pad, topk)

    tmm = _pick(n, ROW_TILES)
    tn = _pick(d, COL_TILES[:2])
    ga_off, gb_off = offs["gate_a"][0] // tn, offs["gate_b"][0] // tn
    bw = MLA_HEADS * MLA_V
    merged = pl.pallas_call(
        _branch_kernel, grid=(d // tn, n // tmm),
        in_specs=[pl.BlockSpec((tmm, bw), lambda j, i: (i, 0)),
                  pl.BlockSpec((tmm, bw), lambda j, i: (i, 0)),
                  pl.BlockSpec((bw, tn), lambda j, i: (0, j)),
                  pl.BlockSpec((bw, tn), lambda j, i: (1, j)),
                  pl.BlockSpec((tmm, tn), lambda j, i: (i, ga_off + j)),
                  pl.BlockSpec((tmm, tn), lambda j, i: (i, gb_off + j))],
        out_specs=pl.BlockSpec((tmm, tn), lambda j, i: (i, j)),
        out_shape=jax.ShapeDtypeStruct((n, d), BF16),
        compiler_params=_params(("parallel", "parallel"), VMEM_MATMUL_MIB), name="branch_merge",
    )(o_a, o_b, w_br, w_br, proj, proj)

    h1 = pl.pallas_call(
        _residual_mm_kernel, grid=(d // tn, n // tmm),
        in_specs=[pl.BlockSpec((tmm, d), lambda j, i: (i, 0)),
                  pl.BlockSpec((d, tn), lambda j, i: (0, j)),
                  pl.BlockSpec((tmm, tn), lambda j, i: (i, j))],
        out_specs=pl.BlockSpec((tmm, tn), lambda j, i: (i, j)),
        out_shape=jax.ShapeDtypeStruct((n, d), F32),
        compiler_params=_params(("parallel", "parallel"), VMEM_MATMUL_MIB), name="out_proj",
    )(merged, w_o, h0)

    np_tok = b * s
    tf = _pick(s, (256, 128))
    tiles_per_seq = s // tf
    real_rows = pl.BlockSpec((pl.Element(tf), pl.Element(d)),
                             lambda bb, t: (pl.multiple_of(bb * t_pad + N_META + t * tf, SUBLANE), 0))
    vec = pl.BlockSpec((1, d), lambda bb, t: (0, 0))

    hn2_t = pl.pallas_call(
        _rmsnorm_t_kernel, grid=(b, tiles_per_seq),
        in_specs=[real_rows, vec],
        out_specs=pl.BlockSpec((d, tf), lambda bb, t: (0, bb * tiles_per_seq + t)),
        out_shape=jax.ShapeDtypeStruct((d, np_tok), BF16),
        compiler_params=_params(("parallel", "parallel"), VMEM_ROWWISE_MIB), name="ffn_norm_t",
    )(h1, ffn_norm_g[0].reshape(1, d).astype(F32))
    q_t = _matmul(w_pq_t, hn2_t, F32, "peer_query")
    lim, rank2, e1, e2 = _peer_select(q_t, keys)
    peer_t = _peer_main(hn2_t, u_tab, v_tab_t, lim, rank2, e1, e2)

    out = pl.pallas_call(
        _final_kernel, grid=(b, tiles_per_seq),
        in_specs=[real_rows,
                  pl.BlockSpec((d, tf), lambda bb, t: (0, bb * tiles_per_seq + t)), vec],
        out_specs=pl.BlockSpec((tf, d), lambda bb, t: (bb * tiles_per_seq + t, 0)),
        out_shape=jax.ShapeDtypeStruct((np_tok, d), F32),
        compiler_params=_params(("parallel", "parallel"), VMEM_ROWWISE_MIB), name="final_norm",
    )(h1, peer_t, final_norm_g.reshape(1, d).astype(F32))
    return out.reshape(b, s, d)
```

```python
import functools

import jax
import jax.numpy as jnp
import numpy as np
from jax import lax
from jax.experimental import pallas as pl
from jax.experimental.pallas import tpu as pltpu

N_META = 16
ROPE_THETA = 500000.0
Q_BLOCK = 128
NORM_EPS = 1e-6
MLA_HEADS = 16
MLA_Q_LORA = 1024
MLA_KV_LORA = 512
MLA_NOPE = 128
MLA_ROPE = 64
MLA_V = 128
DSA_HEADS = 16
DSA_KV_HEADS = 4
DSA_HEAD_DIM = 128
DSA_ROT = DSA_HEAD_DIM // 4
IDX_HEADS = 16
IDX_DIM = 64
IDX_ROT = IDX_DIM // 4
DSA_TOPK_MAX = 256
PEER_HEADS = 8
PEER_N_KEYS = 128
PEER_HALF = 128
PEER_TOPK = 16
DSA_Q_WIDTH = DSA_HEADS * DSA_HEAD_DIM
DSA_KV_WIDTH = DSA_KV_HEADS * DSA_HEAD_DIM
IDX_Q_WIDTH = IDX_HEADS * IDX_DIM

LANE = 128
SUBLANE = 8
MXU_WIDTH = 256
GATE_ROWS = 16
MLA_HEAD_GROUP = 4
LOG2_E = 1.4426950408889634
SCORE_DTYPE = jnp.bfloat16
NOT_RANKED = 127.0
MLA_QK_PAD = 2 * LANE
MASK_NEG = -1e30
INT32_MIN = -(2 ** 31)
MIB = 1024 * 1024
VMEM_ROWWISE_MIB = 40
VMEM_MATMUL_MIB = 48
VMEM_RESIDENT_MIB = 58
ROW_TILES = (512, 256, 128)
COL_TILES = (1024, 512, 256, 128)
PEER_EXPERT_TILE = 512

BF16 = jnp.bfloat16
F32 = jnp.float32


def _pick(n, prefs):
    for p in prefs:
        if n % p == 0:
            return p
    raise ValueError(f"no tile in {prefs} divides {n}")


def _params(sem, vmem_mib):
    return pltpu.CompilerParams(dimension_semantics=sem, vmem_limit_bytes=vmem_mib * MIB)


def _nt_dot(a, b, out_dtype=F32):
    out = lax.dot_general(a, b, (((1,), (1,)), ((), ())), preferred_element_type=F32)
    return out.astype(out_dtype)


def _rms(x, g):
    ms = jnp.mean(x * x, axis=-1, keepdims=True)
    return x * lax.rsqrt(ms + NORM_EPS) * g


def _rope(x, c, s_up, s_dn, half):
    return x * c + pltpu.roll(x, half, 1) * s_up + pltpu.roll(x, LANE - half, 1) * s_dn


def _rope_tables(t_pad, block, rot):
    half = rot // 2
    inv = ROPE_THETA ** (-jnp.arange(0, rot, 2, dtype=F32) / rot)
    ang = jnp.arange(t_pad, dtype=F32)[:, None] * inv[None, :]
    cos, sin = jnp.cos(ang), jnp.sin(ang)
    lane = np.arange(LANE) % block
    lo = lane < half
    hi = (lane >= half) & (lane < rot)
    idx = np.where(lo, lane, np.where(hi, lane - half, 0))
    cos_l, sin_l = cos[:, idx], sin[:, idx]
    c = jnp.where(lo | hi, cos_l, 1.0)
    s_up = jnp.where(hi, sin_l, 0.0)
    s_dn = jnp.where(lo, -sin_l, 0.0)
    return c.astype(F32), s_up.astype(F32), s_dn.astype(F32)


def _rmsnorm_kernel(x_ref, g_ref, o_ref):
    o_ref[...] = _rms(x_ref[...].astype(F32), g_ref[...]).astype(o_ref.dtype)


def _rmsnorm(x, g, out_dtype):
    m, d = x.shape
    tm = _pick(m, (256, 128))
    return pl.pallas_call(
        _rmsnorm_kernel,
        grid=(m // tm,),
        in_specs=[pl.BlockSpec((tm, d), lambda i: (i, 0)),
                  pl.BlockSpec((1, d), lambda i: (0, 0))],
        out_specs=pl.BlockSpec((tm, d), lambda i: (i, 0)),
        out_shape=jax.ShapeDtypeStruct((m, d), out_dtype),
        compiler_params=_params(("parallel",), VMEM_ROWWISE_MIB),
        name="rmsnorm",
    )(x, g.reshape(1, d).astype(F32))


def _mm_kernel(a_ref, b_ref, o_ref):
    o_ref[...] = jnp.dot(a_ref[...], b_ref[...], preferred_element_type=F32).astype(o_ref.dtype)


def _matmul(a, b, out_dtype, name, col_tiles=COL_TILES, vmem_mib=VMEM_MATMUL_MIB):
    m, k = a.shape
    _, n = b.shape
    tm = _pick(m, ROW_TILES)
    tn = _pick(n, col_tiles)
    return pl.pallas_call(
        _mm_kernel,
        grid=(n // tn, m // tm),
        in_specs=[pl.BlockSpec((tm, k), lambda j, i: (i, 0)),
                  pl.BlockSpec((k, tn), lambda j, i: (0, j))],
        out_specs=pl.BlockSpec((tm, tn), lambda j, i: (i, j)),
        out_shape=jax.ShapeDtypeStruct((m, n), out_dtype),
        compiler_params=_params(("parallel", "parallel"), vmem_mib),
        name=name,
    )(a, b)


def _mla_q_kernel(cq_ref, g_ref, w_ref, c_ref, su_ref, sd_ref, o_ref):
    xn = _rms(cq_ref[...], g_ref[...]).astype(BF16)
    q = jnp.dot(xn, w_ref[...], preferred_element_type=F32)
    q = q * ((MLA_NOPE + MLA_ROPE) ** -0.5 * LOG2_E)
    c, su, sd = c_ref[...], su_ref[...], sd_ref[...]
    for h in range(MLA_HEADS):
        base = h * MLA_QK_PAD
        o_ref[:, base:base + LANE] = q[:, base:base + LANE].astype(BF16)
        o_ref[:, base + LANE:base + 2 * LANE] = _rope(
            q[:, base + LANE:base + 2 * LANE], c, su, sd, MLA_ROPE // 2).astype(BF16)


def _mla_kv_kernel(ckv_ref, g_ref, w_ref, kpe_ref, c_ref, su_ref, sd_ref, k_ref, v_ref):
    xn = _rms(ckv_ref[...], g_ref[...]).astype(BF16)
    kv = jnp.dot(xn, w_ref[...], preferred_element_type=F32)
    kpe = _rope(kpe_ref[...], c_ref[...], su_ref[...], sd_ref[...], MLA_ROPE // 2).astype(BF16)
    nk = MLA_HEADS * MLA_NOPE
    for h in range(MLA_HEADS):
        base = h * MLA_QK_PAD
        k_ref[:, base:base + LANE] = kv[:, h * MLA_NOPE:(h + 1) * MLA_NOPE].astype(BF16)
        k_ref[:, base + LANE:base + 2 * LANE] = kpe
        v_ref[:, base:base + MLA_V] = kv[:, nk + h * MLA_V:nk + (h + 1) * MLA_V].astype(BF16)
        v_ref[:, base + MLA_V:base + 2 * MLA_V] = jnp.ones((kv.shape[0], MLA_V), BF16)


def _dsa_prep_kernel(qb_ref, kb_ref, vb_ref, qix_ref, kix_ref, wix_ref,
                     dc_ref, dsu_ref, dsd_ref, ic_ref, isu_ref, isd_ref,
                     qb_o, kb_o, vb_o, qix_o, klo_o, khi_o, w_o):
    dc, dsu, dsd = dc_ref[...], dsu_ref[...], dsd_ref[...]
    ic, isu, isd = ic_ref[...], isu_ref[...], isd_ref[...]
    for h in range(DSA_HEADS):
        sl = slice(h * LANE, (h + 1) * LANE)
        qb = _rope(qb_ref[:, sl], dc, dsu, dsd, DSA_ROT // 2)
        qb_o[:, sl] = (qb * (DSA_HEAD_DIM ** -0.5 * LOG2_E)).astype(BF16)
    for h in range(DSA_KV_HEADS):
        sl = slice(h * LANE, (h + 1) * LANE)
        kb_o[:, sl] = _rope(kb_ref[:, sl], dc, dsu, dsd, DSA_ROT // 2).astype(BF16)
    for h in range(DSA_KV_HEADS):
        vb_o[:, 2 * h * LANE:(2 * h + 1) * LANE] = vb_ref[:, h * LANE:(h + 1) * LANE].astype(BF16)
        vb_o[:, (2 * h + 1) * LANE:(2 * h + 2) * LANE] = jnp.ones((vb_ref.shape[0], LANE), BF16)
    for p in range(IDX_HEADS * IDX_DIM // LANE):
        sl = slice(p * LANE, (p + 1) * LANE)
        qix_o[:, sl] = _rope(qix_ref[:, sl], ic, isu, isd, IDX_ROT // 2).astype(BF16)
    kix = _rope(kix_ref[...], ic, isu, isd, IDX_ROT // 2)
    klo_o[...] = kix.astype(BF16)
    khi_o[...] = pltpu.roll(kix, IDX_DIM, 1).astype(BF16)
    w_o[...] = wix_ref[...] * (IDX_HEADS ** -0.5 * IDX_DIM ** -0.5)


def _softmax_steps(scores, values, carries):
    stats = []
    for s, (m, _, _) in zip(scores, carries):
        m_new = jnp.maximum(m, jnp.max(s, axis=1, keepdims=True).astype(F32))
        alpha = jnp.exp2(m - m_new)
        p = jnp.exp2(s - m_new.astype(s.dtype))
        stats.append((m_new, alpha, p.astype(BF16)))
    out = []
    for (m_new, alpha, p), v, (_, l, acc) in zip(stats, values, carries):
        pv = jnp.dot(p, v, preferred_element_type=F32)
        width = acc.shape[1]
        out.append((m_new, alpha * l + pv[:, width:width + 1], alpha * acc + pv[:, :width]))
    return tuple(out)


def _softmax_init(rows, width, chains):
    return tuple((jnp.full((rows, 1), MASK_NEG, F32), jnp.zeros((rows, 1), F32),
                  jnp.zeros((rows, width), F32)) for _ in range(chains))


def _mla_attn_kernel(q_ref, k_ref, v_ref, o_ref, *, tq, heads):
    i = pl.program_id(2)
    qs = [q_ref[:, h * MLA_QK_PAD:(h + 1) * MLA_QK_PAD] for h in range(heads)]

    def chunk(j, carries, diagonal):
        off = pl.multiple_of(j * tq, tq)
        scores = [_nt_dot(qs[h], k_ref[pl.ds(off, tq), h * MLA_QK_PAD:(h + 1) * MLA_QK_PAD],
                          SCORE_DTYPE) for h in range(heads)]
        if diagonal:
            row = lax.broadcasted_iota(jnp.int32, (tq, tq), 0)
            col = lax.broadcasted_iota(jnp.int32, (tq, tq), 1)
            scores = [jnp.where(col <= row, s, MASK_NEG) for s in scores]
        values = [v_ref[pl.ds(off, tq), h * 2 * MLA_V:(h + 1) * 2 * MLA_V] for h in range(heads)]
        return _softmax_steps(scores, values, carries)

    carries = lax.fori_loop(0, i, lambda j, c: chunk(j, c, False),
                            _softmax_init(tq, MLA_V, heads))
    carries = chunk(i, carries, True)
    for h, (_, l, acc) in enumerate(carries):
        o_ref[:, h * MLA_V:(h + 1) * MLA_V] = (acc / l).astype(o_ref.dtype)


def _mla_attention(q, k, v, b, t_pad):
    tq = _pick(t_pad, (384, 128))
    hg = MLA_HEAD_GROUP
    q3 = q.reshape(b, t_pad, MLA_HEADS * MLA_QK_PAD)
    k3 = k.reshape(b, t_pad, MLA_HEADS * MLA_QK_PAD)
    v3 = v.reshape(b, t_pad, MLA_HEADS * 2 * MLA_V)
    out = pl.pallas_call(
        functools.partial(_mla_attn_kernel, tq=tq, heads=hg),
        grid=(b, MLA_HEADS // hg, t_pad // tq),
        in_specs=[pl.BlockSpec((None, tq, hg * MLA_QK_PAD), lambda bb, h, i: (bb, i, h)),
                  pl.BlockSpec((None, t_pad, hg * MLA_QK_PAD), lambda bb, h, i: (bb, 0, h)),
                  pl.BlockSpec((None, t_pad, hg * 2 * MLA_V), lambda bb, h, i: (bb, 0, h))],
        out_specs=pl.BlockSpec((None, tq, hg * MLA_V), lambda bb, h, i: (bb, i, h)),
        out_shape=jax.ShapeDtypeStruct((b, t_pad, MLA_HEADS * MLA_V), BF16),
        compiler_params=_params(("parallel", "parallel", "arbitrary"), VMEM_MATMUL_MIB),
        name="mla_attention",
    )(q3, k3, v3)
    return out.reshape(b * t_pad, MLA_HEADS * MLA_V)


def _dsa_kernel(qb_ref, kb_ref, vb_ref, qix_ref, klo_ref, khi_ref, w_ref, o_ref,
                key_scr, bias_scr, *, tq, tk, topk):
    i = pl.program_id(1)
    n_chunks = ((i + 1) * tq + tk - 1) // tk
    row = i * tq + lax.broadcasted_iota(jnp.int32, (tq, tk), 0)
    col0 = lax.broadcasted_iota(jnp.int32, (tq, tk), 1)
    w = w_ref[...]
    qix = qix_ref[...]

    def score_chunk(c, carry):
        off = pl.multiple_of(c * tk, tk)
        klo = klo_ref[pl.ds(off, tk), :]
        khi = khi_ref[pl.ds(off, tk), :]
        sc = jnp.zeros((tq, tk), F32)
        for h in range(IDX_HEADS):
            qp = qix[:, (h // 2) * LANE:(h // 2 + 1) * LANE]
            d = _nt_dot(qp, klo if h % 2 == 0 else khi)
            sc = sc + w[:, h:h + 1] * jnp.maximum(d, 0.0)
        sc = jnp.where(off + col0 <= row, sc, -jnp.inf)
        bits = pltpu.bitcast(sc, jnp.int32)
        key_scr[:, pl.ds(off, tk)] = bits ^ ((bits >> 31) & jnp.int32(0x7FFFFFFF))
        return carry

    lax.fori_loop(0, n_chunks, score_chunk, 0)

    def bit_step(bi, t):
        cand = t + (jnp.int32(1) << (31 - bi))

        def count_chunk(c, acc):
            off = pl.multiple_of(c * tk, tk)
            ge = jnp.where(key_scr[:, pl.ds(off, tk)] >= cand, 1.0, 0.0).astype(BF16)
            for p in range(tk // LANE):
                acc = acc + ge[:, p * LANE:(p + 1) * LANE]
            return acc

        acc = lax.fori_loop(0, n_chunks, count_chunk, jnp.zeros((tq, LANE), BF16))
        cnt = jnp.sum(acc.astype(F32), axis=1, keepdims=True)
        return jnp.where(cnt >= topk, cand, t)

    thr = lax.fori_loop(0, 32, bit_step, jnp.full((tq, 1), INT32_MIN, jnp.int32))

    def bias_chunk(c, carry):
        off = pl.multiple_of(c * tk, tk)
        ok = (key_scr[:, pl.ds(off, tk)] >= thr) & (off + col0 <= row)
        bias_scr[:, pl.ds(off, tk)] = jnp.where(ok, 0.0, MASK_NEG)
        return carry

    lax.fori_loop(0, n_chunks, bias_chunk, 0)

    rep = DSA_HEADS // DSA_KV_HEADS
    groups = [slice(g * LANE, (g + 1) * LANE) for g in range(DSA_KV_HEADS)]
    for r0 in range(0, tq, Q_BLOCK):
        rows = slice(r0, r0 + Q_BLOCK)
        qgs = [jnp.concatenate(
            [qb_ref[rows, (g * rep + j) * LANE:(g * rep + j + 1) * LANE] for j in range(rep)],
            axis=0) for g in range(DSA_KV_HEADS)]

        def att_chunk(c, carries, rows=rows, qgs=qgs):
            off = pl.multiple_of(c * tk, tk)
            bias = bias_scr[rows, pl.ds(off, tk)].astype(SCORE_DTYPE)
            bias = jnp.concatenate([bias] * rep, axis=0)
            scores = [_nt_dot(qg, kb_ref[pl.ds(off, tk), gsl], SCORE_DTYPE) + bias
                      for qg, gsl in zip(qgs, groups)]
            values = [vb_ref[pl.ds(off, tk), 2 * g * LANE:(2 * g + 2) * LANE]
                      for g in range(DSA_KV_HEADS)]
            return _softmax_steps(scores, values, carries)

        group_chunks = ((i * tq + r0 + Q_BLOCK) + tk - 1) // tk
        carries = lax.fori_loop(0, group_chunks, att_chunk,
                                _softmax_init(rep * Q_BLOCK, DSA_HEAD_DIM, DSA_KV_HEADS))
        for g, (_, l, acc) in enumerate(carries):
            o = acc / l
            for j in range(rep):
                hq = g * rep + j
                o_ref[rows, hq * LANE:(hq + 1) * LANE] = (
                    o[j * Q_BLOCK:(j + 1) * Q_BLOCK].astype(o_ref.dtype))


def _dsa_attention(qb, kb, vb, qix, klo, khi, w, b, t_pad, topk):
    tq = _pick(t_pad, (384, 128))
    tk = _pick(t_pad, (384, 128))
    assert t_pad // LANE <= 256, "bf16 partial counts must stay exact"
    kvw = DSA_KV_HEADS * DSA_HEAD_DIM
    r3 = lambda a: a.reshape(b, t_pad, a.shape[-1])
    qspec = lambda wd: pl.BlockSpec((None, tq, wd), lambda bb, i: (bb, i, 0))
    kspec = lambda wd: pl.BlockSpec((None, t_pad, wd), lambda bb, i: (bb, 0, 0),
                                    pipeline_mode=pl.Buffered(1))
    out = pl.pallas_call(
        functools.partial(_dsa_kernel, tq=tq, tk=tk, topk=topk),
        grid=(b, t_pad // tq),
        in_specs=[qspec(DSA_HEADS * DSA_HEAD_DIM), kspec(kvw), kspec(2 * kvw),
                  qspec(IDX_HEADS * IDX_DIM), kspec(LANE), kspec(LANE), qspec(LANE)],
        out_specs=qspec(DSA_HEADS * DSA_HEAD_DIM),
        out_shape=jax.ShapeDtypeStruct((b, t_pad, DSA_HEADS * DSA_HEAD_DIM), BF16),
        scratch_shapes=[pltpu.VMEM((tq, t_pad), jnp.int32), pltpu.VMEM((tq, t_pad), F32)],
        compiler_params=_params(("parallel", "arbitrary"), VMEM_RESIDENT_MIB),
        name="dsa_attention",
    )(r3(qb), r3(kb), r3(vb), r3(qix), r3(klo), r3(khi), r3(w))
    return out.reshape(b * t_pad, DSA_HEADS * DSA_HEAD_DIM)


def _sigmoid(x):
    return 1.0 / (1.0 + jnp.exp(-x))


def _branch_kernel(oa_ref, ob_ref, wa_ref, wb_ref, ga_ref, gb_ref, o_ref):
    ya = jnp.dot(oa_ref[...], wa_ref[...], preferred_element_type=F32)
    yb = jnp.dot(ob_ref[...], wb_ref[...], preferred_element_type=F32)
    o_ref[...] = (_sigmoid(ga_ref[...]) * ya + _sigmoid(gb_ref[...]) * yb).astype(o_ref.dtype)


def _residual_mm_kernel(a_ref, b_ref, h_ref, o_ref):
    o_ref[...] = h_ref[...] + jnp.dot(a_ref[...], b_ref[...], preferred_element_type=F32)


def _pack_rows(rows):
    assert len(rows) == SUBLANE
    shape = (SUBLANE, rows[0].shape[1])
    sub = lax.broadcasted_iota(jnp.int32, shape, 0)
    out = jnp.broadcast_to(rows[-1], shape)
    for k in range(SUBLANE - 2, -1, -1):
        out = jnp.where(sub == k, rows[k], out)
    return out


def _peer_select_kernel(qt_ref, keys_ref, lim_ref, rank_ref, e1_ref, e2_ref):
    neg_inf = -jnp.inf
    for h in range(PEER_HEADS):
        scores, tops = [], []
        rank2 = None
        for c in range(2):
            sl = slice((2 * h + c) * PEER_HALF, (2 * h + c + 1) * PEER_HALF)
            s = jnp.dot(keys_ref[sl, :], qt_ref[sl, :].astype(BF16), preferred_element_type=F32)
            scores.append(s)
            vals, x = [], s
            if c == 1:
                rank2 = jnp.full(s.shape, NOT_RANKED, F32)
            for k in range(PEER_TOPK):
                m = jnp.max(x, axis=0, keepdims=True)
                vals.append(m)
                found = x == m
                if c == 1:
                    rank2 = jnp.where(found, float(k), rank2)
                x = jnp.where(found, neg_inf, x)
            tops.append(vals)
        a, bv = tops
        b_packs = [_pack_rows(bv[k:k + SUBLANE]) for k in range(0, PEER_TOPK, SUBLANE)]
        sub = lax.broadcasted_iota(jnp.int32, b_packs[0].shape, 0)
        cands, owners, single = [], [], []
        for i in range(PEER_TOPK):
            partners = PEER_TOPK // (i + 1)
            if partners == 1:
                single.append(i)
                continue
            for k in range(0, partners, SUBLANE):
                cd = a[i] + b_packs[k // SUBLANE]
                if partners - k < SUBLANE:
                    cd = jnp.where(sub < partners - k, cd, neg_inf)
                cands.append(cd)
                owners.append([i] * SUBLANE)
        assert len(single) % SUBLANE == 0
        for k in range(0, len(single), SUBLANE):
            ids = single[k:k + SUBLANE]
            cands.append(_pack_rows([a[i] for i in ids]) + bv[0])
            owners.append(ids)
        originals = list(cands)
        v0 = a[0] + bv[0]
        z = jnp.zeros_like(v0)
        tau = v0
        for _ in range(PEER_TOPK):
            tau = jnp.max(functools.reduce(jnp.maximum, cands), axis=0, keepdims=True)
            z = z + jnp.exp(tau - v0)
            cands = [jnp.where(cd == tau, neg_inf, cd) for cd in cands]
        partners = [None] * PEER_TOPK
        for cd, ids in zip(originals, owners):
            reached = jnp.where(cd >= tau, 1.0, 0.0)
            if len(set(ids)) == 1:
                cnt = jnp.sum(reached, axis=0, keepdims=True)
                partners[ids[0]] = cnt if partners[ids[0]] is None else partners[ids[0]] + cnt
            else:
                for k, i in enumerate(ids):
                    partners[i] = jnp.sum(jnp.where(sub == k, reached, 0.0), axis=0, keepdims=True)
        lim = jnp.full(scores[0].shape, -1.0, F32)
        for i in range(PEER_TOPK):
            lim = jnp.where(scores[0] == a[i], partners[i] - 1.0, lim)
        hs = slice(h * PEER_N_KEYS, (h + 1) * PEER_N_KEYS)
        lim_ref[hs, :] = lim
        rank_ref[hs, :] = rank2.astype(rank_ref.dtype)
        e1_ref[hs, :] = jnp.exp(scores[0] - a[0])
        e2_ref[hs, :] = (jnp.exp(scores[1] - bv[0]) / z).astype(e2_ref.dtype)


def _peer_select(qt_arr, keys):
    rows, n = qt_arr.shape
    tm = _pick(n, (256, 128))
    hk = PEER_HEADS * PEER_N_KEYS
    big = lambda: pl.BlockSpec((hk, tm), lambda i: (0, i))
    return pl.pallas_call(
        _peer_select_kernel,
        grid=(n // tm,),
        in_specs=[pl.BlockSpec((rows, tm), lambda i: (0, i)),
                  pl.BlockSpec(keys.shape, lambda i: (0, 0))],
        out_specs=[big(), big(), big(), big()],
        out_shape=[jax.ShapeDtypeStruct((hk, n), F32), jax.ShapeDtypeStruct((hk, n), BF16),
                   jax.ShapeDtypeStruct((hk, n), F32), jax.ShapeDtypeStruct((hk, n), BF16)],
        compiler_params=_params(("parallel",), VMEM_ROWWISE_MIB),
        name="peer_select",
    )(qt_arr, keys)


def _gelu_tanh(x):
    c = np.float32(np.sqrt(2.0 / np.pi))
    return x * (0.5 * (1.0 + jnp.tanh(c * (x + 0.044715 * (x * x * x)))))


def _peer_main_kernel(hn_ref, u_ref, vt_ref, lim_ref, rank_ref, e1_ref, e2_ref, o_ref,
                      raw_scr, a_scr, *, rows_per_tile, n_tiles):
    j = pl.program_id(1)
    last_slot = (n_tiles - 1) % 2

    def first_matmul(slot):
        raw_scr[slot] = jnp.dot(u_ref[...], hn_ref[...], preferred_element_type=F32)

    def gate_and_project(slot, init):
        tm = o_ref.shape[1]
        for rr in range(rows_per_tile):
            r = (j - 1) * rows_per_tile + rr
            lim_rows = [jnp.broadcast_to(lim_ref[pl.ds(h * PEER_N_KEYS + r, 1), :],
                                         (GATE_ROWS, tm)).astype(BF16) for h in range(PEER_HEADS)]
            e1_rows = [jnp.broadcast_to(e1_ref[pl.ds(h * PEER_N_KEYS + r, 1), :],
                                        (GATE_ROWS, tm)).astype(BF16) for h in range(PEER_HEADS)]
            for c0 in range(0, PEER_N_KEYS, GATE_ROWS):
                gate = None
                for h in range(PEER_HEADS):
                    cs = slice(h * PEER_N_KEYS + c0, h * PEER_N_KEYS + c0 + GATE_ROWS)
                    hit = rank_ref[cs, :] <= lim_rows[h]
                    term = jnp.where(hit, e2_ref[cs, :] * e1_rows[h], 0.0)
                    gate = term if gate is None else gate + term
                es = slice(rr * PEER_N_KEYS + c0, rr * PEER_N_KEYS + c0 + GATE_ROWS)
                act = _gelu_tanh(raw_scr[slot, es, :]) * gate.astype(F32)
                a_scr[es, :] = act.astype(BF16)
        contrib = jnp.dot(vt_ref[...], a_scr[...], preferred_element_type=F32)
        o_ref[...] = contrib if init else o_ref[...] + contrib

    @pl.when(j == 0)
    def _():
        first_matmul(0)

    @pl.when(j == 1)
    def _():
        first_matmul(1)
        gate_and_project(0, init=True)

    @pl.when((j > 1) & (j < n_tiles) & (j % 2 == 0))
    def _():
        first_matmul(0)
        gate_and_project(1, init=False)

    @pl.when((j > 1) & (j < n_tiles) & (j % 2 == 1))
    def _():
        first_matmul(1)
        gate_and_project(0, init=False)

    @pl.when(j == n_tiles)
    def _():
        gate_and_project(last_slot, init=False)


def _peer_main(hn_t, u, vt, lim, rank2, e1, e2):
    d, n = hn_t.shape
    n_exp = u.shape[0]
    tm = _pick(n, ROW_TILES)
    te = PEER_EXPERT_TILE
    hk = PEER_HEADS * PEER_N_KEYS
    tok = lambda rows: pl.BlockSpec((rows, tm), lambda i, j: (0, i), pipeline_mode=pl.Buffered(1))
    n_tiles = n_exp // te
    assert n_tiles >= 2
    return pl.pallas_call(
        functools.partial(_peer_main_kernel, rows_per_tile=te // PEER_N_KEYS, n_tiles=n_tiles),
        grid=(n // tm, n_tiles + 1),
        in_specs=[tok(d),
                  pl.BlockSpec((te, d), lambda i, j: (jnp.minimum(j, n_tiles - 1), 0)),
                  pl.BlockSpec((d, te), lambda i, j: (0, jnp.maximum(j - 1, 0))),
                  tok(hk), tok(hk), tok(hk), tok(hk)],
        out_specs=pl.BlockSpec((d, tm), lambda i, j: (0, i)),
        out_shape=jax.ShapeDtypeStruct((d, n), F32),
        scratch_shapes=[pltpu.VMEM((2, te, tm), F32), pltpu.VMEM((te, tm), BF16)],
        compiler_params=_params(("parallel", "arbitrary"), VMEM_RESIDENT_MIB),
        name="peer_main",
    )(hn_t, u, vt, lim, rank2, e1, e2)


def _rmsnorm_t_kernel(x_ref, g_ref, o_ref):
    o_ref[...] = _rms(x_ref[...], g_ref[...]).T.astype(o_ref.dtype)


def _final_kernel(h_ref, p_ref, g_ref, o_ref):
    o_ref[...] = _rms(h_ref[...] + p_ref[...].T, g_ref[...])


def _in_proj_layout(d):
    segs = [("gate_a", d), ("gate_b", d), ("q_b", DSA_Q_WIDTH), ("c_q", MLA_Q_LORA),
            ("q_ix", IDX_Q_WIDTH), ("c_kv", MLA_KV_LORA), ("k_b", DSA_KV_WIDTH),
            ("v_b", DSA_KV_WIDTH), ("k_pe", LANE), ("k_ix", LANE), ("w_ix", LANE)]
    segs.sort(key=lambda s: -s[1])
    offs, o = {}, 0
    for name, wd in segs:
        assert o % wd == 0
        offs[name] = (o, wd)
        o += wd
    total = -(-o // COL_TILES[0]) * COL_TILES[0]
    return offs, total


def _build_w_in(w_in, d):
    src = {}
    o = 0
    for name, wd in (("c_q", MLA_Q_LORA), ("c_kv", MLA_KV_LORA), ("k_pe", MLA_ROPE),
                     ("q_b", DSA_Q_WIDTH), ("k_b", DSA_KV_WIDTH), ("v_b", DSA_KV_WIDTH),
                     ("q_ix", IDX_Q_WIDTH),
                     ("k_ix", IDX_DIM), ("w_ix", IDX_HEADS), ("gate_a", d), ("gate_b", d)):
        src[name] = w_in[:, o:o + wd]
        o += wd
    offs, total = _in_proj_layout(d)
    cols = []
    o = 0
    for name, (off, wd) in sorted(offs.items(), key=lambda kv: kv[1][0]):
        blk = src[name]
        cols.append(jnp.pad(blk, ((0, 0), (0, wd - blk.shape[1]))))
        o = off + wd
    cols.append(jnp.zeros((d, total - o), w_in.dtype))
    return jnp.concatenate(cols, axis=1).astype(BF16), offs


def kernel(x, meta_tokens, attn_norm_g, w_in, q_norm_g, w_uq, kv_norm_g, w_ukv, w_branch, w_out,
           ffn_norm_g, peer_w_q, peer_sub_keys, peer_u, peer_v, final_norm_g):
    b, s, d = x.shape
    assert attn_norm_g.shape[0] == 1, "single-layer block"
    assert s % Q_BLOCK == 0 and d % COL_TILES[1] == 0 and (d & (d - 1)) == 0
    assert N_META % SUBLANE == 0
    t = s + N_META
    t_pad = -(-t // Q_BLOCK) * Q_BLOCK
    topk = min(DSA_TOPK_MAX, s // 4)
    n = b * t_pad

    meta = jnp.broadcast_to(meta_tokens.astype(x.dtype)[None], (b, N_META, d))
    h0 = jnp.concatenate([meta, x, jnp.zeros((b, t_pad - t, d), x.dtype)], axis=1).reshape(n, d)

    w_in_p, offs = _build_w_in(w_in[0], d)
    wq = w_uq[0].reshape(MLA_Q_LORA, MLA_HEADS, MLA_NOPE + MLA_ROPE)
    wq = jnp.pad(wq, ((0, 0), (0, 0), (0, MLA_QK_PAD - MLA_NOPE - MLA_ROPE)))
    wq = wq.reshape(MLA_Q_LORA, MLA_HEADS * MLA_QK_PAD).astype(BF16)
    wkv = w_ukv[0].reshape(MLA_KV_LORA, MLA_HEADS, MLA_NOPE + MLA_V)
    wkv = jnp.concatenate([wkv[:, :, :MLA_NOPE].reshape(MLA_KV_LORA, -1),
                           wkv[:, :, MLA_NOPE:].reshape(MLA_KV_LORA, -1)], axis=1).astype(BF16)
    w_br = w_branch[0].astype(BF16)
    w_o = w_out[0].astype(BF16)
    w_pq_t = peer_w_q[0].T.astype(BF16)
    keys = peer_sub_keys[0].reshape(PEER_HEADS * 2 * PEER_N_KEYS, PEER_HALF).astype(BF16)
    u_tab = peer_u[0].astype(BF16)
    v_tab_t = peer_v[0].T.astype(BF16)

    mla_tabs = _rope_tables(t_pad, LANE, MLA_ROPE)
    dsa_tabs = _rope_tables(t_pad, LANE, DSA_ROT)
    idx_tabs = _rope_tables(t_pad, IDX_DIM, IDX_ROT)

    hn = _rmsnorm(h0, attn_norm_g[0], BF16)
    proj = _matmul(hn, w_in_p, F32, "in_proj", col_tiles=(2 * COL_TILES[0],) + COL_TILES,
                   vmem_mib=VMEM_RESIDENT_MIB)

    tm = _pick(t_pad, (384, 128))
    n_pos = t_pad // tm
    row = lambda wd, off: pl.BlockSpec((tm, wd), lambda i, o=off // wd: (i, o))
    seg = lambda name: row(offs[name][1], offs[name][0])
    tab = pl.BlockSpec((tm, LANE), lambda i: (i % n_pos, 0))
    full = lambda a: pl.BlockSpec(a.shape, lambda i: (0,) * a.ndim)
    out_row = lambda wd: pl.BlockSpec((tm, wd), lambda i: (i, 0))
    sds = lambda wd, dt: jax.ShapeDtypeStruct((n, wd), dt)

    qg = q_norm_g[0].reshape(1, -1).astype(F32)
    q_a = pl.pallas_call(
        _mla_q_kernel, grid=(n // tm,),
        in_specs=[seg("c_q"), full(qg), full(wq), tab, tab, tab],
        out_specs=out_row(MLA_HEADS * MLA_QK_PAD),
        out_shape=sds(MLA_HEADS * MLA_QK_PAD, BF16),
        compiler_params=_params(("parallel",), VMEM_MATMUL_MIB), name="mla_q_prep",
    )(proj, qg, wq, *mla_tabs)

    kg = kv_norm_g[0].reshape(1, -1).astype(F32)
    k_a, v_a = pl.pallas_call(
        _mla_kv_kernel, grid=(n // tm,),
        in_specs=[seg("c_kv"), full(kg), full(wkv), seg("k_pe"), tab, tab, tab],
        out_specs=[out_row(MLA_HEADS * MLA_QK_PAD), out_row(MLA_HEADS * 2 * MLA_V)],
        out_shape=[sds(MLA_HEADS * MLA_QK_PAD, BF16), sds(MLA_HEADS * 2 * MLA_V, BF16)],
        compiler_params=_params(("parallel",), VMEM_MATMUL_MIB), name="mla_kv_prep",
    )(proj, kg, wkv, proj, *mla_tabs)

    kvw = DSA_KV_HEADS * DSA_HEAD_DIM
    q_b, k_b, v_b, q_ix, k_lo, k_hi, w_ix = pl.pallas_call(
        _dsa_prep_kernel, grid=(n // tm,),
        in_specs=[seg("q_b"), seg("k_b"), seg("v_b"), seg("q_ix"), seg("k_ix"), seg("w_ix")]
        + [tab] * 6,
        out_specs=[out_row(DSA_Q_WIDTH), out_row(kvw), out_row(2 * kvw), out_row(IDX_Q_WIDTH),
                   out_row(LANE), out_row(LANE), out_row(LANE)],
        out_shape=[sds(DSA_Q_WIDTH, BF16), sds(kvw, BF16), sds(2 * kvw, BF16),
                   sds(IDX_Q_WIDTH, BF16),
                   sds(LANE, BF16), sds(LANE, BF16), sds(LANE, F32)],
        compiler_params=_params(("parallel",), VMEM_MATMUL_MIB), name="dsa_prep",
    )(proj, proj, proj, proj, proj, proj, *dsa_tabs, *idx_tabs)

    o_a = _mla_attention(q_a, k_a, v_a, b, t_pad)
    o_b = _dsa_attention(q_b, k_b, v_b, q_ix, k_lo, k_hi, w_ix, b, t_pad, topk)

    tmm = _pick(n, ROW_TILES)
    tn = _pick(d, COL_TILES[:2])
    ga_off, gb_off = offs["gate_a"][0] // tn, offs["gate_b"][0] // tn
    bw = MLA_HEADS * MLA_V
    merged = pl.pallas_call(
        _branch_kernel, grid=(d // tn, n // tmm),
        in_specs=[pl.BlockSpec((tmm, bw), lambda j, i: (i, 0)),
                  pl.BlockSpec((tmm, bw), lambda j, i: (i, 0)),
                  pl.BlockSpec((bw, tn), lambda j, i: (0, j)),
                  pl.BlockSpec((bw, tn), lambda j, i: (1, j)),
                  pl.BlockSpec((tmm, tn), lambda j, i: (i, ga_off + j)),
                  pl.BlockSpec((tmm, tn), lambda j, i: (i, gb_off + j))],
        out_specs=pl.BlockSpec((tmm, tn), lambda j, i: (i, j)),
        out_shape=jax.ShapeDtypeStruct((n, d), BF16),
        compiler_params=_params(("parallel", "parallel"), VMEM_MATMUL_MIB), name="branch_merge",
    )(o_a, o_b, w_br, w_br, proj, proj)

    h1 = pl.pallas_call(
        _residual_mm_kernel, grid=(d // tn, n // tmm),
        in_specs=[pl.BlockSpec((tmm, d), lambda j, i: (i, 0)),
                  pl.BlockSpec((d, tn), lambda j, i: (0, j)),
                  pl.BlockSpec((tmm, tn), lambda j, i: (i, j))],
        out_specs=pl.BlockSpec((tmm, tn), lambda j, i: (i, j)),
        out_shape=jax.ShapeDtypeStruct((n, d), F32),
        compiler_params=_params(("parallel", "parallel"), VMEM_MATMUL_MIB), name="out_proj",
    )(merged, w_o, h0)

    np_tok = b * s
    tf = _pick(s, (256, 128))
    tiles_per_seq = s // tf
    real_rows = pl.BlockSpec((pl.Element(tf), pl.Element(d)),
                             lambda bb, t: (pl.multiple_of(bb * t_pad + N_META + t * tf, SUBLANE), 0))
    vec = pl.BlockSpec((1, d), lambda bb, t: (0, 0))

    hn2_t = pl.pallas_call(
        _rmsnorm_t_kernel, grid=(b, tiles_per_seq),
        in_specs=[real_rows, vec],
        out_specs=pl.BlockSpec((d, tf), lambda bb, t: (0, bb * tiles_per_seq + t)),
        out_shape=jax.ShapeDtypeStruct((d, np_tok), BF16),
        compiler_params=_params(("parallel", "parallel"), VMEM_ROWWISE_MIB), name="ffn_norm_t",
    )(h1, ffn_norm_g[0].reshape(1, d).astype(F32))
    q_t = _matmul(w_pq_t, hn2_t, F32, "peer_query")
    lim, rank2, e1, e2 = _peer_select(q_t, keys)
    peer_t = _peer_main(hn2_t, u_tab, v_tab_t, lim, rank2, e1, e2)

    out = pl.pallas_call(
        _final_kernel, grid=(b, tiles_per_seq),
        in_specs=[real_rows,
                  pl.BlockSpec((d, tf), lambda bb, t: (0, bb * tiles_per_seq + t)), vec],
        out_specs=pl.BlockSpec((tf, d), lambda bb, t: (bb * tiles_per_seq + t, 0)),
        out_shape=jax.ShapeDtypeStruct((np_tok, d), F32),
        compiler_params=_params(("parallel", "parallel"), VMEM_ROWWISE_MIB), name="final_norm",
    )(h1, peer_t, final_norm_g.reshape(1, d).astype(F32))
    return out.reshape(b, s, d)
```

```python
import functools

import jax
import jax.numpy as jnp
import numpy as np
from jax import lax
from jax.experimental import pallas as pl
from jax.experimental.pallas import tpu as pltpu

N_META = 16
ROPE_THETA = 500000.0
Q_BLOCK = 128
NORM_EPS = 1e-6
MLA_HEADS = 16
MLA_Q_LORA = 1024
MLA_KV_LORA = 512
MLA_NOPE = 128
MLA_ROPE = 64
MLA_V = 128
DSA_HEADS = 16
DSA_KV_HEADS = 4
DSA_HEAD_DIM = 128
DSA_ROT = DSA_HEAD_DIM // 4
IDX_HEADS = 16
IDX_DIM = 64
IDX_ROT = IDX_DIM // 4
DSA_TOPK_MAX = 256
PEER_HEADS = 8
PEER_N_KEYS = 128
PEER_HALF = 128
PEER_TOPK = 16
DSA_Q_WIDTH = DSA_HEADS * DSA_HEAD_DIM
DSA_KV_WIDTH = DSA_KV_HEADS * DSA_HEAD_DIM
IDX_Q_WIDTH = IDX_HEADS * IDX_DIM

LANE = 128
SUBLANE = 8
MXU_WIDTH = 256
GATE_ROWS = 16
MLA_HEAD_GROUP = 4
LOG2_E = 1.4426950408889634
SCORE_DTYPE = jnp.bfloat16
NOT_RANKED = 127.0
MLA_QK_PAD = 2 * LANE
MASK_NEG = -1e30
INT32_MIN = -(2 ** 31)
MIB = 1024 * 1024
VMEM_ROWWISE_MIB = 40
VMEM_MATMUL_MIB = 48
VMEM_RESIDENT_MIB = 58
ROW_TILES = (512, 256, 128)
COL_TILES = (1024, 512, 256, 128)
PEER_EXPERT_TILE = 512

BF16 = jnp.bfloat16
F32 = jnp.float32


def _pick(n, prefs):
    for p in prefs:
        if n % p == 0:
            return p
    raise ValueError(f"no tile in {prefs} divides {n}")


def _params(sem, vmem_mib):
    return pltpu.CompilerParams(dimension_semantics=sem, vmem_limit_bytes=vmem_mib * MIB)


def _nt_dot(a, b, out_dtype=F32):
    out = lax.dot_general(a, b, (((1,), (1,)), ((), ())), preferred_element_type=F32)
    return out.astype(out_dtype)


def _rms(x, g):
    ms = jnp.mean(x * x, axis=-1, keepdims=True)
    return x * lax.rsqrt(ms + NORM_EPS) * g


def _rope(x, c, s_up, s_dn, half):
    return x * c + pltpu.roll(x, half, 1) * s_up + pltpu.roll(x, LANE - half, 1) * s_dn


def _rope_tables(t_pad, block, rot):
    half = rot // 2
    inv = ROPE_THETA ** (-jnp.arange(0, rot, 2, dtype=F32) / rot)
    ang = jnp.arange(t_pad, dtype=F32)[:, None] * inv[None, :]
    cos, sin = jnp.cos(ang), jnp.sin(ang)
    lane = np.arange(LANE) % block
    lo = lane < half
    hi = (lane >= half) & (lane < rot)
    idx = np.where(lo, lane, np.where(hi, lane - half, 0))
    cos_l, sin_l = cos[:, idx], sin[:, idx]
    c = jnp.where(lo | hi, cos_l, 1.0)
    s_up = jnp.where(hi, sin_l, 0.0)
    s_dn = jnp.where(lo, -sin_l, 0.0)
    return c.astype(F32), s_up.astype(F32), s_dn.astype(F32)


def _rmsnorm_kernel(x_ref, g_ref, o_ref):
    o_ref[...] = _rms(x_ref[...].astype(F32), g_ref[...]).astype(o_ref.dtype)


def _rmsnorm(x, g, out_dtype):
    m, d = x.shape
    tm = _pick(m, (256, 128))
    return pl.pallas_call(
        _rmsnorm_kernel,
        grid=(m // tm,),
        in_specs=[pl.BlockSpec((tm, d), lambda i: (i, 0)),
                  pl.BlockSpec((1, d), lambda i: (0, 0))],
        out_specs=pl.BlockSpec((tm, d), lambda i: (i, 0)),
        out_shape=jax.ShapeDtypeStruct((m, d), out_dtype),
        compiler_params=_params(("parallel",), VMEM_ROWWISE_MIB),
        name="rmsnorm",
    )(x, g.reshape(1, d).astype(F32))


def _mm_kernel(a_ref, b_ref, o_ref):
    o_ref[...] = jnp.dot(a_ref[...], b_ref[...], preferred_element_type=F32).astype(o_ref.dtype)


def _matmul(a, b, out_dtype, name, col_tiles=COL_TILES, vmem_mib=VMEM_MATMUL_MIB):
    m, k = a.shape
    _, n = b.shape
    tm = _pick(m, ROW_TILES)
    tn = _pick(n, col_tiles)
    return pl.pallas_call(
        _mm_kernel,
        grid=(n // tn, m // tm),
        in_specs=[pl.BlockSpec((tm, k), lambda j, i: (i, 0)),
                  pl.BlockSpec((k, tn), lambda j, i: (0, j))],
        out_specs=pl.BlockSpec((tm, tn), lambda j, i: (i, j)),
        out_shape=jax.ShapeDtypeStruct((m, n), out_dtype),
        compiler_params=_params(("parallel", "parallel"), vmem_mib),
        name=name,
    )(a, b)


def _mla_q_kernel(cq_ref, g_ref, w_ref, c_ref, su_ref, sd_ref, o_ref):
    xn = _rms(cq_ref[...], g_ref[...]).astype(BF16)
    q = jnp.dot(xn, w_ref[...], preferred_element_type=F32)
    q = q * ((MLA_NOPE + MLA_ROPE) ** -0.5 * LOG2_E)
    c, su, sd = c_ref[...], su_ref[...], sd_ref[...]
    for h in range(MLA_HEADS):
        base = h * MLA_QK_PAD
        o_ref[:, base:base + LANE] = q[:, base:base + LANE].astype(BF16)
        o_ref[:, base + LANE:base + 2 * LANE] = _rope(
            q[:, base + LANE:base + 2 * LANE], c, su, sd, MLA_ROPE // 2).astype(BF16)


def _mla_kv_kernel(ckv_ref, g_ref, w_ref, kpe_ref, c_ref, su_ref, sd_ref, k_ref, v_ref):
    xn = _rms(ckv_ref[...], g_ref[...]).astype(BF16)
    kv = jnp.dot(xn, w_ref[...], preferred_element_type=F32)
    kpe = _rope(kpe_ref[...], c_ref[...], su_ref[...], sd_ref[...], MLA_ROPE // 2).astype(BF16)
    nk = MLA_HEADS * MLA_NOPE
    for h in range(MLA_HEADS):
        base = h * MLA_QK_PAD
        k_ref[:, base:base + LANE] = kv[:, h * MLA_NOPE:(h + 1) * MLA_NOPE].astype(BF16)
        k_ref[:, base + LANE:base + 2 * LANE] = kpe
        v_ref[:, base:base + MLA_V] = kv[:, nk + h * MLA_V:nk + (h + 1) * MLA_V].astype(BF16)
        v_ref[:, base + MLA_V:base + 2 * MLA_V] = jnp.ones((kv.shape[0], MLA_V), BF16)


def _dsa_prep_kernel(qb_ref, kb_ref, vb_ref, qix_ref, kix_ref, wix_ref,
                     dc_ref, dsu_ref, dsd_ref, ic_ref, isu_ref, isd_ref,
                     qb_o, kb_o, vb_o, qix_o, klo_o, khi_o, w_o):
    dc, dsu, dsd = dc_ref[...], dsu_ref[...], dsd_ref[...]
    ic, isu, isd = ic_ref[...], isu_ref[...], isd_ref[...]
    for h in range(DSA_HEADS):
        sl = slice(h * LANE, (h + 1) * LANE)
        qb = _rope(qb_ref[:, sl], dc, dsu, dsd, DSA_ROT // 2)
        qb_o[:, sl] = (qb * (DSA_HEAD_DIM ** -0.5 * LOG2_E)).astype(BF16)
    for h in range(DSA_KV_HEADS):
        sl = slice(h * LANE, (h + 1) * LANE)
        kb_o[:, sl] = _rope(kb_ref[:, sl], dc, dsu, dsd, DSA_ROT // 2).astype(BF16)
    for h in range(DSA_KV_HEADS):
        vb_o[:, 2 * h * LANE:(2 * h + 1) * LANE] = vb_ref[:, h * LANE:(h + 1) * LANE].astype(BF16)
        vb_o[:, (2 * h + 1) * LANE:(2 * h + 2) * LANE] = jnp.ones((vb_ref.shape[0], LANE), BF16)
    for p in range(IDX_HEADS * IDX_DIM // LANE):
        sl = slice(p * LANE, (p + 1) * LANE)
        qix_o[:, sl] = _rope(qix_ref[:, sl], ic, isu, isd, IDX_ROT // 2).astype(BF16)
    kix = _rope(kix_ref[...], ic, isu, isd, IDX_ROT // 2)
    klo_o[...] = kix.astype(BF16)
    khi_o[...] = pltpu.roll(kix, IDX_DIM, 1).astype(BF16)
    w_o[...] = wix_ref[...] * (IDX_HEADS ** -0.5 * IDX_DIM ** -0.5)


def _softmax_steps(scores, values, carries):
    stats = []
    for s, (m, _, _) in zip(scores, carries):
        m_new = jnp.maximum(m, jnp.max(s, axis=1, keepdims=True).astype(F32))
        alpha = jnp.exp2(m - m_new)
        p = jnp.exp2(s - m_new.astype(s.dtype))
        stats.append((m_new, alpha, p.astype(BF16)))
    out = []
    for (m_new, alpha, p), v, (_, l, acc) in zip(stats, values, carries):
        pv = jnp.dot(p, v, preferred_element_type=F32)
        width = acc.shape[1]
        out.append((m_new, alpha * l + pv[:, width:width + 1], alpha * acc + pv[:, :width]))
    return tuple(out)


def _softmax_init(rows, width, chains):
    return tuple((jnp.full((rows, 1), MASK_NEG, F32), jnp.zeros((rows, 1), F32),
                  jnp.zeros((rows, width), F32)) for _ in range(chains))


def _mla_attn_kernel(q_ref, k_ref, v_ref, o_ref, *, tq, heads):
    i = pl.program_id(2)

    def chunk(j, carries, diagonal):
        off = pl.multiple_of(j * tq, tq)
        scores = [_nt_dot(q_ref[:, h * MLA_QK_PAD:(h + 1) * MLA_QK_PAD],
                          k_ref[pl.ds(off, tq), h * MLA_QK_PAD:(h + 1) * MLA_QK_PAD],
                          SCORE_DTYPE) for h in range(heads)]
        if diagonal:
            row = lax.broadcasted_iota(jnp.int32, (tq, tq), 0)
            col = lax.broadcasted_iota(jnp.int32, (tq, tq), 1)
            scores = [jnp.where(col <= row, s, MASK_NEG) for s in scores]
        values = [v_ref[pl.ds(off, tq), h * 2 * MLA_V:(h + 1) * 2 * MLA_V] for h in range(heads)]
        return _softmax_steps(scores, values, carries)

    carries = lax.fori_loop(0, i, lambda j, c: chunk(j, c, False),
                            _softmax_init(tq, MLA_V, heads))
    carries = chunk(i, carries, True)
    for h, (_, l, acc) in enumerate(carries):
        o_ref[:, h * MLA_V:(h + 1) * MLA_V] = (acc / l).astype(o_ref.dtype)


def _mla_attention(q, k, v, b, t_pad):
    tq = _pick(t_pad, (384, 128))
    hg = MLA_HEAD_GROUP
    q3 = q.reshape(b, t_pad, MLA_HEADS * MLA_QK_PAD)
    k3 = k.reshape(b, t_pad, MLA_HEADS * MLA_QK_PAD)
    v3 = v.reshape(b, t_pad, MLA_HEADS * 2 * MLA_V)
    out = pl.pallas_call(
        functools.partial(_mla_attn_kernel, tq=tq, heads=hg),
        grid=(b, MLA_HEADS // hg, t_pad // tq),
        in_specs=[pl.BlockSpec((None, tq, hg * MLA_QK_PAD), lambda bb, h, i: (bb, i, h)),
                  pl.BlockSpec((None, t_pad, hg * MLA_QK_PAD), lambda bb, h, i: (bb, 0, h)),
                  pl.BlockSpec((None, t_pad, hg * 2 * MLA_V), lambda bb, h, i: (bb, 0, h))],
        out_specs=pl.BlockSpec((None, tq, hg * MLA_V), lambda bb, h, i: (bb, i, h)),
        out_shape=jax.ShapeDtypeStruct((b, t_pad, MLA_HEADS * MLA_V), BF16),
        compiler_params=_params(("parallel", "parallel", "arbitrary"), VMEM_MATMUL_MIB),
        name="mla_attention",
    )(q3, k3, v3)
    return out.reshape(b * t_pad, MLA_HEADS * MLA_V)


def _dsa_kernel(qb_ref, kb_ref, vb_ref, qix_ref, klo_ref, khi_ref, w_ref, o_ref,
                key_scr, bias_scr, *, tq, tk, topk):
    i = pl.program_id(1)
    n_chunks = ((i + 1) * tq + tk - 1) // tk
    row = i * tq + lax.broadcasted_iota(jnp.int32, (tq, tk), 0)
    col0 = lax.broadcasted_iota(jnp.int32, (tq, tk), 1)
    w = w_ref[...]
    qix = qix_ref[...]

    def score_chunk(c, carry):
        off = pl.multiple_of(c * tk, tk)
        klo = klo_ref[pl.ds(off, tk), :]
        khi = khi_ref[pl.ds(off, tk), :]
        sc = jnp.zeros((tq, tk), F32)
        for h in range(IDX_HEADS):
            qp = qix[:, (h // 2) * LANE:(h // 2 + 1) * LANE]
            d = _nt_dot(qp, klo if h % 2 == 0 else khi)
            sc = sc + w[:, h:h + 1] * jnp.maximum(d, 0.0)
        sc = jnp.where(off + col0 <= row, sc, -jnp.inf)
        bits = pltpu.bitcast(sc, jnp.int32)
        key_scr[:, pl.ds(off, tk)] = bits ^ ((bits >> 31) & jnp.int32(0x7FFFFFFF))
        return carry

    lax.fori_loop(0, n_chunks, score_chunk, 0)

    def bit_step(bi, t):
        cand = t + (jnp.int32(1) << (31 - bi))

        def count_chunk(c, acc):
            off = pl.multiple_of(c * tk, tk)
            ge = jnp.where(key_scr[:, pl.ds(off, tk)] >= cand, 1.0, 0.0).astype(BF16)
            for p in range(tk // LANE):
                acc = acc + ge[:, p * LANE:(p + 1) * LANE]
            return acc

        acc = lax.fori_loop(0, n_chunks, count_chunk, jnp.zeros((tq, LANE), BF16))
        cnt = jnp.sum(acc.astype(F32), axis=1, keepdims=True)
        return jnp.where(cnt >= topk, cand, t)

    thr = lax.fori_loop(0, 32, bit_step, jnp.full((tq, 1), INT32_MIN, jnp.int32))

    def bias_chunk(c, carry):
        off = pl.multiple_of(c * tk, tk)
        ok = (key_scr[:, pl.ds(off, tk)] >= thr) & (off + col0 <= row)
        bias_scr[:, pl.ds(off, tk)] = jnp.where(ok, 0.0, MASK_NEG)
        return carry

    lax.fori_loop(0, n_chunks, bias_chunk, 0)

    rep = DSA_HEADS // DSA_KV_HEADS
    groups = [slice(g * LANE, (g + 1) * LANE) for g in range(DSA_KV_HEADS)]
    for r0 in range(0, tq, Q_BLOCK):
        rows = slice(r0, r0 + Q_BLOCK)

        def att_chunk(c, carries, rows=rows):
            qgs = [jnp.concatenate(
                [qb_ref[rows, (g * rep + j) * LANE:(g * rep + j + 1) * LANE] for j in range(rep)],
                axis=0) for g in range(DSA_KV_HEADS)]
            off = pl.multiple_of(c * tk, tk)
            bias = bias_scr[rows, pl.ds(off, tk)].astype(SCORE_DTYPE)
            bias = jnp.concatenate([bias] * rep, axis=0)
            scores = [_nt_dot(qg, kb_ref[pl.ds(off, tk), gsl], SCORE_DTYPE) + bias
                      for qg, gsl in zip(qgs, groups)]
            values = [vb_ref[pl.ds(off, tk), 2 * g * LANE:(2 * g + 2) * LANE]
                      for g in range(DSA_KV_HEADS)]
            return _softmax_steps(scores, values, carries)

        group_chunks = ((i * tq + r0 + Q_BLOCK) + tk - 1) // tk
        carries = lax.fori_loop(0, group_chunks, att_chunk,
                                _softmax_init(rep * Q_BLOCK, DSA_HEAD_DIM, DSA_KV_HEADS))
        for g, (_, l, acc) in enumerate(carries):
            o = acc / l
            for j in range(rep):
                hq = g * rep + j
                o_ref[rows, hq * LANE:(hq + 1) * LANE] = (
                    o[j * Q_BLOCK:(j + 1) * Q_BLOCK].astype(o_ref.dtype))


def _dsa_attention(qb, kb, vb, qix, klo, khi, w, b, t_pad, topk):
    tq = _pick(t_pad, (384, 128))
    tk = _pick(t_pad, (384, 128))
    assert t_pad // LANE <= 256, "bf16 partial counts must stay exact"
    kvw = DSA_KV_HEADS * DSA_HEAD_DIM
    r3 = lambda a: a.reshape(b, t_pad, a.shape[-1])
    qspec = lambda wd: pl.BlockSpec((None, tq, wd), lambda bb, i: (bb, i, 0))
    kspec = lambda wd: pl.BlockSpec((None, t_pad, wd), lambda bb, i: (bb, 0, 0),
                                    pipeline_mode=pl.Buffered(1))
    out = pl.pallas_call(
        functools.partial(_dsa_kernel, tq=tq, tk=tk, topk=topk),
        grid=(b, t_pad // tq),
        in_specs=[qspec(DSA_HEADS * DSA_HEAD_DIM), kspec(kvw), kspec(2 * kvw),
                  qspec(IDX_HEADS * IDX_DIM), kspec(LANE), kspec(LANE), qspec(LANE)],
        out_specs=qspec(DSA_HEADS * DSA_HEAD_DIM),
        out_shape=jax.ShapeDtypeStruct((b, t_pad, DSA_HEADS * DSA_HEAD_DIM), BF16),
        scratch_shapes=[pltpu.VMEM((tq, t_pad), jnp.int32), pltpu.VMEM((tq, t_pad), F32)],
        compiler_params=_params(("parallel", "arbitrary"), VMEM_RESIDENT_MIB),
        name="dsa_attention",
    )(r3(qb), r3(kb), r3(vb), r3(qix), r3(klo), r3(khi), r3(w))
    return out.reshape(b * t_pad, DSA_HEADS * DSA_HEAD_DIM)


def _sigmoid(x):
    return 1.0 / (1.0 + jnp.exp(-x))


def _branch_kernel(oa_ref, ob_ref, wa_ref, wb_ref, ga_ref, gb_ref, o_ref):
    ya = jnp.dot(oa_ref[...], wa_ref[...], preferred_element_type=F32)
    yb = jnp.dot(ob_ref[...], wb_ref[...], preferred_element_type=F32)
    o_ref[...] = (_sigmoid(ga_ref[...]) * ya + _sigmoid(gb_ref[...]) * yb).astype(o_ref.dtype)


def _residual_mm_kernel(a_ref, b_ref, h_ref, o_ref):
    o_ref[...] = h_ref[...] + jnp.dot(a_ref[...], b_ref[...], preferred_element_type=F32)


def _pack_rows(rows):
    assert len(rows) == SUBLANE
    shape = (SUBLANE, rows[0].shape[1])
    sub = lax.broadcasted_iota(jnp.int32, shape, 0)
    out = jnp.broadcast_to(rows[-1], shape)
    for k in range(SUBLANE - 2, -1, -1):
        out = jnp.where(sub == k, rows[k], out)
    return out


def _peer_select_kernel(qt_ref, keys_ref, lim_ref, rank_ref, e1_ref, e2_ref):
    neg_inf = -jnp.inf
    for h in range(PEER_HEADS):
        scores, tops = [], []
        rank2 = None
        for c in range(2):
            sl = slice((2 * h + c) * PEER_HALF, (2 * h + c + 1) * PEER_HALF)
            s = jnp.dot(keys_ref[sl, :], qt_ref[sl, :].astype(BF16), preferred_element_type=F32)
            scores.append(s)
            vals, x = [], s
            if c == 1:
                rank2 = jnp.full(s.shape, NOT_RANKED, F32)
            for k in range(PEER_TOPK):
                m = jnp.max(x, axis=0, keepdims=True)
                vals.append(m)
                found = x == m
                if c == 1:
                    rank2 = jnp.where(found, float(k), rank2)
                x = jnp.where(found, neg_inf, x)
            tops.append(vals)
        a, bv = tops
        b_packs = [_pack_rows(bv[k:k + SUBLANE]) for k in range(0, PEER_TOPK, SUBLANE)]
        sub = lax.broadcasted_iota(jnp.int32, b_packs[0].shape, 0)
        cands, owners, single = [], [], []
        for i in range(PEER_TOPK):
            partners = PEER_TOPK // (i + 1)
            if partners == 1:
                single.append(i)
                continue
            for k in range(0, partners, SUBLANE):
                cd = a[i] + b_packs[k // SUBLANE]
                if partners - k < SUBLANE:
                    cd = jnp.where(sub < partners - k, cd, neg_inf)
                cands.append(cd)
                owners.append([i] * SUBLANE)
        assert len(single) % SUBLANE == 0
        for k in range(0, len(single), SUBLANE):
            ids = single[k:k + SUBLANE]
            cands.append(_pack_rows([a[i] for i in ids]) + bv[0])
            owners.append(ids)
        originals = list(cands)
        v0 = a[0] + bv[0]
        z = jnp.zeros_like(v0)
        tau = v0
        for _ in range(PEER_TOPK):
            tau = jnp.max(functools.reduce(jnp.maximum, cands), axis=0, keepdims=True)
            z = z + jnp.exp(tau - v0)
            cands = [jnp.where(cd == tau, neg_inf, cd) for cd in cands]
        partners = [None] * PEER_TOPK
        for cd, ids in zip(originals, owners):
            reached = jnp.where(cd >= tau, 1.0, 0.0)
            if len(set(ids)) == 1:
                cnt = jnp.sum(reached, axis=0, keepdims=True)
                partners[ids[0]] = cnt if partners[ids[0]] is None else partners[ids[0]] + cnt
            else:
                for k, i in enumerate(ids):
                    partners[i] = jnp.sum(jnp.where(sub == k, reached, 0.0), axis=0, keepdims=True)
        lim = jnp.full(scores[0].shape, -1.0, F32)
        for i in range(PEER_TOPK):
            lim = jnp.where(scores[0] == a[i], partners[i] - 1.0, lim)
        hs = slice(h * PEER_N_KEYS, (h + 1) * PEER_N_KEYS)
        lim_ref[hs, :] = lim
        rank_ref[hs, :] = rank2.astype(rank_ref.dtype)
        e1_ref[hs, :] = jnp.exp(scores[0] - a[0])
        e2_ref[hs, :] = (jnp.exp(scores[1] - bv[0]) / z).astype(e2_ref.dtype)


def _peer_select(qt_arr, keys):
    rows, n = qt_arr.shape
    tm = _pick(n, (256, 128))
    hk = PEER_HEADS * PEER_N_KEYS
    big = lambda: pl.BlockSpec((hk, tm), lambda i: (0, i))
    return pl.pallas_call(
        _peer_select_kernel,
        grid=(n // tm,),
        in_specs=[pl.BlockSpec((rows, tm), lambda i: (0, i)),
                  pl.BlockSpec(keys.shape, lambda i: (0, 0))],
        out_specs=[big(), big(), big(), big()],
        out_shape=[jax.ShapeDtypeStruct((hk, n), F32), jax.ShapeDtypeStruct((hk, n), BF16),
                   jax.ShapeDtypeStruct((hk, n), F32), jax.ShapeDtypeStruct((hk, n), BF16)],
        compiler_params=_params(("parallel",), VMEM_ROWWISE_MIB),
        name="peer_select",
    )(qt_arr, keys)


def _gelu_tanh(x):
    c = np.float32(np.sqrt(2.0 / np.pi))
    return x * (0.5 * (1.0 + jnp.tanh(c * (x + 0.044715 * (x * x * x)))))


def _peer_main_kernel(hn_ref, u_ref, vt_ref, lim_ref, rank_ref, e1_ref, e2_ref, o_ref,
                      raw_scr, a_scr, *, rows_per_tile, n_tiles):
    j = pl.program_id(1)
    last_slot = (n_tiles - 1) % 2

    def first_matmul(slot):
        raw_scr[slot] = jnp.dot(u_ref[...], hn_ref[...], preferred_element_type=F32)

    def gate_and_project(slot, init):
        tm = o_ref.shape[1]
        for rr in range(rows_per_tile):
            r = (j - 1) * rows_per_tile + rr
            lim_rows = [jnp.broadcast_to(lim_ref[pl.ds(h * PEER_N_KEYS + r, 1), :],
                                         (GATE_ROWS, tm)).astype(BF16) for h in range(PEER_HEADS)]
            e1_rows = [jnp.broadcast_to(e1_ref[pl.ds(h * PEER_N_KEYS + r, 1), :],
                                        (GATE_ROWS, tm)).astype(BF16) for h in range(PEER_HEADS)]
            for c0 in range(0, PEER_N_KEYS, GATE_ROWS):
                gate = None
                for h in range(PEER_HEADS):
                    cs = slice(h * PEER_N_KEYS + c0, h * PEER_N_KEYS + c0 + GATE_ROWS)
                    hit = rank_ref[cs, :] <= lim_rows[h]
                    term = jnp.where(hit, e2_ref[cs, :] * e1_rows[h], 0.0)
                    gate = term if gate is None else gate + term
                es = slice(rr * PEER_N_KEYS + c0, rr * PEER_N_KEYS + c0 + GATE_ROWS)
                act = _gelu_tanh(raw_scr[slot, es, :]) * gate.astype(F32)
                a_scr[es, :] = act.astype(BF16)
        contrib = jnp.dot(vt_ref[...], a_scr[...], preferred_element_type=F32)
        o_ref[...] = contrib if init else o_ref[...] + contrib

    @pl.when(j == 0)
    def _():
        first_matmul(0)

    @pl.when(j == 1)
    def _():
        first_matmul(1)
        gate_and_project(0, init=True)

    @pl.when((j > 1) & (j < n_tiles) & (j % 2 == 0))
    def _():
        first_matmul(0)
        gate_and_project(1, init=False)

    @pl.when((j > 1) & (j < n_tiles) & (j % 2 == 1))
    def _():
        first_matmul(1)
        gate_and_project(0, init=False)

    @pl.when(j == n_tiles)
    def _():
        gate_and_project(last_slot, init=False)


def _peer_main(hn_t, u, vt, lim, rank2, e1, e2):
    d, n = hn_t.shape
    n_exp = u.shape[0]
    tm = _pick(n, ROW_TILES)
    te = PEER_EXPERT_TILE
    hk = PEER_HEADS * PEER_N_KEYS
    tok = lambda rows: pl.BlockSpec((rows, tm), lambda i, j: (0, i), pipeline_mode=pl.Buffered(1))
    n_tiles = n_exp // te
    assert n_tiles >= 2
    return pl.pallas_call(
        functools.partial(_peer_main_kernel, rows_per_tile=te // PEER_N_KEYS, n_tiles=n_tiles),
        grid=(n // tm, n_tiles + 1),
        in_specs=[tok(d),
                  pl.BlockSpec((te, d), lambda i, j: (jnp.minimum(j, n_tiles - 1), 0)),
                  pl.BlockSpec((d, te), lambda i, j: (0, jnp.maximum(j - 1, 0))),
                  tok(hk), tok(hk), tok(hk), tok(hk)],
        out_specs=pl.BlockSpec((d, tm), lambda i, j: (0, i)),
        out_shape=jax.ShapeDtypeStruct((d, n), F32),
        scratch_shapes=[pltpu.VMEM((2, te, tm), F32), pltpu.VMEM((te, tm), BF16)],
        compiler_params=_params(("parallel", "arbitrary"), VMEM_RESIDENT_MIB),
        name="peer_main",
    )(hn_t, u, vt, lim, rank2, e1, e2)


def _rmsnorm_t_kernel(x_ref, g_ref, o_ref):
    o_ref[...] = _rms(x_ref[...], g_ref[...]).T.astype(o_ref.dtype)


def _final_kernel(h_ref, p_ref, g_ref, o_ref):
    o_ref[...] = _rms(h_ref[...] + p_ref[...].T, g_ref[...])


def _in_proj_layout(d):
    segs = [("gate_a", d), ("gate_b", d), ("q_b", DSA_Q_WIDTH), ("c_q", MLA_Q_LORA),
            ("q_ix", IDX_Q_WIDTH), ("c_kv", MLA_KV_LORA), ("k_b", DSA_KV_WIDTH),
            ("v_b", DSA_KV_WIDTH), ("k_pe", LANE), ("k_ix", LANE), ("w_ix", LANE)]
    segs.sort(key=lambda s: -s[1])
    offs, o = {}, 0
    for name, wd in segs:
        assert o % wd == 0
        offs[name] = (o, wd)
        o += wd
    total = -(-o // COL_TILES[0]) * COL_TILES[0]
    return offs, total


def _build_w_in(w_in, d):
    src = {}
    o = 0
    for name, wd in (("c_q", MLA_Q_LORA), ("c_kv", MLA_KV_LORA), ("k_pe", MLA_ROPE),
                     ("q_b", DSA_Q_WIDTH), ("k_b", DSA_KV_WIDTH), ("v_b", DSA_KV_WIDTH),
                     ("q_ix", IDX_Q_WIDTH),
                     ("k_ix", IDX_DIM), ("w_ix", IDX_HEADS), ("gate_a", d), ("gate_b", d)):
        src[name] = w_in[:, o:o + wd]
        o += wd
    offs, total = _in_proj_layout(d)
    cols = []
    o = 0
    for name, (off, wd) in sorted(offs.items(), key=lambda kv: kv[1][0]):
        blk = src[name]
        cols.append(jnp.pad(blk, ((0, 0), (0, wd - blk.shape[1]))))
        o = off + wd
    cols.append(jnp.zeros((d, total - o), w_in.dtype))
    return jnp.concatenate(cols, axis=1).astype(BF16), offs


def kernel(x, meta_tokens, attn_norm_g, w_in, q_norm_g, w_uq, kv_norm_g, w_ukv, w_branch, w_out,
           ffn_norm_g, peer_w_q, peer_sub_keys, peer_u, peer_v, final_norm_g):
    b, s, d = x.shape
    assert attn_norm_g.shape[0] == 1, "single-layer block"
    assert s % Q_BLOCK == 0 and d % COL_TILES[1] == 0 and (d & (d - 1)) == 0
    assert N_META % SUBLANE == 0
    t = s + N_META
    t_pad = -(-t // Q_BLOCK) * Q_BLOCK
    topk = min(DSA_TOPK_MAX, s // 4)
    n = b * t_pad

    meta = jnp.broadcast_to(meta_tokens.astype(x.dtype)[None], (b, N_META, d))
    h0 = jnp.concatenate([meta, x, jnp.zeros((b, t_pad - t, d), x.dtype)], axis=1).reshape(n, d)

    w_in_p, offs = _build_w_in(w_in[0], d)
    wq = w_uq[0].reshape(MLA_Q_LORA, MLA_HEADS, MLA_NOPE + MLA_ROPE)
    wq = jnp.pad(wq, ((0, 0), (0, 0), (0, MLA_QK_PAD - MLA_NOPE - MLA_ROPE)))
    wq = wq.reshape(MLA_Q_LORA, MLA_HEADS * MLA_QK_PAD).astype(BF16)
    wkv = w_ukv[0].reshape(MLA_KV_LORA, MLA_HEADS, MLA_NOPE + MLA_V)
    wkv = jnp.concatenate([wkv[:, :, :MLA_NOPE].reshape(MLA_KV_LORA, -1),
                           wkv[:, :, MLA_NOPE:].reshape(MLA_KV_LORA, -1)], axis=1).astype(BF16)
    w_br = w_branch[0].astype(BF16)
    w_o = w_out[0].astype(BF16)
    w_pq_t = peer_w_q[0].T.astype(BF16)
    keys = peer_sub_keys[0].reshape(PEER_HEADS * 2 * PEER_N_KEYS, PEER_HALF).astype(BF16)
    u_tab = peer_u[0].astype(BF16)
    v_tab_t = peer_v[0].T.astype(BF16)

    mla_tabs = _rope_tables(t_pad, LANE, MLA_ROPE)
    dsa_tabs = _rope_tables(t_pad, LANE, DSA_ROT)
    idx_tabs = _rope_tables(t_pad, IDX_DIM, IDX_ROT)

    hn = _rmsnorm(h0, attn_norm_g[0], BF16)
    proj = _matmul(hn, w_in_p, F32, "in_proj", col_tiles=(2 * COL_TILES[0],) + COL_TILES,
                   vmem_mib=VMEM_RESIDENT_MIB)

    tm = _pick(t_pad, (384, 128))
    n_pos = t_pad // tm
    row = lambda wd, off: pl.BlockSpec((tm, wd), lambda i, o=off // wd: (i, o))
    seg = lambda name: row(offs[name][1], offs[name][0])
    tab = pl.BlockSpec((tm, LANE), lambda i: (i % n_pos, 0))
    full = lambda a: pl.BlockSpec(a.shape, lambda i: (0,) * a.ndim)
    out_row = lambda wd: pl.BlockSpec((tm, wd), lambda i: (i, 0))
    sds = lambda wd, dt: jax.ShapeDtypeStruct((n, wd), dt)

    qg = q_norm_g[0].reshape(1, -1).astype(F32)
    q_a = pl.pallas_call(
        _mla_q_kernel, grid=(n // tm,),
        in_specs=[seg("c_q"), full(qg), full(wq), tab, tab, tab],
        out_specs=out_row(MLA_HEADS * MLA_QK_PAD),
        out_shape=sds(MLA_HEADS * MLA_QK_PAD, BF16),
        compiler_params=_params(("parallel",), VMEM_MATMUL_MIB), name="mla_q_prep",
    )(proj, qg, wq, *mla_tabs)

    kg = kv_norm_g[0].reshape(1, -1).astype(F32)
    k_a, v_a = pl.pallas_call(
        _mla_kv_kernel, grid=(n // tm,),
        in_specs=[seg("c_kv"), full(kg), full(wkv), seg("k_pe"), tab, tab, tab],
        out_specs=[out_row(MLA_HEADS * MLA_QK_PAD), out_row(MLA_HEADS * 2 * MLA_V)],
        out_shape=[sds(MLA_HEADS * MLA_QK_PAD, BF16), sds(MLA_HEADS * 2 * MLA_V, BF16)],
        compiler_params=_params(("parallel",), VMEM_MATMUL_MIB), name="mla_kv_prep",
    )(proj, kg, wkv, proj, *mla_tabs)

    kvw = DSA_KV_HEADS * DSA_HEAD_DIM
    q_b, k_b, v_b, q_ix, k_lo, k_hi, w_ix = pl.pallas_call(
        _dsa_prep_kernel, grid=(n // tm,),
        in_specs=[seg("q_b"), seg("k_b"), seg("v_b"), seg("q_ix"), seg("k_ix"), seg("w_ix")]
        + [tab] * 6,
        out_specs=[out_row(DSA_Q_WIDTH), out_row(kvw), out_row(2 * kvw), out_row(IDX_Q_WIDTH),
                   out_row(LANE), out_row(LANE), out_row(LANE)],
        out_shape=[sds(DSA_Q_WIDTH, BF16), sds(kvw, BF16), sds(2 * kvw, BF16),
                   sds(IDX_Q_WIDTH, BF16),
                   sds(LANE, BF16), sds(LANE, BF16), sds(LANE, F32)],
        compiler_params=_params(("parallel",), VMEM_MATMUL_MIB), name="dsa_prep",
    )(proj, proj, proj, proj, proj, proj, *dsa_tabs, *idx_tabs)

    o_a = _mla_attention(q_a, k_a, v_a, b, t_pad)
    o_b = _dsa_attention(q_b, k_b, v_b, q_ix, k_lo, k_hi, w_ix, b, t_pad, topk)

    tmm = _pick(n, ROW_TILES)
    tn = _pick(d, COL_TILES[:2])
    ga_off, gb_off = offs["gate_a"][0] // tn, offs["gate_b"][0] // tn
    bw = MLA_HEADS * MLA_V
    merged = pl.pallas_call(
        _branch_kernel, grid=(d // tn, n // tmm),
        in_specs=[pl.BlockSpec((tmm, bw), lambda j, i: (i, 0)),
                  pl.BlockSpec((tmm, bw), lambda j, i: (i, 0)),
                  pl.BlockSpec((bw, tn), lambda j, i: (0, j)),
                  pl.BlockSpec((bw, tn), lambda j, i: (1, j)),
                  pl.BlockSpec((tmm, tn), lambda j, i: (i, ga_off + j)),
                  pl.BlockSpec((tmm, tn), lambda j, i: (i, gb_off + j))],
        out_specs=pl.BlockSpec((tmm, tn), lambda j, i: (i, j)),
        out_shape=jax.ShapeDtypeStruct((n, d), BF16),
        compiler_params=_params(("parallel", "parallel"), VMEM_MATMUL_MIB), name="branch_merge",
    )(o_a, o_b, w_br, w_br, proj, proj)

    h1 = pl.pallas_call(
        _residual_mm_kernel, grid=(d // tn, n // tmm),
        in_specs=[pl.BlockSpec((tmm, d), lambda j, i: (i, 0)),
                  pl.BlockSpec((d, tn), lambda j, i: (0, j)),
                  pl.BlockSpec((tmm, tn), lambda j, i: (i, j))],
        out_specs=pl.BlockSpec((tmm, tn), lambda j, i: (i, j)),
        out_shape=jax.ShapeDtypeStruct((n, d), F32),
        compiler_params=_params(("parallel", "parallel"), VMEM_MATMUL_MIB), name="out_proj",
    )(merged, w_o, h0)

    np_tok = b * s
    tf = _pick(s, (256, 128))
    tiles_per_seq = s // tf
    real_rows = pl.BlockSpec((pl.Element(tf), pl.Element(d)),
                             lambda bb, t: (pl.multiple_of(bb * t_pad + N_META + t * tf, SUBLANE), 0))
    vec = pl.BlockSpec((1, d), lambda bb, t: (0, 0))

    hn2_t = pl.pallas_call(
        _rmsnorm_t_kernel, grid=(b, tiles_per_seq),
        in_specs=[real_rows, vec],
        out_specs=pl.BlockSpec((d, tf), lambda bb, t: (0, bb * tiles_per_seq + t)),
        out_shape=jax.ShapeDtypeStruct((d, np_tok), BF16),
        compiler_params=_params(("parallel", "parallel"), VMEM_ROWWISE_MIB), name="ffn_norm_t",
    )(h1, ffn_norm_g[0].reshape(1, d).astype(F32))
    q_t = _matmul(w_pq_t, hn2_t, F32, "peer_query")
    lim, rank2, e1, e2 = _peer_select(q_t, keys)
    peer_t = _peer_main(hn2_t, u_tab, v_tab_t, lim, rank2, e1, e2)

    out = pl.pallas_call(
        _final_kernel, grid=(b, tiles_per_seq),
        in_specs=[real_rows,
                  pl.BlockSpec((d, tf), lambda bb, t: (0, bb * tiles_per_seq + t)), vec],
        out_specs=pl.BlockSpec((tf, d), lambda bb, t: (bb * tiles_per_seq + t, 0)),
        out_shape=jax.ShapeDtypeStruct((np_tok, d), F32),
        compiler_params=_params(("parallel", "parallel"), VMEM_ROWWISE_MIB), name="final_norm",
    )(h1, peer_t, final_norm_g.reshape(1, d).astype(F32))
    return out.reshape(b, s, d)
```
